```python
import math
import jax, jax.numpy as jnp
from jax import lax
import numpy as np

D_MODEL = 2048
BATCH = 4
SEQ = 2048
DEPTH = 4
DEC_BATCH = 128
DEC_SEQ = 4
PAST_LEN = 16384
PAGE_SIZE = 128

N_EVEN = (DEPTH + 1) // 2
N_ODD = DEPTH // 2
MIX_HALF = D_MODEL // 2
RET_HEADS = 4
RET_DK = MIX_HALF // RET_HEADS
RET_DV = MIX_HALF // RET_HEADS
RET_CHUNK = 128
ROPE_BASE = 10000.0
LRU_WIDTH = MIX_HALF
LRU_HEADS = 8
LRU_BLOCK = LRU_WIDTH // LRU_HEADS
LRU_C = 8.0
CONV_W = 4
IN_WIDTH = 6 * MIX_HALF
SSM_GROUP = 16
SSM_GROUPS = D_MODEL // SSM_GROUP
SSM_P = 64
SSM_CHUNK = 128
FFN_HIDDEN = ((8 * D_MODEL + 3 * 256 - 1) // (3 * 256)) * 256
EPS = 1e-6

kernel_name = 'hybrid_retention_rglru_s5_decode_step'


def rmsnorm(x, g):
    xf = x.astype(jnp.float32)
    y = xf * lax.rsqrt(jnp.mean(xf * xf, axis=-1, keepdims=True) + EPS)
    return (y * g.astype(jnp.float32)).astype(x.dtype)


def rotary(x, pos):
    half = x.shape[-1] // 2
    inv = 1.0 / jnp.power(ROPE_BASE, jnp.linspace(0.0, 1.0, half, dtype=jnp.float32))
    ang = pos.astype(jnp.float32)[:, None] * inv[None, :]
    cos = jnp.cos(ang)[None, :, None, :]
    sin = jnp.sin(ang)[None, :, None, :]
    x1, x2 = x[..., :half], x[..., half:]
    return jnp.concatenate([x1 * cos - x2 * sin, x2 * cos + x1 * sin], axis=-1)


def retention(q, k, v, s0, pos0):
    B, T, H, dk = q.shape
    C = math.gcd(T, RET_CHUNK)
    NC = T // C
    pos = pos0 + jnp.arange(T)
    qf = rotary(q.astype(jnp.float32), pos)
    kf = rotary(k.astype(jnp.float32), pos) * (dk ** -0.5)
    vf = v.astype(jnp.float32)
    log_g = jnp.log1p(-jnp.exp2(-5.0 - jnp.arange(H, dtype=jnp.float32)))
    idx = jnp.arange(C, dtype=jnp.float32)
    diff = idx[:, None] - idx[None, :]
    mask = jnp.where(diff[None] >= 0, jnp.exp(jnp.maximum(diff, 0.0)[None] * log_g[:, None, None]), 0.0)
    q_dec = jnp.exp((idx[:, None] + 1.0) * log_g[None, :])
    k_dec = jnp.exp((C - 1.0 - idx)[:, None] * log_g[None, :])
    chunk_dec = jnp.exp(C * log_g)

    def to_chunks(a):
        return a.reshape(B, NC, C, H, a.shape[-1]).swapaxes(0, 1)

    def step(s, inp):
        qc, kc, vc = inp
        scores = jnp.einsum('bnhd,bmhd->bhnm', qc, kc) * mask
        o = (jnp.einsum('bhnm,bmhe->bnhe', scores, vc)
             + jnp.einsum('bnhd,bhde->bnhe', qc, s) * q_dec[None, :, :, None])
        s = (s * chunk_dec[None, :, None, None]
             + jnp.einsum('bmhd,bmhe->bhde', kc * k_dec[None, :, :, None], vc))
        return s, o

    s, o = lax.scan(step, s0.astype(jnp.float32), (to_chunks(qf), to_chunks(kf), to_chunks(vf)))
    o = o.swapaxes(0, 1).reshape(B, T, H, vf.shape[-1])
    return o, s


def causal_conv(x, buf, w, b):
    T = x.shape[1]
    xp = jnp.concatenate([buf.astype(jnp.float32), x.astype(jnp.float32)], axis=1)
    wf = w.astype(jnp.float32)
    y = b.astype(jnp.float32) + sum(xp[:, i:i + T] * wf[i] for i in range(CONV_W))
    new_buf = xp[:, xp.shape[1] - (CONV_W - 1):]
    return y, new_buf


def _lin_combine(e1, e2):
    a1, b1 = e1
    a2, b2 = e2
    return a1 * a2, a2 * b1 + b2


def rglru(x, h0, wa, ba, wx, bx, lam):
    B, T, W = x.shape
    xb = x.reshape(B, T, LRU_HEADS, LRU_BLOCK)
    r = jax.nn.sigmoid(jnp.einsum('btnk,nkj->btnj', xb, wa.astype(jnp.float32)).reshape(B, T, W) + ba.astype(jnp.float32))
    i = jax.nn.sigmoid(jnp.einsum('btnk,nkj->btnj', xb, wx.astype(jnp.float32)).reshape(B, T, W) + bx.astype(jnp.float32))
    log_a = -LRU_C * r * jax.nn.softplus(-lam.astype(jnp.float32))
    a = jnp.exp(log_a)
    mult = jnp.sqrt(jnp.maximum(-jnp.expm1(2.0 * log_a), 0.0))
    bterm = mult * i * x
    bterm = bterm.at[:, 0].add(a[:, 0] * h0.astype(jnp.float32))
    _, h = lax.associative_scan(_lin_combine, (a, bterm), axis=1)
    return h, h[:, -1]


def _cplx_combine(e1, e2):
    ar1, ai1, br1, bi1 = e1
    ar2, ai2, br2, bi2 = e2
    return (ar2 * ar1 - ai2 * ai1, ar2 * ai1 + ai2 * ar1,
            ar2 * br1 - ai2 * bi1 + br2, ar2 * bi1 + ai2 * br1 + bi2)


def s5(u, h0_re, h0_im, a_re, a_im, b_re, b_im, c_re, c_im, d, log_dt):
    B, T, Dm = u.shape
    f32 = jnp.float32
    uf = u.astype(f32)
    a_re, a_im = a_re.astype(f32), a_im.astype(f32)
    dt = jnp.exp(log_dt.astype(f32))[:, None]
    mag = jnp.exp(a_re * dt)
    abr = mag * jnp.cos(a_im * dt)
    abi = mag * jnp.sin(a_im * dt)
    den = a_re * a_re + a_im * a_im
    nr, ni = abr - 1.0, abi
    fr = (nr * a_re + ni * a_im) / den
    fi = (ni * a_re - nr * a_im) / den
    b_re, b_im = b_re.astype(f32), b_im.astype(f32)
    bbr = fr[..., None] * b_re - fi[..., None] * b_im
    bbi = fr[..., None] * b_im + fi[..., None] * b_re
    c_re, c_im = c_re.astype(f32), c_im.astype(f32)
    C = math.gcd(T, SSM_CHUNK)
    NC = T // C
    abr_c = jnp.broadcast_to(abr, (B, C, SSM_GROUPS, SSM_P))
    abi_c = jnp.broadcast_to(abi, (B, C, SSM_GROUPS, SSM_P))

    def step(carry, uc):
        hr, hi = carry
        bu_r = jnp.einsum('bcgk,gpk->bcgp', uc, bbr)
        bu_i = jnp.einsum('bcgk,gpk->bcgp', uc, bbi)
        bu_r = bu_r.at[:, 0].add(abr * hr - abi * hi)
        bu_i = bu_i.at[:, 0].add(abr * hi + abi * hr)
        _, _, sr, si = lax.associative_scan(_cplx_combine, (abr_c, abi_c, bu_r, bu_i), axis=1)
        y = jnp.einsum('bcgp,gkp->bcgk', sr, c_re) - jnp.einsum('bcgp,gkp->bcgk', si, c_im)
        return (sr[:, -1], si[:, -1]), y

    uc_all = uf.reshape(B, NC, C, SSM_GROUPS, SSM_GROUP).swapaxes(0, 1)
    (hr, hi), y = lax.scan(step, (h0_re.astype(f32), h0_im.astype(f32)), uc_all)
    y = y.swapaxes(0, 1).reshape(B, T, Dm) + d.astype(f32) * uf
    return y, hr, hi


def swiglu(x, w_gu, w_down):
    h = x @ w_gu
    g, u = jnp.split(h, 2, axis=-1)
    return (jax.nn.silu(g) * u) @ w_down


def even_mixer(x, s_ret0, h0, conv0, pos0, g, w_in, conv_w, conv_b, wa, ba, wx, bx, lam, w_out):
    B, T, _ = x.shape
    proj = rmsnorm(x, g) @ w_in
    q, k, v, gr, xl, yl = jnp.split(proj, 6, axis=-1)
    q = q.reshape(B, T, RET_HEADS, RET_DK)
    k = k.reshape(B, T, RET_HEADS, RET_DK)
    v = v.reshape(B, T, RET_HEADS, RET_DV)
    o, s_ret = retention(q, k, v, s0=s_ret0, pos0=pos0)
    o = o * lax.rsqrt(jnp.mean(o * o, axis=-1, keepdims=True) + EPS)
    o = o.reshape(B, T, RET_HEADS * RET_DV) * jax.nn.silu(gr.astype(jnp.float32))
    xc, conv_new = causal_conv(xl, conv0, conv_w, conv_b)
    hs, h_last = rglru(xc, h0, wa, ba, wx, bx, lam)
    lo = jax.nn.gelu(yl.astype(jnp.float32)) * hs
    out = jnp.concatenate([o, lo], axis=-1).astype(x.dtype) @ w_out
    return out, s_ret, h_last, conv_new


def odd_mixer(x, h0_re, h0_im, g, a_re, a_im, b_re, b_im, c_re, c_im, d, log_dt, w_glu, b_glu):
    u = rmsnorm(x, g)
    y, hr, hi = s5(u, h0_re, h0_im, a_re, a_im, b_re, b_im, c_re, c_im, d, log_dt)
    y = jax.nn.gelu(y).astype(x.dtype)
    z = y @ w_glu + b_glu
    z1, z2 = jnp.split(z, 2, axis=-1)
    return z1 * jax.nn.sigmoid(z2), hr, hi


def setup_inputs(seed: int = 0) -> dict:
    key = jax.random.key(seed)
    ks = iter(jax.random.split(key, 40))
    f32 = jnp.float32

    def nrm(shape, scale):
        return jax.random.normal(next(ks), shape, f32) * scale

    def gain(shape):
        return 1.0 + nrm(shape, 0.02)

    x_prompt = nrm((BATCH, SEQ, D_MODEL), 1.0)
    x_sample = nrm((DEC_BATCH, DEC_SEQ, D_MODEL), 1.0)
    state_ret = nrm((N_EVEN, DEC_BATCH, RET_HEADS, RET_DK, RET_DV), 0.1)
    state_lru = nrm((N_EVEN, DEC_BATCH, LRU_WIDTH), 0.5)
    state_conv = nrm((N_EVEN, DEC_BATCH, CONV_W - 1, LRU_WIDTH), 1.0)
    state_ssm_re = nrm((N_ODD, DEC_BATCH, SSM_GROUPS, SSM_P), 0.1)
    state_ssm_im = nrm((N_ODD, DEC_BATCH, SSM_GROUPS, SSM_P), 0.1)
    norm_mix_even = gain((N_EVEN, D_MODEL))
    w_in_even = nrm((N_EVEN, D_MODEL, IN_WIDTH), D_MODEL ** -0.5)
    lru_conv_w = nrm((N_EVEN, CONV_W, LRU_WIDTH), CONV_W ** -0.5)
    lru_conv_b = nrm((N_EVEN, LRU_WIDTH), 0.01)
    lru_wa = nrm((N_EVEN, LRU_HEADS, LRU_BLOCK, LRU_BLOCK), LRU_BLOCK ** -0.5)
    lru_ba = nrm((N_EVEN, LRU_WIDTH), 0.01)
    lru_wx = nrm((N_EVEN, LRU_HEADS, LRU_BLOCK, LRU_BLOCK), LRU_BLOCK ** -0.5)
    lru_bx = nrm((N_EVEN, LRU_WIDTH), 0.01)
    a_pow = jax.random.uniform(next(ks), (N_EVEN, LRU_WIDTH), f32, 0.9, 0.999)
    s = a_pow ** (1.0 / LRU_C)
    lru_lambda = jnp.log(s) - jnp.log1p(-s)
    w_out_even = nrm((N_EVEN, 2 * MIX_HALF, D_MODEL), (2 * MIX_HALF) ** -0.5)
    norm_mix_odd = gain((N_ODD, D_MODEL))
    ssm_a_re = -0.5 + nrm((N_ODD, SSM_GROUPS, SSM_P), 0.01)
    ssm_a_im = math.pi * jnp.arange(SSM_P, dtype=f32) + nrm((N_ODD, SSM_GROUPS, SSM_P), 0.01)
    ssm_b_re = nrm((N_ODD, SSM_GROUPS, SSM_P, SSM_GROUP), (2 * SSM_GROUP) ** -0.5)
    ssm_b_im = nrm((N_ODD, SSM_GROUPS, SSM_P, SSM_GROUP), (2 * SSM_GROUP) ** -0.5)
    ssm_c_re = nrm((N_ODD, SSM_GROUPS, SSM_GROUP, SSM_P), SSM_P ** -0.5)
    ssm_c_im = nrm((N_ODD, SSM_GROUPS, SSM_GROUP, SSM_P), SSM_P ** -0.5)
    ssm_d = nrm((N_ODD, D_MODEL), 0.5)
    ssm_log_dt = jax.random.uniform(next(ks), (N_ODD, SSM_GROUPS), f32, math.log(1e-3), math.log(1e-1))
    w_glu = nrm((N_ODD, D_MODEL, 2 * D_MODEL), D_MODEL ** -0.5)
    b_glu = nrm((N_ODD, 2 * D_MODEL), 0.01)
    norm_ffn = gain((DEPTH, D_MODEL))
    w_ffn_gu = nrm((DEPTH, D_MODEL, 2 * FFN_HIDDEN), D_MODEL ** -0.5)
    w_ffn_down = nrm((DEPTH, FFN_HIDDEN, D_MODEL), FFN_HIDDEN ** -0.5)
    norm_final = gain((D_MODEL,))
    return {
        'x_prompt': x_prompt, 'x_sample': x_sample,
        'state_ret': state_ret, 'state_lru': state_lru, 'state_conv': state_conv,
        'state_ssm_re': state_ssm_re, 'state_ssm_im': state_ssm_im,
        'norm_mix_even': norm_mix_even, 'w_in_even': w_in_even,
        'lru_conv_w': lru_conv_w, 'lru_conv_b': lru_conv_b,
        'lru_wa': lru_wa, 'lru_ba': lru_ba, 'lru_wx': lru_wx, 'lru_bx': lru_bx,
        'lru_lambda': lru_lambda, 'w_out_even': w_out_even,
        'norm_mix_odd': norm_mix_odd, 'ssm_a_re': ssm_a_re, 'ssm_a_im': ssm_a_im,
        'ssm_b_re': ssm_b_re, 'ssm_b_im': ssm_b_im, 'ssm_c_re': ssm_c_re, 'ssm_c_im': ssm_c_im,
        'ssm_d': ssm_d, 'ssm_log_dt': ssm_log_dt, 'w_glu': w_glu, 'b_glu': b_glu,
        'norm_ffn': norm_ffn, 'w_ffn_gu': w_ffn_gu, 'w_ffn_down': w_ffn_down,
        'norm_final': norm_final,
    }


def reference(x_prompt, x_sample, state_ret, state_lru, state_conv, state_ssm_re, state_ssm_im,
              norm_mix_even, w_in_even, lru_conv_w, lru_conv_b, lru_wa, lru_ba, lru_wx, lru_bx,
              lru_lambda, w_out_even, norm_mix_odd, ssm_a_re, ssm_a_im, ssm_b_re, ssm_b_im,
              ssm_c_re, ssm_c_im, ssm_d, ssm_log_dt, w_glu, b_glu, norm_ffn, w_ffn_gu, w_ffn_down,
              norm_final):
    def trunk(x, ret0, lru0, conv0, sre0, sim0, pos0):
        rets, lrus, convs, sres, sims = [], [], [], [], []
        for layer in range(DEPTH):
            if layer % 2 == 0:
                e = layer // 2
                out, s_r, s_l, s_c = even_mixer(
                    x, ret0[e], lru0[e], conv0[e], pos0, norm_mix_even[e], w_in_even[e],
                    lru_conv_w[e], lru_conv_b[e], lru_wa[e], lru_ba[e], lru_wx[e], lru_bx[e],
                    lru_lambda[e], w_out_even[e])
                rets.append(s_r)
                lrus.append(s_l)
                convs.append(s_c)
            else:
                o = layer // 2
                out, s_re, s_im = odd_mixer(
                    x, sre0[o], sim0[o], norm_mix_odd[o], ssm_a_re[o], ssm_a_im[o], ssm_b_re[o],
                    ssm_b_im[o], ssm_c_re[o], ssm_c_im[o], ssm_d[o], ssm_log_dt[o], w_glu[o], b_glu[o])
                sres.append(s_re)
                sims.append(s_im)
            x = x + out.astype(x.dtype)
            x = x + swiglu(rmsnorm(x, norm_ffn[layer]), w_ffn_gu[layer], w_ffn_down[layer])
        y = rmsnorm(x, norm_final)
        return (y,
                jnp.stack(rets).astype(state_ret.dtype),
                jnp.stack(lrus).astype(state_lru.dtype),
                jnp.stack(convs).astype(state_conv.dtype),
                jnp.stack(sres).astype(state_ssm_re.dtype),
                jnp.stack(sims).astype(state_ssm_im.dtype))

    bp = x_prompt.shape[0]
    ret0_p = jnp.zeros((N_EVEN, bp, RET_HEADS, RET_DK, RET_DV), state_ret.dtype)
    lru0_p = jnp.zeros((N_EVEN, bp, LRU_WIDTH), state_lru.dtype)
    conv0_p = jnp.zeros((N_EVEN, bp, CONV_W - 1, LRU_WIDTH), state_conv.dtype)
    sre0_p = jnp.zeros((N_ODD, bp, SSM_GROUPS, SSM_P), state_ssm_re.dtype)
    sim0_p = jnp.zeros((N_ODD, bp, SSM_GROUPS, SSM_P), state_ssm_im.dtype)

    y_prompt, ret_p, lru_p, conv_p, ssm_re_p, ssm_im_p = trunk(
        x_prompt, ret0_p, lru0_p, conv0_p, sre0_p, sim0_p, 0)
    y_sample, ret_s, lru_s, conv_s, ssm_re_s, ssm_im_s = trunk(
        x_sample, state_ret, state_lru, state_conv, state_ssm_re, state_ssm_im, PAST_LEN)
    return (y_prompt, y_sample, ret_p, ret_s, lru_p, lru_s, conv_p, conv_s,
            ssm_re_p, ssm_re_s, ssm_im_p, ssm_im_s)
```

```python
import functools
import math

import jax
import jax.numpy as jnp
from jax import lax
from jax.experimental import pallas as pl
from jax.experimental.pallas import tpu as pltpu

F32 = jnp.float32
BF16 = jnp.bfloat16

EPS = 1e-6
PAST_LEN = 16384
ROPE_BASE = 10000.0
RET_HEADS = 4
RET_CHUNK = 128
LRU_HEADS = 8
LRU_C = 8.0
CONV_W = 4
SSM_GROUP = 16
SSM_P = 64
SSM_CHUNK = 128

V7X_SUBLANES = 8
V7X_LANES = 128
V7X_MXU_DIM = 256
V7X_VMEM_BYTES = 64 * 1024 * 1024
VMEM_LIMIT_CAP = V7X_VMEM_BYTES - 6 * 1024 * 1024

SSM_BLOCK_GROUPS = V7X_MXU_DIM // SSM_GROUP
SSM_BLOCK_STATES = SSM_BLOCK_GROUPS * SSM_P

RET_SAMPLE_ROWS = 128


def _pick_tile(n, target, mult):
    best = None
    for t in range(mult, min(n, target) + 1, mult):
        if n % t == 0:
            best = t
    assert best is not None, (n, target, mult)
    return best


def _params(semantics, vmem_bytes):
    limit = min(int(vmem_bytes * 1.2) + (6 << 20), VMEM_LIMIT_CAP)
    return pltpu.CompilerParams(dimension_semantics=semantics, vmem_limit_bytes=limit)


def _once(block_shape, index_map):
    return pl.BlockSpec(block_shape, index_map, pipeline_mode=pl.Buffered(1))


def _rms(x, g):
    return x * lax.rsqrt(jnp.mean(x * x, axis=-1, keepdims=True) + EPS) * g


def _norm_rows_into(x_ref, g_ref, xn_ref):
    tm = x_ref.shape[0]
    rows = _pick_tile(tm, 64, 16)

    def body(r, carry):
        sl = pl.ds(pl.multiple_of(r * rows, rows), rows)
        xn_ref[sl, :] = _rms(x_ref[sl, :], g_ref[...]).astype(xn_ref.dtype)
        return carry

    lax.fori_loop(0, tm // rows, body, 0)


def _bdot(a, b):
    return jnp.dot(a, b, preferred_element_type=F32)


def _inproj_kernel(x_ref, g_ref, w_ref, o_ref, xn_ref):
    @pl.when(pl.program_id(1) == 0)
    def _():
        _norm_rows_into(x_ref, g_ref, xn_ref)

    o_ref[...] = _bdot(xn_ref[...], w_ref[...].astype(BF16))


def _inproj(x, g, w, e):
    m, d = x.shape
    n = w.shape[2]
    tm = _pick_tile(m, 1100, 16)
    tn = _pick_tile(n, 512, V7X_LANES)
    vmem = tm * d * 4 + tm * d * 2 + 2 * d * tn * 4 + 2 * tm * tn * 4 + d * tn * 2
    return pl.pallas_call(
        _inproj_kernel,
        grid=(m // tm, n // tn),
        in_specs=[
            _once((tm, d), lambda i, j: (i, 0)),
            _once((1, d), lambda i, j: (0, 0)),
            pl.BlockSpec((None, d, tn), lambda i, j: (e, 0, j)),
        ],
        out_specs=pl.BlockSpec((tm, tn), lambda i, j: (i, j)),
        out_shape=jax.ShapeDtypeStruct((m, n), F32),
        scratch_shapes=[pltpu.VMEM((tm, d), BF16)],
        compiler_params=_params(("parallel", "arbitrary"), vmem),
        name="inproj",
    )(x, g.reshape(1, d), w)


def _outproj_kernel(o_ref, lo_ref, wo_ref, wl_ref, x_ref, out_ref):
    acc = _bdot(o_ref[...], wo_ref[...].astype(BF16))
    acc = acc + _bdot(lo_ref[...], wl_ref[...].astype(BF16))
    out_ref[...] = x_ref[...] + acc


def _outproj(o, lo, w, e, x):
    m, d = x.shape
    half = o.shape[1]
    tm = _pick_tile(m, 1100, 16)
    tn = _pick_tile(d, 512, V7X_LANES)
    vmem = 2 * tm * half * 2 + 4 * half * tn * 4 + 4 * tm * tn * 4 + 2 * half * tn * 2
    return pl.pallas_call(
        _outproj_kernel,
        grid=(m // tm, d // tn),
        in_specs=[
            _once((tm, half), lambda i, j: (i, 0)),
            _once((tm, half), lambda i, j: (i, 0)),
            pl.BlockSpec((None, half, tn), lambda i, j: (e, 0, j)),
            pl.BlockSpec((None, half, tn), lambda i, j: (e, 1, j)),
            pl.BlockSpec((tm, tn), lambda i, j: (i, j)),
        ],
        out_specs=pl.BlockSpec((tm, tn), lambda i, j: (i, j)),
        out_shape=jax.ShapeDtypeStruct((m, d), F32),
        compiler_params=_params(("parallel", "arbitrary"), vmem),
        name="outproj",
    )(o, lo, w, w, x)


def _glu_kernel(y_ref, w1_ref, w2_ref, b1_ref, b2_ref, x_ref, out_ref):
    y = y_ref[...]
    z1 = _bdot(y, w1_ref[...].astype(BF16)) + b1_ref[...]
    z2 = _bdot(y, w2_ref[...].astype(BF16)) + b2_ref[...]
    out_ref[...] = x_ref[...] + z1 * jax.nn.sigmoid(z2)


def _glu(y, w, o, b, x):
    m, d = x.shape
    tm = _pick_tile(m, 1100, 16)
    tn = _pick_tile(d, 256, V7X_LANES)
    nj = d // tn
    vmem = tm * d * 2 + 4 * d * tn * 4 + 4 * tm * tn * 4 + 2 * d * tn * 2 + 2 * tm * tn * 4
    return pl.pallas_call(
        _glu_kernel,
        grid=(m // tm, nj),
        in_specs=[
            _once((tm, d), lambda i, j: (i, 0)),
            pl.BlockSpec((None, d, tn), lambda i, j: (o, 0, j)),
            pl.BlockSpec((None, d, tn), lambda i, j: (o, 0, nj + j)),
            pl.BlockSpec((1, tn), lambda i, j: (0, j)),
            pl.BlockSpec((1, tn), lambda i, j: (0, nj + j)),
            pl.BlockSpec((tm, tn), lambda i, j: (i, j)),
        ],
        out_specs=pl.BlockSpec((tm, tn), lambda i, j: (i, j)),
        out_shape=jax.ShapeDtypeStruct((m, d), F32),
        compiler_params=_params(("parallel", "arbitrary"), vmem),
        name="glu",
    )(y, w, w, b.reshape(1, 2 * d), b.reshape(1, 2 * d), x)


def _ffn_kernel(x_ref, g_ref, wg_ref, wu_ref, wd_ref, out_ref, xn_ref):
    @pl.when(pl.program_id(1) == 0)
    def _():
        _norm_rows_into(x_ref, g_ref, xn_ref)
        out_ref[...] = x_ref[...]

    xn = xn_ref[...]
    gate = _bdot(xn, wg_ref[...].astype(BF16))
    up = _bdot(xn, wu_ref[...].astype(BF16))
    act = (gate * jax.nn.sigmoid(gate) * up).astype(BF16)
    out_ref[...] += _bdot(act, wd_ref[...].astype(BF16))


def _ffn(x, g, w_gu, w_down, layer):
    m, d = x.shape
    hidden = w_down.shape[1]
    tm = _pick_tile(m, 1100, 16)
    th = _pick_tile(hidden, 256, V7X_LANES)
    nh = hidden // th
    vmem = (2 * tm * d * 4 + tm * d * 2 + 6 * d * th * 4 + 3 * d * th * 2
            + 3 * tm * th * 4)
    return pl.pallas_call(
        _ffn_kernel,
        grid=(m // tm, nh),
        in_specs=[
            _once((tm, d), lambda i, j: (i, 0)),
            _once((1, d), lambda i, j: (0, 0)),
            pl.BlockSpec((None, d, th), lambda i, j: (layer, 0, j)),
            pl.BlockSpec((None, d, th), lambda i, j: (layer, 0, nh + j)),
            pl.BlockSpec((None, th, d), lambda i, j: (layer, j, 0)),
        ],
        out_specs=_once((tm, d), lambda i, j: (i, 0)),
        out_shape=jax.ShapeDtypeStruct((m, d), F32),
        scratch_shapes=[pltpu.VMEM((tm, d), BF16)],
        compiler_params=_params(("parallel", "arbitrary"), vmem),
        name="ffn",
    )(x, g.reshape(1, d), w_gu, w_gu, w_down)


def _final_norm_kernel(x_ref, g_ref, yp_ref, ys_ref, *, n_prompt_tiles):
    y = _rms(x_ref[...], g_ref[...])
    i = pl.program_id(0)

    @pl.when(i < n_prompt_tiles)
    def _():
        yp_ref[...] = y

    @pl.when(i >= n_prompt_tiles)
    def _():
        ys_ref[...] = y


def _final_norm(x, g, m_prompt):
    m, d = x.shape
    m_sample = m - m_prompt
    tm = _pick_tile(math.gcd(m_prompt, m_sample), 512, V7X_SUBLANES)
    npt = m_prompt // tm
    return pl.pallas_call(
        functools.partial(_final_norm_kernel, n_prompt_tiles=npt),
        grid=(m // tm,),
        in_specs=[
            pl.BlockSpec((tm, d), lambda i: (i, 0)),
            pl.BlockSpec((1, d), lambda i: (0, 0)),
        ],
        out_specs=[
            pl.BlockSpec((tm, d), lambda i: (jnp.minimum(i, npt - 1), 0)),
            pl.BlockSpec((tm, d), lambda i: (jnp.maximum(i - npt, 0), 0)),
        ],
        out_shape=[jax.ShapeDtypeStruct((m_prompt, d), F32),
                   jax.ShapeDtypeStruct((m_sample, d), F32)],
        compiler_params=_params(("arbitrary",), 6 * tm * d * 4),
        name="final_norm",
    )(x, g.reshape(1, d))


def _rope_tables(pos, half):
    inv = 1.0 / jnp.power(ROPE_BASE, jnp.linspace(0.0, 1.0, half, dtype=F32))
    ang = pos.astype(F32)[:, None] * inv[None, :]
    return jnp.cos(ang), jnp.sin(ang)


def _decay_tables(c, dk):
    log_g = jnp.log1p(-jnp.exp2(-5.0 - jnp.arange(RET_HEADS, dtype=F32)))
    idx = jnp.arange(c, dtype=F32)
    diff = idx[:, None] - idx[None, :]
    mask = jnp.where(diff[None] >= 0,
                     jnp.exp(jnp.maximum(diff, 0.0)[None] * log_g[:, None, None]), 0.0)
    q_dec = jnp.exp((idx[None, :] + 1.0) * log_g[:, None])
    k_dec = jnp.exp((c - 1.0 - idx)[None, :] * log_g[:, None])
    chunk_dec = jnp.exp(c * log_g)
    q_dec = jnp.broadcast_to(q_dec[:, :, None], (RET_HEADS, c, dk))
    k_dec = jnp.broadcast_to(k_dec[:, :, None], (RET_HEADS, c, dk))
    chunk_dec = jnp.broadcast_to(chunk_dec[:, None, None], (RET_HEADS, 1, dk))
    return mask, q_dec, k_dec, chunk_dec


def _rotate(x, cos, sin):
    half = x.shape[-1] // 2
    x1, x2 = x[:, :half], x[:, half:]
    return jnp.concatenate([x1 * cos - x2 * sin, x2 * cos + x1 * sin], axis=-1)


def _ret_prompt_kernel(q_ref, k_ref, v_ref, g_ref, cos_ref, sin_ref, mask_ref, qd_ref,
                       kd_ref, cd_ref, o_ref, s_out_ref, s_ref, *, n_chunks, scale):
    c = pl.program_id(2)

    @pl.when(c == 0)
    def _():
        s_ref[...] = jnp.zeros_like(s_ref)

    cos, sin = cos_ref[...], sin_ref[...]
    q = _rotate(q_ref[...], cos, sin)
    k = _rotate(k_ref[...], cos, sin) * scale
    qb, kb, vb = q.astype(BF16), k.astype(BF16), v_ref[...].astype(BF16)
    s = s_ref[...]
    scores = lax.dot_general(qb, kb, (((1,), (1,)), ((), ())),
                             preferred_element_type=F32) * mask_ref[0]
    o = _bdot(scores.astype(BF16), vb) + _bdot(qb, s.astype(BF16)) * qd_ref[0]
    kdb = (k * kd_ref[0]).astype(BF16)
    s_new = s * cd_ref[0] + lax.dot_general(kdb, vb, (((0,), (0,)), ((), ())),
                                            preferred_element_type=F32)
    s_ref[...] = s_new
    o = o * lax.rsqrt(jnp.mean(o * o, axis=-1, keepdims=True) + EPS)
    gr = g_ref[...]
    o_ref[...] = (o * (gr * jax.nn.sigmoid(gr))).astype(o_ref.dtype)

    @pl.when(c == n_chunks - 1)
    def _():
        s_out_ref[0, 0] = s_new


def _ret_prompt(proj, b, t, half):
    dk = half // RET_HEADS
    c = RET_CHUNK
    assert t % c == 0
    nc = t // c
    cos, sin = _rope_tables(jnp.arange(t), dk // 2)
    mask, q_dec, k_dec, chunk_dec = _decay_tables(c, dk)
    h = RET_HEADS

    def col(off):
        return pl.BlockSpec((c, dk), lambda bi, hi, ci: (bi * nc + ci, off + hi))

    def per_head(shape):
        return pl.BlockSpec((1,) + shape, lambda bi, hi, ci: (hi, 0, 0))

    vmem = 8 * c * dk * 4 + 2 * c * dk * 2 + 6 * dk * dk * 4 + 8 * c * dk * 4
    return pl.pallas_call(
        functools.partial(_ret_prompt_kernel, n_chunks=nc, scale=dk ** -0.5),
        grid=(b, h, nc),
        in_specs=[
            col(0), col(h), col(2 * h), col(3 * h),
            pl.BlockSpec((c, dk // 2), lambda bi, hi, ci: (ci, 0)),
            pl.BlockSpec((c, dk // 2), lambda bi, hi, ci: (ci, 0)),
            per_head((c, c)), per_head((c, dk)), per_head((c, dk)), per_head((1, dk)),
        ],
        out_specs=[
            pl.BlockSpec((c, dk), lambda bi, hi, ci: (bi * nc + ci, hi)),
            pl.BlockSpec((1, 1, dk, dk), lambda bi, hi, ci: (bi, hi, 0, 0)),
        ],
        out_shape=[jax.ShapeDtypeStruct((b * t, half), BF16),
                   jax.ShapeDtypeStruct((b, h, dk, dk), F32)],
        scratch_shapes=[pltpu.VMEM((dk, dk), F32)],
        compiler_params=_params(("parallel", "parallel", "arbitrary"), vmem),
        name="ret_prompt",
    )(proj, proj, proj, proj, cos, sin, mask, q_dec, k_dec, chunk_dec)


def _ret_sample_kernel(q_ref, k_ref, v_ref, g_ref, cos_ref, sin_ref, mask_ref, qd_ref,
                       kd_ref, cd_ref, s_in_ref, o_ref, s_out_ref, kdt_ref, acc_ref,
                       *, dt, scale):
    rows = q_ref.shape[0]
    nb = rows // dt
    cos, sin = cos_ref[...], sin_ref[...]
    q = _rotate(q_ref[...], cos, sin)
    k = _rotate(k_ref[...], cos, sin) * scale
    qb, kb, vb = q.astype(BF16), k.astype(BF16), v_ref[...].astype(BF16)
    scores = lax.dot_general(qb, kb, (((1,), (1,)), ((), ())),
                             preferred_element_type=F32) * mask_ref[0]
    acc_ref[...] = _bdot(scores.astype(BF16), vb)
    kdt_ref[...] = (k * kd_ref[0]).T
    qd = qd_ref[0]
    cd = cd_ref[0]
    row_batch = lax.broadcasted_iota(jnp.int32, (rows, 1), 0) // dt
    col_batch = lax.broadcasted_iota(jnp.int32, (1, rows), 1) // dt

    def body(j, carry):
        s = s_in_ref[0, j, 0]
        inter = _bdot(qb, s.astype(BF16)) * qd
        acc_ref[...] += jnp.where(row_batch == j, inter, 0.0)
        kdt_j = jnp.where(col_batch == j, kdt_ref[...], 0.0).astype(BF16)
        s_out_ref[j, 0] = s * cd + _bdot(kdt_j, vb)
        return carry

    lax.fori_loop(0, nb, body, 0)
    o = acc_ref[...]
    o = o * lax.rsqrt(jnp.mean(o * o, axis=-1, keepdims=True) + EPS)
    gr = g_ref[...]
    o_ref[...] = (o * (gr * jax.nn.sigmoid(gr))).astype(o_ref.dtype)


def _ret_sample(qkvg, state_ret, e, db, dt, half, past_len):
    dk = half // RET_HEADS
    h = RET_HEADS
    assert RET_CHUNK % dt == 0 and RET_SAMPLE_ROWS % dt == 0
    rows = RET_SAMPLE_ROWS
    bb = rows // dt
    assert db % bb == 0
    cos, sin = _rope_tables(past_len + jnp.arange(dt), dk // 2)
    cos, sin = jnp.tile(cos, (bb, 1)), jnp.tile(sin, (bb, 1))
    mask, q_dec, k_dec, chunk_dec = _decay_tables(dt, dk)
    mask = jnp.einsum("ab,hnm->hanbm", jnp.eye(bb, dtype=F32), mask).reshape(h, rows, rows)
    q_dec, k_dec = jnp.tile(q_dec, (1, bb, 1)), jnp.tile(k_dec, (1, bb, 1))

    def col(off):
        return pl.BlockSpec((rows, dk), lambda bi, hi: (bi, off + hi))

    def per_head(shape):
        return pl.BlockSpec((1,) + shape, lambda bi, hi: (hi, 0, 0))

    vmem = 4 * bb * dk * dk * 4 + 12 * rows * dk * 4 + 4 * dk * dk * 4
    return pl.pallas_call(
        functools.partial(_ret_sample_kernel, dt=dt, scale=dk ** -0.5),
        grid=(db // bb, h),
        in_specs=[
            col(0), col(h), col(2 * h), col(3 * h),
            pl.BlockSpec((rows, dk // 2), lambda bi, hi: (0, 0)),
            pl.BlockSpec((rows, dk // 2), lambda bi, hi: (0, 0)),
            per_head((rows, rows)), per_head((rows, dk)), per_head((rows, dk)),
            per_head((1, dk)),
            pl.BlockSpec((1, bb, 1, dk, dk), lambda bi, hi: (e, bi, hi, 0, 0)),
        ],
        out_specs=[
            pl.BlockSpec((rows, dk), lambda bi, hi: (bi, hi)),
            pl.BlockSpec((bb, 1, dk, dk), lambda bi, hi: (bi, hi, 0, 0)),
        ],
        out_shape=[jax.ShapeDtypeStruct((db * dt, half), BF16),
                   jax.ShapeDtypeStruct((db, h, dk, dk), F32)],
        scratch_shapes=[pltpu.VMEM((dk, rows), F32), pltpu.VMEM((rows, dk), F32)],
        compiler_params=_params(("parallel", "parallel"), vmem),
        name="ret_sample",
    )(qkvg, qkvg, qkvg, qkvg, cos, sin, mask, q_dec, k_dec, chunk_dec, state_ret)


def _lru_gates(xc, wa_ref, ba_ref, wx_ref, bx_ref, sp_ref):
    xcb = xc.astype(BF16)
    blk = wa_ref.shape[1]
    ra, ri = [], []
    for n in range(wa_ref.shape[0]):
        xn = xcb[:, n * blk:(n + 1) * blk]
        ra.append(_bdot(xn, wa_ref[n].astype(BF16)))
        ri.append(_bdot(xn, wx_ref[n].astype(BF16)))
    r = jax.nn.sigmoid(jnp.concatenate(ra, axis=-1) + ba_ref[...])
    i = jax.nn.sigmoid(jnp.concatenate(ri, axis=-1) + bx_ref[...])
    log_a = -LRU_C * r * sp_ref[...]
    a = jnp.exp(log_a)
    mult = jnp.sqrt(jnp.maximum(-jnp.tanh(log_a) * (a * a + 1.0), 0.0))
    return a, mult * i * xc


def _lru_prompt_kernel(xl_ref, yl_ref, cw_ref, cb_ref, wa_ref, ba_ref, wx_ref, bx_ref,
                       sp_ref, lo_ref, h_out_ref, conv_out_ref, xs_ref, a_ref, b_ref,
                       hc_ref, *, n_chunks):
    c = pl.program_id(1)
    tc, w = xl_ref.shape
    sub = V7X_SUBLANES

    @pl.when(c == 0)
    def _():
        xs_ref[0:sub, :] = jnp.zeros((sub, w), F32)
        hc_ref[...] = jnp.zeros_like(hc_ref)

    x = xl_ref[...]
    xs_ref[sub:sub + tc, :] = x
    xc = cb_ref[...] + x * cw_ref[CONV_W - 1:CONV_W, :]
    for i in range(CONV_W - 1):
        back = CONV_W - 1 - i
        xc = xc + xs_ref[sub - back:sub - back + tc, :] * cw_ref[i:i + 1, :]
    xs_ref[0:sub, :] = xs_ref[tc:tc + sub, :]

    a, bt = _lru_gates(xc, wa_ref, ba_ref, wx_ref, bx_ref, sp_ref)

    a3 = a.reshape(tc // sub, sub, w)
    b3 = bt.reshape(tc // sub, sub, w)
    step = lax.broadcasted_iota(jnp.int32, (1, sub, 1), 1)
    for s in (1, 2, 4):
        keep = step >= s
        a_prev = jnp.where(keep, pltpu.roll(a3, s, axis=1), 1.0)
        b_prev = jnp.where(keep, pltpu.roll(b3, s, axis=1), 0.0)
        b3 = a3 * b_prev + b3
        a3 = a3 * a_prev
    a_ref[...] = a3.reshape(tc, w)
    b_ref[...] = b3.reshape(tc, w)

    def body(g, h):
        sl = pl.ds(pl.multiple_of(g * sub, sub), sub)
        hg = b_ref[sl, :] + a_ref[sl, :] * h
        b_ref[sl, :] = hg
        return jnp.broadcast_to(hg[sub - 1:sub, :], (sub, w))

    h_last = lax.fori_loop(0, tc // sub, body, hc_ref[...])
    hc_ref[...] = h_last
    lo_ref[...] = (jax.nn.gelu(yl_ref[...]) * b_ref[...]).astype(lo_ref.dtype)

    @pl.when(c == n_chunks - 1)
    def _():
        h_out_ref[0] = h_last[0:1, :]
        conv_out_ref[0] = xs_ref[sub - (CONV_W - 1):sub, :]


def _lru_prompt(proj, b, t, w, cw, cb, wa, ba, wx, bx, sp):
    tc = _pick_tile(t, 256, 16)
    nc = t // tc
    xl_col = (proj.shape[1] - 2 * w) // w
    vec = lambda: pl.BlockSpec((1, w), lambda bi, ci: (0, 0))
    blocks = lambda: pl.BlockSpec(wa.shape, lambda bi, ci: (0, 0, 0))
    vmem = 4 * tc * w * 4 + 2 * tc * w * 2 + 3 * tc * w * 4 + 8 * tc * w * 4
    return pl.pallas_call(
        functools.partial(_lru_prompt_kernel, n_chunks=nc),
        grid=(b, nc),
        in_specs=[
            pl.BlockSpec((tc, w), lambda bi, ci: (bi * nc + ci, xl_col)),
            pl.BlockSpec((tc, w), lambda bi, ci: (bi * nc + ci, xl_col + 1)),
            pl.BlockSpec((CONV_W, w), lambda bi, ci: (0, 0)),
            vec(), blocks(), vec(), blocks(), vec(), vec(),
        ],
        out_specs=[
            pl.BlockSpec((tc, w), lambda bi, ci: (bi * nc + ci, 0)),
            pl.BlockSpec((1, 1, w), lambda bi, ci: (bi, 0, 0)),
            pl.BlockSpec((1, CONV_W - 1, w), lambda bi, ci: (bi, 0, 0)),
        ],
        out_shape=[jax.ShapeDtypeStruct((b * t, w), BF16),
                   jax.ShapeDtypeStruct((b, 1, w), F32),
                   jax.ShapeDtypeStruct((b, CONV_W - 1, w), F32)],
        scratch_shapes=[pltpu.VMEM((tc + V7X_SUBLANES, w), F32),
                        pltpu.VMEM((tc, w), F32), pltpu.VMEM((tc, w), F32),
                        pltpu.VMEM((V7X_SUBLANES, w), F32)],
        compiler_params=_params(("parallel", "arbitrary"), vmem),
        name="lru_prompt",
    )(proj, proj, cw, cb.reshape(1, w), wa, ba.reshape(1, w), wx, bx.reshape(1, w), sp)


def _lru_sample_kernel(xl_ref, yl_ref, conv0_ref, h0_ref, cw_ref, cb_ref, wa_ref, ba_ref,
                       wx_ref, bx_ref, sp_ref, lo_ref, h_out_ref, conv_out_ref, *, dt):
    db = h0_ref.shape[1]
    w = h0_ref.shape[2]
    taps = CONV_W - 1
    xp = [conv0_ref[0, :, i * w:(i + 1) * w] for i in range(taps)]
    xp += [xl_ref[t * db:(t + 1) * db, :] for t in range(dt)]
    xcs = []
    for t in range(dt):
        xc = cb_ref[...] + xp[t] * cw_ref[0:1, :]
        for i in range(1, CONV_W):
            xc = xc + xp[t + i] * cw_ref[i:i + 1, :]
        xcs.append(xc)
    a, bt = _lru_gates(jnp.concatenate(xcs, axis=0), wa_ref, ba_ref, wx_ref, bx_ref, sp_ref)
    h = h0_ref[0]
    for t in range(dt):
        rows = slice(t * db, (t + 1) * db)
        h = a[rows] * h + bt[rows]
        lo_ref[rows, :] = (jax.nn.gelu(yl_ref[rows, :]) * h).astype(lo_ref.dtype)
    h_out_ref[...] = h
    for i in range(taps):
        conv_out_ref[:, i * w:(i + 1) * w] = xp[dt + i]


def _lru_sample(proj, m_prompt, state_conv, state_lru, e, db, dt, w, cw, cb, wa, ba, wx, bx, sp):
    rows = db * dt
    assert m_prompt % rows == 0
    rb = m_prompt // rows
    xl_col = (proj.shape[1] - 2 * w) // w
    taps = CONV_W - 1
    conv0 = state_conv.reshape(state_conv.shape[0], db, taps * w)
    vec = lambda: pl.BlockSpec((1, w), lambda i: (0, 0))
    blocks = lambda: pl.BlockSpec(wa.shape, lambda i: (0, 0, 0))
    vmem = 16 * rows * w * 4
    lo, h_new, conv_new = pl.pallas_call(
        functools.partial(_lru_sample_kernel, dt=dt),
        grid=(1,),
        in_specs=[
            pl.BlockSpec((rows, w), lambda i: (rb, xl_col)),
            pl.BlockSpec((rows, w), lambda i: (rb, xl_col + 1)),
            pl.BlockSpec((1, db, taps * w), lambda i: (e, 0, 0)),
            pl.BlockSpec((1, db, w), lambda i: (e, 0, 0)),
            pl.BlockSpec((CONV_W, w), lambda i: (0, 0)),
            vec(), blocks(), vec(), blocks(), vec(), vec(),
        ],
        out_specs=[
            pl.BlockSpec((rows, w), lambda i: (0, 0)),
            pl.BlockSpec((db, w), lambda i: (0, 0)),
            pl.BlockSpec((db, taps * w), lambda i: (0, 0)),
        ],
        out_shape=[jax.ShapeDtypeStruct((rows, w), BF16),
                   jax.ShapeDtypeStruct((db, w), F32),
                   jax.ShapeDtypeStruct((db, taps * w), F32)],
        compiler_params=_params(("arbitrary",), vmem),
        name="lru_sample",
    )(proj, proj, conv0, state_lru, cw, cb.reshape(1, w), wa, ba.reshape(1, w), wx,
      bx.reshape(1, w), sp)
    return lo, h_new, conv_new.reshape(db, taps, w)


def _s5_tables(a_re, a_im, b_re, b_im, c_re, c_im, d, log_dt):
    g = a_re.shape[0]
    nb = g // SSM_BLOCK_GROUPS
    dt = jnp.exp(log_dt)[:, None]
    mag = jnp.exp(a_re * dt)
    abr = mag * jnp.cos(a_im * dt)
    abi = mag * jnp.sin(a_im * dt)
    den = a_re * a_re + a_im * a_im
    nr, ni = abr - 1.0, abi
    fr = (nr * a_re + ni * a_im) / den
    fi = (ni * a_re - nr * a_im) / den
    bbr = fr[..., None] * b_re - fi[..., None] * b_im
    bbi = fr[..., None] * b_im + fi[..., None] * b_re
    eye = jnp.eye(SSM_BLOCK_GROUPS, dtype=F32)

    def pack_in(bb):
        bb = bb.reshape(nb, SSM_BLOCK_GROUPS, SSM_P, SSM_GROUP)
        return jnp.einsum("igpk,gh->igkhp", bb, eye).reshape(
            nb, V7X_MXU_DIM, SSM_BLOCK_STATES)

    def pack_out(cc):
        cc = cc.reshape(nb, SSM_BLOCK_GROUPS, SSM_GROUP, SSM_P)
        return jnp.einsum("igkp,gh->igphk", cc, eye).reshape(
            nb, SSM_BLOCK_STATES, V7X_MXU_DIM)

    w_in = jnp.concatenate([pack_in(bbr), pack_in(bbi)], axis=-1).astype(BF16)
    w_out = jnp.concatenate([pack_out(c_re), pack_out(-c_im)], axis=1).astype(BF16)
    return (abr.reshape(nb, SSM_BLOCK_STATES), abi.reshape(nb, SSM_BLOCK_STATES),
            w_in, w_out, d.reshape(nb, 1, V7X_MXU_DIM))


def _s5_prompt_kernel(x_ref, g_ref, ar_ref, ai_ref, win_ref, wout_ref, d_ref, y_ref,
                      hr_out_ref, hi_out_ref, u_ref, sr_ref, si_ref, hr_ref, hi_ref, *,
                      n_chunks):
    c = pl.program_id(1)
    tc = x_ref.shape[0]
    nb = win_ref.shape[0]
    ns = SSM_BLOCK_STATES
    bw = V7X_MXU_DIM

    @pl.when(c == 0)
    def _():
        hr_ref[...] = jnp.zeros_like(hr_ref)
        hi_ref[...] = jnp.zeros_like(hi_ref)

    u_ref[...] = _rms(x_ref[...], g_ref[...])
    ln = V7X_LANES
    nl = ns // ln
    for i in range(nb):
        bu = _bdot(u_ref[:, i * bw:(i + 1) * bw].astype(BF16), win_ref[i])
        for l in range(nl):
            sr_ref[l, i * tc:(i + 1) * tc, :] = bu[:, l * ln:(l + 1) * ln]
            si_ref[l, i * tc:(i + 1) * tc, :] = bu[:, ns + l * ln:ns + (l + 1) * ln]

    group = 4
    for l0 in range(0, nl, group):
        ls = range(l0, l0 + group)
        ar = [ar_ref[:, l * ln:(l + 1) * ln] for l in ls]
        ai = [ai_ref[:, l * ln:(l + 1) * ln] for l in ls]

        def body(t, carry):
            rows = pl.ds(t, nb, stride=tc)
            out = []
            for n, l in enumerate(ls):
                hr, hi = carry[n]
                hr_n = ar[n] * hr - ai[n] * hi + sr_ref[l, rows, :]
                hi_n = ar[n] * hi + ai[n] * hr + si_ref[l, rows, :]
                sr_ref[l, rows, :] = hr_n
                si_ref[l, rows, :] = hi_n
                out.append((hr_n, hi_n))
            return tuple(out)

        init = tuple((hr_ref[:, l * ln:(l + 1) * ln], hi_ref[:, l * ln:(l + 1) * ln])
                     for l in ls)
        last = lax.fori_loop(0, tc, body, init, unroll=4)
        for n, l in enumerate(ls):
            hr_ref[:, l * ln:(l + 1) * ln] = last[n][0]
            hi_ref[:, l * ln:(l + 1) * ln] = last[n][1]

    for i in range(nb):
        rows = slice(i * tc, (i + 1) * tc)
        s_cat = jnp.concatenate([sr_ref[l, rows, :].astype(BF16) for l in range(nl)]
                                + [si_ref[l, rows, :].astype(BF16) for l in range(nl)],
                                axis=-1)
        cols = slice(i * bw, (i + 1) * bw)
        y = _bdot(s_cat, wout_ref[i]) + d_ref[i] * u_ref[:, cols]
        y_ref[:, cols] = jax.nn.gelu(y).astype(y_ref.dtype)

    @pl.when(c == n_chunks - 1)
    def _():
        hr_out_ref[0] = hr_ref[...]
        hi_out_ref[0] = hi_ref[...]


def _s5_prompt(x, g, tables, b, t):
    d = x.shape[1]
    abr, abi, w_in, w_out, dd = tables
    nb = w_in.shape[0]
    ns = SSM_BLOCK_STATES
    tc = _pick_tile(t, 128, 16)
    nc = t // tc
    const = lambda a: _once(a.shape, lambda bi, ci: (0,) * a.ndim)
    vmem = (5 * tc * d * 4 + 2 * tc * d * 2 + w_in.size * 2 + w_out.size * 2
            + 2 * nb * tc * ns * 4 + 6 * tc * 2 * ns * 4)
    return pl.pallas_call(
        functools.partial(_s5_prompt_kernel, n_chunks=nc),
        grid=(b, nc),
        in_specs=[
            pl.BlockSpec((tc, d), lambda bi, ci: (bi * nc + ci, 0)),
            pl.BlockSpec((1, d), lambda bi, ci: (0, 0)),
            const(abr), const(abi), const(w_in), const(w_out), const(dd),
        ],
        out_specs=[
            pl.BlockSpec((tc, d), lambda bi, ci: (bi * nc + ci, 0)),
            pl.BlockSpec((1, nb, ns), lambda bi, ci: (bi, 0, 0)),
            pl.BlockSpec((1, nb, ns), lambda bi, ci: (bi, 0, 0)),
        ],
        out_shape=[jax.ShapeDtypeStruct((b * t, d), BF16),
                   jax.ShapeDtypeStruct((b, nb, ns), F32),
                   jax.ShapeDtypeStruct((b, nb, ns), F32)],
        scratch_shapes=[pltpu.VMEM((tc, d), F32),
                        pltpu.VMEM((ns // V7X_LANES, nb * tc, V7X_LANES), F32),
                        pltpu.VMEM((ns // V7X_LANES, nb * tc, V7X_LANES), F32),
                        pltpu.VMEM((nb, ns), F32), pltpu.VMEM((nb, ns), F32)],
        compiler_params=_params(("parallel", "arbitrary"), vmem),
        name="s5_prompt",
    )(x, g.reshape(1, d), abr, abi, w_in, w_out, dd)


def _s5_sample_kernel(x_ref, g_ref, ar_ref, ai_ref, win_ref, wout_ref, d_ref, h0r_ref,
                      h0i_ref, y_ref, hr_out_ref, hi_out_ref, uf_ref, ub_ref, *, dt):
    i = pl.program_id(0)
    nb = uf_ref.shape[0]
    bw = V7X_MXU_DIM
    ns = SSM_BLOCK_STATES
    db = h0r_ref.shape[1]

    @pl.when(i == 0)
    def _():
        u = _rms(x_ref[...], g_ref[...])
        for n in range(nb):
            uf_ref[n] = u[:, n * bw:(n + 1) * bw]
            ub_ref[n] = u[:, n * bw:(n + 1) * bw].astype(BF16)

    bu = _bdot(ub_ref[i], win_ref[0])
    ar, ai = ar_ref[pl.ds(i, 1), :], ai_ref[pl.ds(i, 1), :]
    hr, hi = h0r_ref[0], h0i_ref[0]
    states = []
    for t in range(dt):
        rows = slice(t * db, (t + 1) * db)
        hr, hi = (ar * hr - ai * hi + bu[rows, :ns], ar * hi + ai * hr + bu[rows, ns:])
        states.append(jnp.concatenate([hr.astype(BF16), hi.astype(BF16)], axis=-1))
    y = _bdot(jnp.concatenate(states, axis=0), wout_ref[0]) + d_ref[0] * uf_ref[i]
    y_ref[...] = jax.nn.gelu(y).astype(y_ref.dtype)
    hr_out_ref[...] = hr
    hi_out_ref[...] = hi


def _s5_sample(x, m_prompt, g, tables, state_re, state_im, o, db, dt):
    d = x.shape[1]
    abr, abi, w_in, w_out, dd = tables
    nb = w_in.shape[0]
    ns = SSM_BLOCK_STATES
    bw = V7X_MXU_DIM
    rows = db * dt
    assert m_prompt % rows == 0
    rb = m_prompt // rows
    no = state_re.shape[0]
    h0r = state_re.reshape(no, db, nb * ns)
    h0i = state_im.reshape(no, db, nb * ns)
    blk = lambda a: pl.BlockSpec((1,) + a.shape[1:], lambda i: (i, 0, 0))
    vmem = (2 * rows * d * 4 + rows * d * 6 + 4 * bw * 2 * ns * 2 + 8 * db * ns * 4
            + 8 * rows * 2 * ns * 4)
    return pl.pallas_call(
        functools.partial(_s5_sample_kernel, dt=dt),
        grid=(nb,),
        in_specs=[
            _once((rows, d), lambda i: (rb, 0)),
            pl.BlockSpec((1, d), lambda i: (0, 0)),
            pl.BlockSpec(abr.shape, lambda i: (0, 0)),
            pl.BlockSpec(abi.shape, lambda i: (0, 0)),
            blk(w_in), blk(w_out), blk(dd),
            pl.BlockSpec((1, db, ns), lambda i: (o, 0, i)),
            pl.BlockSpec((1, db, ns), lambda i: (o, 0, i)),
        ],
        out_specs=[
            pl.BlockSpec((rows, bw), lambda i: (0, i)),
            pl.BlockSpec((db, ns), lambda i: (0, i)),
            pl.BlockSpec((db, ns), lambda i: (0, i)),
        ],
        out_shape=[jax.ShapeDtypeStruct((rows, d), BF16),
                   jax.ShapeDtypeStruct((db, nb * ns), F32),
                   jax.ShapeDtypeStruct((db, nb * ns), F32)],
        scratch_shapes=[pltpu.VMEM((nb, rows, bw), F32), pltpu.VMEM((nb, rows, bw), BF16)],
        compiler_params=_params(("arbitrary",), vmem),
        name="s5_sample",
    )(x, g.reshape(1, d), abr, abi, w_in, w_out, dd, h0r, h0i)


def kernel(x_prompt, x_sample, state_ret, state_lru, state_conv, state_ssm_re, state_ssm_im, norm_mix_even, w_in_even, lru_conv_w, lru_conv_b, lru_wa, lru_ba, lru_wx, lru_bx, lru_lambda, w_out_even, norm_mix_odd, ssm_a_re, ssm_a_im, ssm_b_re, ssm_b_im, ssm_c_re, ssm_c_im, ssm_d, ssm_log_dt, w_glu, b_glu, norm_ffn, w_ffn_gu, w_ffn_down, norm_final):
    b, t, d = x_prompt.shape
    db, dt, _ = x_sample.shape
    depth = norm_ffn.shape[0]
    half = d // 2
    m_prompt = b * t
    past_len = PAST_LEN
    groups, ssm_p = ssm_a_re.shape[1:]
    assert ssm_p == SSM_P and groups * SSM_GROUP == d and groups % SSM_BLOCK_GROUPS == 0

    x = jnp.concatenate([x_prompt.reshape(m_prompt, d),
                         x_sample.transpose(1, 0, 2).reshape(dt * db, d)], axis=0)

    rets_p, rets_s, lrus_p, lrus_s, convs_p, convs_s = [], [], [], [], [], []
    sres_p, sres_s, sims_p, sims_s = [], [], [], []
    for layer in range(depth):
        if layer % 2 == 0:
            e = layer // 2
            proj = _inproj(x, norm_mix_even[e], w_in_even, e)
            sp = jax.nn.softplus(-lru_lambda[e]).reshape(1, half)
            lru_w = (lru_conv_w[e], lru_conv_b[e], lru_wa[e], lru_ba[e], lru_wx[e], lru_bx[e], sp)

            o_p, ret_p = _ret_prompt(proj, b, t, half)
            qkvg_s = proj[m_prompt:, :4 * half].reshape(dt, db, 4 * half)
            qkvg_s = qkvg_s.transpose(1, 0, 2).reshape(db * dt, 4 * half)
            o_s, ret_s = _ret_sample(qkvg_s, state_ret, e, db, dt, half, past_len)
            o_s = o_s.reshape(db, dt, half).transpose(1, 0, 2).reshape(dt * db, half)

            lo_p, lru_p, conv_p = _lru_prompt(proj, b, t, half, *lru_w)
            lo_s, lru_s, conv_s = _lru_sample(proj, m_prompt, state_conv, state_lru, e, db, dt,
                                              half, *lru_w)

            x = _outproj(jnp.concatenate([o_p, o_s], axis=0),
                         jnp.concatenate([lo_p, lo_s], axis=0), w_out_even, e, x)
            rets_p.append(ret_p)
            rets_s.append(ret_s)
            lrus_p.append(lru_p.reshape(b, half))
            lrus_s.append(lru_s)
            convs_p.append(conv_p)
            convs_s.append(conv_s)
        else:
            o = layer // 2
            tables = _s5_tables(ssm_a_re[o], ssm_a_im[o], ssm_b_re[o], ssm_b_im[o],
                                ssm_c_re[o], ssm_c_im[o], ssm_d[o], ssm_log_dt[o])
            y_p, sre_p, sim_p = _s5_prompt(x, norm_mix_odd[o], tables, b, t)
            y_s, sre_s, sim_s = _s5_sample(x, m_prompt, norm_mix_odd[o], tables, state_ssm_re,
                                           state_ssm_im, o, db, dt)
            x = _glu(jnp.concatenate([y_p, y_s], axis=0), w_glu, o, b_glu[o], x)
            sres_p.append(sre_p.reshape(b, groups, ssm_p))
            sims_p.append(sim_p.reshape(b, groups, ssm_p))
            sres_s.append(sre_s.reshape(db, groups, ssm_p))
            sims_s.append(sim_s.reshape(db, groups, ssm_p))
        x = _ffn(x, norm_ffn[layer], w_ffn_gu, w_ffn_down, layer)

    y_p, y_s = _final_norm(x, norm_final, m_prompt)
    y_prompt = y_p.reshape(b, t, d)
    y_sample = y_s.reshape(dt, db, d).transpose(1, 0, 2)
    return (y_prompt, y_sample, jnp.stack(rets_p), jnp.stack(rets_s), jnp.stack(lrus_p),
            jnp.stack(lrus_s), jnp.stack(convs_p), jnp.stack(convs_s), jnp.stack(sres_p),
            jnp.stack(sres_s), jnp.stack(sims_p), jnp.stack(sims_s))
```

```python
import functools

import jax
import jax.numpy as jnp
from jax import lax
from jax.experimental import pallas as pl
from jax.experimental.pallas import tpu as pltpu

F32 = jnp.float32
BF16 = jnp.bfloat16

EPS = 1e-6
PAST_LEN = 16384
ROPE_BASE = 10000.0
RET_HEADS = 4
RET_CHUNK = 128
LRU_HEADS = 8
LRU_C = 8.0
CONV_W = 4
SSM_GROUP = 16
SSM_P = 64
SSM_CHUNK = 128

V7X_SUBLANES = 8
V7X_LANES = 128
V7X_MXU_DIM = 256
V7X_VMEM_BYTES = 64 * 1024 * 1024
VMEM_LIMIT_CAP = V7X_VMEM_BYTES - 6 * 1024 * 1024

SSM_BLOCK_GROUPS = V7X_MXU_DIM // SSM_GROUP
SSM_BLOCK_STATES = SSM_BLOCK_GROUPS * SSM_P

RET_SAMPLE_ROWS = 128

S5_ROW_PITCH = 12


def _pick_tile(n, target, mult):
    best = None
    for t in range(mult, min(n, target) + 1, mult):
        if n % t == 0:
            best = t
    assert best is not None, (n, target, mult)
    return best


def _params(semantics, vmem_bytes):
    limit = min(int(vmem_bytes * 1.2) + (6 << 20), VMEM_LIMIT_CAP)
    return pltpu.CompilerParams(dimension_semantics=semantics, vmem_limit_bytes=limit)


def _once(block_shape, index_map):
    return pl.BlockSpec(block_shape, index_map, pipeline_mode=pl.Buffered(1))


def _rms(x, g):
    return x * lax.rsqrt(jnp.mean(x * x, axis=-1, keepdims=True) + EPS) * g


def _norm_rows_into(x_ref, g_ref, xn_ref):
    tm = x_ref.shape[0]
    rows = _pick_tile(tm, 64, 16)

    def body(r, carry):
        sl = pl.ds(pl.multiple_of(r * rows, rows), rows)
        xn_ref[sl, :] = _rms(x_ref[sl, :], g_ref[...]).astype(xn_ref.dtype)
        return carry

    lax.fori_loop(0, tm // rows, body, 0)


def _bdot(a, b):
    return jnp.dot(a, b, preferred_element_type=F32)


def _inproj_kernel(x_ref, g_ref, w_ref, o_ref, xn_ref):
    @pl.when(pl.program_id(1) == 0)
    def _():
        _norm_rows_into(x_ref, g_ref, xn_ref)

    o_ref[...] = _bdot(xn_ref[...], w_ref[...].astype(BF16))


def _inproj(x, g, w, e):
    m, d = x.shape
    n = w.shape[2]
    tm = _pick_tile(m, 1100, 16)
    tn = _pick_tile(n, 512, V7X_LANES)
    vmem = tm * d * 4 + tm * d * 2 + 2 * d * tn * 4 + 2 * tm * tn * 4 + d * tn * 2
    return pl.pallas_call(
        _inproj_kernel,
        grid=(m // tm, n // tn),
        in_specs=[
            _once((tm, d), lambda i, j: (i, 0)),
            _once((1, d), lambda i, j: (0, 0)),
            pl.BlockSpec((None, d, tn), lambda i, j: (e, 0, j)),
        ],
        out_specs=pl.BlockSpec((tm, tn), lambda i, j: (i, j)),
        out_shape=jax.ShapeDtypeStruct((m, n), F32),
        scratch_shapes=[pltpu.VMEM((tm, d), BF16)],
        compiler_params=_params(("parallel", "arbitrary"), vmem),
        name="inproj",
    )(x, g.reshape(1, d), w)


def _outproj_kernel(o_ref, lo_ref, wo_ref, wl_ref, x_ref, out_ref):
    acc = _bdot(o_ref[...], wo_ref[...].astype(BF16))
    acc = acc + _bdot(lo_ref[...], wl_ref[...].astype(BF16))
    out_ref[...] = x_ref[...] + acc


def _outproj(o, lo, w, e, x):
    m, d = x.shape
    half = o.shape[1]
    tm = _pick_tile(m, 1100, 16)
    tn = _pick_tile(d, 512, V7X_LANES)
    vmem = 2 * tm * half * 2 + 4 * half * tn * 4 + 4 * tm * tn * 4 + 2 * half * tn * 2
    return pl.pallas_call(
        _outproj_kernel,
        grid=(m // tm, d // tn),
        in_specs=[
            _once((tm, half), lambda i, j: (i, 0)),
            _once((tm, half), lambda i, j: (i, 0)),
            pl.BlockSpec((None, half, tn), lambda i, j: (e, 0, j)),
            pl.BlockSpec((None, half, tn), lambda i, j: (e, 1, j)),
            pl.BlockSpec((tm, tn), lambda i, j: (i, j)),
        ],
        out_specs=pl.BlockSpec((tm, tn), lambda i, j: (i, j)),
        out_shape=jax.ShapeDtypeStruct((m, d), F32),
        compiler_params=_params(("parallel", "arbitrary"), vmem),
        name="outproj",
    )(o, lo, w, w, x)


def _glu_kernel(y_ref, w1_ref, w2_ref, b1_ref, b2_ref, x_ref, out_ref):
    y = y_ref[...]
    z1 = _bdot(y, w1_ref[...].astype(BF16)) + b1_ref[...]
    z2 = _bdot(y, w2_ref[...].astype(BF16)) + b2_ref[...]
    out_ref[...] = x_ref[...] + z1 * jax.nn.sigmoid(z2)


def _glu(y, w, o, b, x):
    m, d = x.shape
    tm = _pick_tile(m, 1100, 16)
    tn = _pick_tile(d, 256, V7X_LANES)
    nj = d // tn
    vmem = tm * d * 2 + 4 * d * tn * 4 + 4 * tm * tn * 4 + 2 * d * tn * 2 + 2 * tm * tn * 4
    return pl.pallas_call(
        _glu_kernel,
        grid=(m // tm, nj),
        in_specs=[
            _once((tm, d), lambda i, j: (i, 0)),
            pl.BlockSpec((None, d, tn), lambda i, j: (o, 0, j)),
            pl.BlockSpec((None, d, tn), lambda i, j: (o, 0, nj + j)),
            pl.BlockSpec((1, tn), lambda i, j: (0, j)),
            pl.BlockSpec((1, tn), lambda i, j: (0, nj + j)),
            pl.BlockSpec((tm, tn), lambda i, j: (i, j)),
        ],
        out_specs=pl.BlockSpec((tm, tn), lambda i, j: (i, j)),
        out_shape=jax.ShapeDtypeStruct((m, d), F32),
        compiler_params=_params(("parallel", "arbitrary"), vmem),
        name="glu",
    )(y, w, w, b.reshape(1, 2 * d), b.reshape(1, 2 * d), x)


def _ffn_kernel(x_ref, g_ref, gf_ref, wg_ref, wu_ref, wd_ref, out_ref, xn_ref, *, final_norm):
    j = pl.program_id(1)

    @pl.when(j == 0)
    def _():
        _norm_rows_into(x_ref, g_ref, xn_ref)
        out_ref[...] = x_ref[...]

    xn = xn_ref[...]
    gate = _bdot(xn, wg_ref[...].astype(BF16))
    up = _bdot(xn, wu_ref[...].astype(BF16))
    act = (gate * jax.nn.sigmoid(gate) * up).astype(BF16)
    out_ref[...] += _bdot(act, wd_ref[...].astype(BF16))

    if final_norm:
        @pl.when(j == pl.num_programs(1) - 1)
        def _():
            _norm_rows_into(out_ref, gf_ref, out_ref)


def _ffn(x, g, w_gu, w_down, layer, g_final=None):
    m, d = x.shape
    hidden = w_down.shape[1]
    tm = _pick_tile(m, 1100, 16)
    th = _pick_tile(hidden, 256, V7X_LANES)
    nh = hidden // th
    vmem = (2 * tm * d * 4 + tm * d * 2 + 6 * d * th * 4 + 3 * d * th * 2
            + 3 * tm * th * 4)
    gf = g if g_final is None else g_final
    return pl.pallas_call(
        functools.partial(_ffn_kernel, final_norm=g_final is not None),
        grid=(m // tm, nh),
        in_specs=[
            _once((tm, d), lambda i, j: (i, 0)),
            _once((1, d), lambda i, j: (0, 0)),
            _once((1, d), lambda i, j: (0, 0)),
            pl.BlockSpec((None, d, th), lambda i, j: (layer, 0, j)),
            pl.BlockSpec((None, d, th), lambda i, j: (layer, 0, nh + j)),
            pl.BlockSpec((None, th, d), lambda i, j: (layer, j, 0)),
        ],
        out_specs=_once((tm, d), lambda i, j: (i, 0)),
        out_shape=jax.ShapeDtypeStruct((m, d), F32),
        scratch_shapes=[pltpu.VMEM((tm, d), BF16)],
        compiler_params=_params(("parallel", "arbitrary"), vmem),
        name="ffn",
    )(x, g.reshape(1, d), gf.reshape(1, d), w_gu, w_gu, w_down)


def _rope_tables(pos, half):
    inv = 1.0 / jnp.power(ROPE_BASE, jnp.linspace(0.0, 1.0, half, dtype=F32))
    ang = pos.astype(F32)[:, None] * inv[None, :]
    return jnp.cos(ang), jnp.sin(ang)


def _decay_tables(c, dk):
    log_g = jnp.log1p(-jnp.exp2(-5.0 - jnp.arange(RET_HEADS, dtype=F32)))
    idx = jnp.arange(c, dtype=F32)
    diff = idx[:, None] - idx[None, :]
    mask = jnp.where(diff[None] >= 0,
                     jnp.exp(jnp.maximum(diff, 0.0)[None] * log_g[:, None, None]), 0.0)
    q_dec = jnp.exp((idx[None, :] + 1.0) * log_g[:, None])
    k_dec = jnp.exp((c - 1.0 - idx)[None, :] * log_g[:, None])
    chunk_dec = jnp.exp(c * log_g)
    q_dec = jnp.broadcast_to(q_dec[:, :, None], (RET_HEADS, c, dk))
    k_dec = jnp.broadcast_to(k_dec[:, :, None], (RET_HEADS, c, dk))
    chunk_dec = jnp.broadcast_to(chunk_dec[:, None, None], (RET_HEADS, 1, dk))
    return mask, q_dec, k_dec, chunk_dec


def _rotate(x, cos, sin):
    half = x.shape[-1] // 2
    x1, x2 = x[:, :half], x[:, half:]
    return jnp.concatenate([x1 * cos - x2 * sin, x2 * cos + x1 * sin], axis=-1)


def _ret_prompt_kernel(q_ref, k_ref, v_ref, g_ref, cos_ref, sin_ref, mask_ref, qd_ref,
                       kd_ref, cd_ref, o_ref, s_out_ref, s_ref, *, n_chunks, scale):
    c = pl.program_id(2)

    @pl.when(c == 0)
    def _():
        s_ref[...] = jnp.zeros_like(s_ref)

    cos, sin = cos_ref[...], sin_ref[...]
    q = _rotate(q_ref[...], cos, sin)
    k = _rotate(k_ref[...], cos, sin) * scale
    qb, kb, vb = q.astype(BF16), k.astype(BF16), v_ref[...].astype(BF16)
    s = s_ref[...]
    scores = lax.dot_general(qb, kb, (((1,), (1,)), ((), ())),
                             preferred_element_type=F32) * mask_ref[0]
    o = _bdot(scores.astype(BF16), vb) + _bdot(qb, s.astype(BF16)) * qd_ref[0]
    kdb = (k * kd_ref[0]).astype(BF16)
    s_new = s * cd_ref[0] + lax.dot_general(kdb, vb, (((0,), (0,)), ((), ())),
                                            preferred_element_type=F32)
    s_ref[...] = s_new
    o = o * lax.rsqrt(jnp.mean(o * o, axis=-1, keepdims=True) + EPS)
    gr = g_ref[...]
    o_ref[...] = (o * (gr * jax.nn.sigmoid(gr))).astype(o_ref.dtype)

    @pl.when(c == n_chunks - 1)
    def _():
        s_out_ref[0, 0] = s_new


def _ret_prompt(proj, b, t, half):
    dk = half // RET_HEADS
    c = RET_CHUNK
    assert t % c == 0
    nc = t // c
    cos, sin = _rope_tables(jnp.arange(t), dk // 2)
    mask, q_dec, k_dec, chunk_dec = _decay_tables(c, dk)
    h = RET_HEADS

    def col(off):
        return pl.BlockSpec((c, dk), lambda bi, hi, ci: (bi * nc + ci, off + hi))

    def per_head(shape):
        return pl.BlockSpec((1,) + shape, lambda bi, hi, ci: (hi, 0, 0))

    vmem = 8 * c * dk * 4 + 2 * c * dk * 2 + 6 * dk * dk * 4 + 8 * c * dk * 4
    return pl.pallas_call(
        functools.partial(_ret_prompt_kernel, n_chunks=nc, scale=dk ** -0.5),
        grid=(b, h, nc),
        in_specs=[
            col(0), col(h), col(2 * h), col(3 * h),
            pl.BlockSpec((c, dk // 2), lambda bi, hi, ci: (ci, 0)),
            pl.BlockSpec((c, dk // 2), lambda bi, hi, ci: (ci, 0)),
            per_head((c, c)), per_head((c, dk)), per_head((c, dk)), per_head((1, dk)),
        ],
        out_specs=[
            pl.BlockSpec((c, dk), lambda bi, hi, ci: (bi * nc + ci, hi)),
            pl.BlockSpec((1, 1, dk, dk), lambda bi, hi, ci: (bi, hi, 0, 0)),
        ],
        out_shape=[jax.ShapeDtypeStruct((b * t, half), BF16),
                   jax.ShapeDtypeStruct((b, h, dk, dk), F32)],
        scratch_shapes=[pltpu.VMEM((dk, dk), F32)],
        compiler_params=_params(("parallel", "parallel", "arbitrary"), vmem),
        name="ret_prompt",
    )(proj, proj, proj, proj, cos, sin, mask, q_dec, k_dec, chunk_dec)


def _ret_sample_kernel(*refs, bb, scale, chained):
    (q_ref, k_ref, v_ref, g_ref, cos_ref, sin_ref, mask_ref, qd_ref, kd_ref, cd_ref,
     s_in_ref) = refs[:11]
    o_ref, s_out_ref, kdt_ref, acc_ref = refs[12 if chained else 11:]
    dt, _, dk = q_ref.shape
    rows = dt * bb
    cos, sin = cos_ref[...], sin_ref[...]
    q = _rotate(q_ref[...].reshape(rows, dk), cos, sin)
    k = _rotate(k_ref[...].reshape(rows, dk), cos, sin) * scale
    qb, kb = q.astype(BF16), k.astype(BF16)
    vb = v_ref[...].reshape(rows, dk).astype(BF16)
    scores = lax.dot_general(qb, kb, (((1,), (1,)), ((), ())),
                             preferred_element_type=F32) * mask_ref[0]
    acc_ref[...] = _bdot(scores.astype(BF16), vb)
    kdt_ref[...] = (k * kd_ref[0]).T
    qd = qd_ref[0]
    cd = cd_ref[0]
    row_batch = lax.broadcasted_iota(jnp.int32, (rows, 1), 0) % bb
    col_batch = lax.broadcasted_iota(jnp.int32, (1, rows), 1) % bb

    def body(j, carry):
        s = s_in_ref[0, j, 0]
        inter = _bdot(qb, s.astype(BF16)) * qd
        acc_ref[...] += jnp.where(row_batch == j, inter, 0.0)
        kdt_j = jnp.where(col_batch == j, kdt_ref[...], 0.0).astype(BF16)
        s_out_ref[0, j, 0] = s * cd + _bdot(kdt_j, vb)
        return carry

    lax.fori_loop(0, bb, body, 0)
    o = acc_ref[...]
    o = o * lax.rsqrt(jnp.mean(o * o, axis=-1, keepdims=True) + EPS)
    gr = g_ref[...].reshape(rows, dk)
    o_ref[...] = (o * (gr * jax.nn.sigmoid(gr))).astype(o_ref.dtype).reshape(dt, bb, dk)


def _ret_sample(proj, state_ret, e, prev_states, db, dt, half, past_len):
    dk = half // RET_HEADS
    h = RET_HEADS
    assert RET_CHUNK % dt == 0 and RET_SAMPLE_ROWS % dt == 0
    rows = RET_SAMPLE_ROWS
    bb = rows // dt
    assert db % bb == 0
    proj3 = proj.reshape(dt, db, proj.shape[1])
    cos, sin = _rope_tables(past_len + jnp.arange(dt), dk // 2)
    cos, sin = jnp.repeat(cos, bb, axis=0), jnp.repeat(sin, bb, axis=0)
    mask, q_dec, k_dec, chunk_dec = _decay_tables(dt, dk)
    mask = jnp.einsum("hnm,ab->hnamb", mask, jnp.eye(bb, dtype=F32)).reshape(h, rows, rows)
    q_dec, k_dec = jnp.repeat(q_dec, bb, axis=1), jnp.repeat(k_dec, bb, axis=1)

    def col(off):
        return pl.BlockSpec((dt, bb, dk), lambda bi, hi: (0, bi, off + hi))

    def per_head(shape):
        return pl.BlockSpec((1,) + shape, lambda bi, hi: (hi, 0, 0))

    state_block = pl.BlockSpec((1, bb, 1, dk, dk), lambda bi, hi: (e, bi, hi, 0, 0))
    chained = prev_states is not None
    operands = [proj3, proj3, proj3, proj3, cos, sin, mask, q_dec, k_dec, chunk_dec, state_ret]
    in_specs = [
        col(0), col(h), col(2 * h), col(3 * h),
        pl.BlockSpec((rows, dk // 2), lambda bi, hi: (0, 0)),
        pl.BlockSpec((rows, dk // 2), lambda bi, hi: (0, 0)),
        per_head((rows, rows)), per_head((rows, dk)), per_head((rows, dk)),
        per_head((1, dk)), state_block,
    ]
    if chained:
        operands.append(prev_states)
        in_specs.append(pl.BlockSpec(memory_space=pl.ANY))
    vmem = 4 * bb * dk * dk * 4 + 12 * rows * dk * 4 + 4 * dk * dk * 4
    o, states = pl.pallas_call(
        functools.partial(_ret_sample_kernel, bb=bb, scale=dk ** -0.5, chained=chained),
        grid=(db // bb, h),
        in_specs=in_specs,
        out_specs=[
            pl.BlockSpec((dt, bb, dk), lambda bi, hi: (0, bi, hi)),
            state_block,
        ],
        out_shape=[jax.ShapeDtypeStruct((dt, db, half), BF16),
                   jax.ShapeDtypeStruct(state_ret.shape, state_ret.dtype)],
        scratch_shapes=[pltpu.VMEM((dk, rows), F32), pltpu.VMEM((rows, dk), F32)],
        input_output_aliases={len(operands) - 1: 1} if chained else {},
        compiler_params=_params(("parallel", "parallel"), vmem),
        name="ret_sample",
    )(*operands)
    return o.reshape(dt * db, half), states


def _lru_gates(xc, wa_ref, ba_ref, wx_ref, bx_ref, sp_ref):
    xcb = xc.astype(BF16)
    blk = wa_ref.shape[1]
    ra, ri = [], []
    for n in range(wa_ref.shape[0]):
        xn = xcb[:, n * blk:(n + 1) * blk]
        ra.append(_bdot(xn, wa_ref[n].astype(BF16)))
        ri.append(_bdot(xn, wx_ref[n].astype(BF16)))
    r = jax.nn.sigmoid(jnp.concatenate(ra, axis=-1) + ba_ref[...])
    i = jax.nn.sigmoid(jnp.concatenate(ri, axis=-1) + bx_ref[...])
    log_a = -LRU_C * r * sp_ref[...]
    a = jnp.exp(log_a)
    mult = jnp.sqrt(jnp.maximum(-jnp.tanh(log_a) * (a * a + 1.0), 0.0))
    return a, mult * i * xc


def _lru_prompt_kernel(xl_ref, yl_ref, cw_ref, cb_ref, wa_ref, ba_ref, wx_ref, bx_ref,
                       sp_ref, lo_ref, h_out_ref, conv_out_ref, xs_ref, a_ref, b_ref,
                       hc_ref, *, n_chunks):
    c = pl.program_id(1)
    tc, w = xl_ref.shape
    sub = V7X_SUBLANES

    @pl.when(c == 0)
    def _():
        xs_ref[0:sub, :] = jnp.zeros((sub, w), F32)
        hc_ref[...] = jnp.zeros_like(hc_ref)

    x = xl_ref[...]
    xs_ref[sub:sub + tc, :] = x
    xc = cb_ref[...] + x * cw_ref[CONV_W - 1:CONV_W, :]
    for i in range(CONV_W - 1):
        back = CONV_W - 1 - i
        xc = xc + xs_ref[sub - back:sub - back + tc, :] * cw_ref[i:i + 1, :]
    xs_ref[0:sub, :] = xs_ref[tc:tc + sub, :]

    a, bt = _lru_gates(xc, wa_ref, ba_ref, wx_ref, bx_ref, sp_ref)

    a3 = a.reshape(tc // sub, sub, w)
    b3 = bt.reshape(tc // sub, sub, w)
    step = lax.broadcasted_iota(jnp.int32, (1, sub, 1), 1)
    for s in (1, 2, 4):
        keep = step >= s
        a_prev = jnp.where(keep, pltpu.roll(a3, s, axis=1), 1.0)
        b_prev = jnp.where(keep, pltpu.roll(b3, s, axis=1), 0.0)
        b3 = a3 * b_prev + b3
        a3 = a3 * a_prev
    a_ref[...] = a3.reshape(tc, w)
    b_ref[...] = b3.reshape(tc, w)

    def body(g, h):
        sl = pl.ds(pl.multiple_of(g * sub, sub), sub)
        hg = b_ref[sl, :] + a_ref[sl, :] * h
        b_ref[sl, :] = hg
        return jnp.broadcast_to(hg[sub - 1:sub, :], (sub, w))

    h_last = lax.fori_loop(0, tc // sub, body, hc_ref[...])
    hc_ref[...] = h_last
    lo_ref[...] = (jax.nn.gelu(yl_ref[...]) * b_ref[...]).astype(lo_ref.dtype)

    @pl.when(c == n_chunks - 1)
    def _():
        h_out_ref[0] = h_last[0:1, :]
        conv_out_ref[0] = xs_ref[sub - (CONV_W - 1):sub, :]


def _lru_prompt(proj, b, t, w, cw, cb, wa, ba, wx, bx, sp):
    tc = _pick_tile(t, 256, 16)
    nc = t // tc
    xl_col = (proj.shape[1] - 2 * w) // w
    vec = lambda: pl.BlockSpec((1, w), lambda bi, ci: (0, 0))
    blocks = lambda: pl.BlockSpec(wa.shape, lambda bi, ci: (0, 0, 0))
    vmem = 4 * tc * w * 4 + 2 * tc * w * 2 + 3 * tc * w * 4 + 8 * tc * w * 4
    return pl.pallas_call(
        functools.partial(_lru_prompt_kernel, n_chunks=nc),
        grid=(b, nc),
        in_specs=[
            pl.BlockSpec((tc, w), lambda bi, ci: (bi * nc + ci, xl_col)),
            pl.BlockSpec((tc, w), lambda bi, ci: (bi * nc + ci, xl_col + 1)),
            pl.BlockSpec((CONV_W, w), lambda bi, ci: (0, 0)),
            vec(), blocks(), vec(), blocks(), vec(), vec(),
        ],
        out_specs=[
            pl.BlockSpec((tc, w), lambda bi, ci: (bi * nc + ci, 0)),
            pl.BlockSpec((1, 1, w), lambda bi, ci: (bi, 0, 0)),
            pl.BlockSpec((1, CONV_W - 1, w), lambda bi, ci: (bi, 0, 0)),
        ],
        out_shape=[jax.ShapeDtypeStruct((b * t, w), BF16),
                   jax.ShapeDtypeStruct((b, 1, w), F32),
                   jax.ShapeDtypeStruct((b, CONV_W - 1, w), F32)],
        scratch_shapes=[pltpu.VMEM((tc + V7X_SUBLANES, w), F32),
                        pltpu.VMEM((tc, w), F32), pltpu.VMEM((tc, w), F32),
                        pltpu.VMEM((V7X_SUBLANES, w), F32)],
        compiler_params=_params(("parallel", "arbitrary"), vmem),
        name="lru_prompt",
    )(proj, proj, cw, cb.reshape(1, w), wa, ba.reshape(1, w), wx, bx.reshape(1, w), sp)


def _lru_sample_kernel(xl_ref, yl_ref, conv0_ref, h0_ref, cw_ref, cb_ref, wa_ref, ba_ref,
                       wx_ref, bx_ref, sp_ref, lo_ref, h_out_ref, conv_out_ref, *, dt):
    db = h0_ref.shape[1]
    w = h0_ref.shape[2]
    taps = CONV_W - 1
    xp = [conv0_ref[0, :, i * w:(i + 1) * w] for i in range(taps)]
    xp += [xl_ref[t * db:(t + 1) * db, :] for t in range(dt)]
    xcs = []
    for t in range(dt):
        xc = cb_ref[...] + xp[t] * cw_ref[0:1, :]
        for i in range(1, CONV_W):
            xc = xc + xp[t + i] * cw_ref[i:i + 1, :]
        xcs.append(xc)
    a, bt = _lru_gates(jnp.concatenate(xcs, axis=0), wa_ref, ba_ref, wx_ref, bx_ref, sp_ref)
    h = h0_ref[0]
    for t in range(dt):
        rows = slice(t * db, (t + 1) * db)
        h = a[rows] * h + bt[rows]
        lo_ref[rows, :] = (jax.nn.gelu(yl_ref[rows, :]) * h).astype(lo_ref.dtype)
    h_out_ref[...] = h
    for i in range(taps):
        conv_out_ref[:, i * w:(i + 1) * w] = xp[dt + i]


def _lru_sample(proj, state_conv, state_lru, e, db, dt, w, cw, cb, wa, ba, wx, bx, sp):
    rows = db * dt
    assert proj.shape[0] == rows
    rb = 0
    xl_col = (proj.shape[1] - 2 * w) // w
    taps = CONV_W - 1
    conv0 = state_conv.reshape(state_conv.shape[0], db, taps * w)
    vec = lambda: pl.BlockSpec((1, w), lambda i: (0, 0))
    blocks = lambda: pl.BlockSpec(wa.shape, lambda i: (0, 0, 0))
    vmem = 16 * rows * w * 4
    lo, h_new, conv_new = pl.pallas_call(
        functools.partial(_lru_sample_kernel, dt=dt),
        grid=(1,),
        in_specs=[
            pl.BlockSpec((rows, w), lambda i: (rb, xl_col)),
            pl.BlockSpec((rows, w), lambda i: (rb, xl_col + 1)),
            pl.BlockSpec((1, db, taps * w), lambda i: (e, 0, 0)),
            pl.BlockSpec((1, db, w), lambda i: (e, 0, 0)),
            pl.BlockSpec((CONV_W, w), lambda i: (0, 0)),
            vec(), blocks(), vec(), blocks(), vec(), vec(),
        ],
        out_specs=[
            pl.BlockSpec((rows, w), lambda i: (0, 0)),
            pl.BlockSpec((db, w), lambda i: (0, 0)),
            pl.BlockSpec((db, taps * w), lambda i: (0, 0)),
        ],
        out_shape=[jax.ShapeDtypeStruct((rows, w), BF16),
                   jax.ShapeDtypeStruct((db, w), F32),
                   jax.ShapeDtypeStruct((db, taps * w), F32)],
        compiler_params=_params(("arbitrary",), vmem),
        name="lru_sample",
    )(proj, proj, conv0, state_lru, cw, cb.reshape(1, w), wa, ba.reshape(1, w), wx,
      bx.reshape(1, w), sp)
    return lo, h_new, conv_new.reshape(db, taps, w)


def _s5_tables(a_re, a_im, b_re, b_im, c_re, c_im, d, log_dt):
    g = a_re.shape[0]
    nb = g // SSM_BLOCK_GROUPS
    dt = jnp.exp(log_dt)[:, None]
    mag = jnp.exp(a_re * dt)
    abr = mag * jnp.cos(a_im * dt)
    abi = mag * jnp.sin(a_im * dt)
    den = a_re * a_re + a_im * a_im
    nr, ni = abr - 1.0, abi
    fr = (nr * a_re + ni * a_im) / den
    fi = (ni * a_re - nr * a_im) / den
    bbr = fr[..., None] * b_re - fi[..., None] * b_im
    bbi = fr[..., None] * b_im + fi[..., None] * b_re
    eye = jnp.eye(SSM_BLOCK_GROUPS, dtype=F32)[None, :, None, :, None]

    def pack_in(bb):
        bb = bb.reshape(nb, SSM_BLOCK_GROUPS, SSM_P, SSM_GROUP).transpose(0, 1, 3, 2)
        return (bb[:, :, :, None, :] * eye).reshape(nb, V7X_MXU_DIM, SSM_BLOCK_STATES)

    def pack_out(cc):
        cc = cc.reshape(nb, SSM_BLOCK_GROUPS, SSM_GROUP, SSM_P).transpose(0, 1, 3, 2)
        return (cc[:, :, :, None, :] * eye).reshape(nb, SSM_BLOCK_STATES, V7X_MXU_DIM)

    w_in = jnp.concatenate([pack_in(bbr), pack_in(bbi)], axis=-1).astype(BF16)
    w_out = jnp.concatenate([pack_out(c_re), pack_out(-c_im)], axis=1).astype(BF16)
    return (abr.reshape(nb, SSM_BLOCK_STATES), abi.reshape(nb, SSM_BLOCK_STATES),
            w_in, w_out, d.reshape(nb, 1, V7X_MXU_DIM))


def _s5_prompt_kernel(x_ref, g_ref, ar_ref, ai_ref, win_ref, wout_ref, d_ref, y_ref,
                      hr_out_ref, hi_out_ref, u_ref, sr_ref, si_ref, hr_ref, hi_ref, *,
                      n_chunks):
    c = pl.program_id(1)
    tc = x_ref.shape[0]
    nb = win_ref.shape[0]
    ns = SSM_BLOCK_STATES
    bw = V7X_MXU_DIM
    ln = V7X_LANES
    slots = V7X_SUBLANES
    nl = sr_ref.shape[0]
    splits = ns // (nl * ln)
    blocks_per_pass = slots // splits
    pitch = S5_ROW_PITCH

    @pl.when(c == 0)
    def _():
        hr_ref[...] = jnp.zeros_like(hr_ref)
        hi_ref[...] = jnp.zeros_like(hi_ref)

    u_ref[...] = _rms(x_ref[...], g_ref[...])

    def slot_rows(slot):
        return pl.ds(slot, tc, stride=pitch)

    for p in range(nb // blocks_per_pass):
        blocks = range(p * blocks_per_pass, (p + 1) * blocks_per_pass)
        for il, i in enumerate(blocks):
            bu = _bdot(u_ref[:, i * bw:(i + 1) * bw].astype(BF16), win_ref[i])
            for sp in range(splits):
                for l in range(nl):
                    col = (sp * nl + l) * ln
                    sr_ref[l, slot_rows(il * splits + sp), :] = bu[:, col:col + ln]
                    si_ref[l, slot_rows(il * splits + sp), :] = bu[:, ns + col:ns + col + ln]

        srows = slice(p * slots, (p + 1) * slots)
        ar = [ar_ref[srows, l * ln:(l + 1) * ln] for l in range(nl)]
        ai = [ai_ref[srows, l * ln:(l + 1) * ln] for l in range(nl)]

        def body(t, carry):
            rows = pl.ds(t * pitch, slots)
            out = []
            for l in range(nl):
                hr, hi = carry[l]
                hr_n = ar[l] * hr - ai[l] * hi + sr_ref[l, rows, :]
                hi_n = ar[l] * hi + ai[l] * hr + si_ref[l, rows, :]
                sr_ref[l, rows, :] = hr_n
                si_ref[l, rows, :] = hi_n
                out.append((hr_n, hi_n))
            return tuple(out)

        init = tuple((hr_ref[srows, l * ln:(l + 1) * ln], hi_ref[srows, l * ln:(l + 1) * ln])
                     for l in range(nl))
        last = lax.fori_loop(0, tc, body, init, unroll=8)
        for l in range(nl):
            hr_ref[srows, l * ln:(l + 1) * ln] = last[l][0]
            hi_ref[srows, l * ln:(l + 1) * ln] = last[l][1]

        for il, i in enumerate(blocks):
            parts = [ref[l, slot_rows(il * splits + sp), :].astype(BF16)
                     for ref in (sr_ref, si_ref) for sp in range(splits) for l in range(nl)]
            cols = slice(i * bw, (i + 1) * bw)
            y = _bdot(jnp.concatenate(parts, axis=-1), wout_ref[i]) + d_ref[i] * u_ref[:, cols]
            y_ref[:, cols] = jax.nn.gelu(y).astype(y_ref.dtype)

    @pl.when(c == n_chunks - 1)
    def _():
        hr_out_ref[0] = hr_ref[...]
        hi_out_ref[0] = hi_ref[...]


def _s5_prompt(x, g, tables, b, t):
    d = x.shape[1]
    abr, abi, w_in, w_out, dd = tables
    nb = w_in.shape[0]
    ns = SSM_BLOCK_STATES
    tc = _pick_tile(t, 256, 16)
    nc = t // tc
    passes = 2
    assert nb % passes == 0 and V7X_SUBLANES % (nb // passes) == 0
    splits = V7X_SUBLANES // (nb // passes)
    slot_lanes = ns // splits
    nl = slot_lanes // V7X_LANES
    abr = abr.reshape(nb * splits, slot_lanes)
    abi = abi.reshape(nb * splits, slot_lanes)
    const = lambda a: _once(a.shape, lambda bi, ci: (0,) * a.ndim)
    scan_bytes = nl * tc * S5_ROW_PITCH * V7X_LANES * 4
    vmem = (5 * tc * d * 4 + 2 * tc * d * 2 + w_in.size * 2 + w_out.size * 2
            + 2 * scan_bytes + 6 * tc * 2 * ns * 4)
    y, hr, hi = pl.pallas_call(
        functools.partial(_s5_prompt_kernel, n_chunks=nc),
        grid=(b, nc),
        in_specs=[
            pl.BlockSpec((tc, d), lambda bi, ci: (bi * nc + ci, 0)),
            pl.BlockSpec((1, d), lambda bi, ci: (0, 0)),
            const(abr), const(abi), const(w_in), const(w_out), const(dd),
        ],
        out_specs=[
            pl.BlockSpec((tc, d), lambda bi, ci: (bi * nc + ci, 0)),
            pl.BlockSpec((1,) + abr.shape, lambda bi, ci: (bi, 0, 0)),
            pl.BlockSpec((1,) + abr.shape, lambda bi, ci: (bi, 0, 0)),
        ],
        out_shape=[jax.ShapeDtypeStruct((b * t, d), BF16),
                   jax.ShapeDtypeStruct((b,) + abr.shape, F32),
                   jax.ShapeDtypeStruct((b,) + abr.shape, F32)],
        scratch_shapes=[pltpu.VMEM((tc, d), F32),
                        pltpu.VMEM((nl, tc * S5_ROW_PITCH, V7X_LANES), F32),
                        pltpu.VMEM((nl, tc * S5_ROW_PITCH, V7X_LANES), F32),
                        pltpu.VMEM(abr.shape, F32), pltpu.VMEM(abr.shape, F32)],
        compiler_params=_params(("parallel", "arbitrary"), vmem),
        name="s5_prompt",
    )(x, g.reshape(1, d), abr, abi, w_in, w_out, dd)
    return y, hr.reshape(b, nb * ns), hi.reshape(b, nb * ns)


def _s5_sample_kernel(x_ref, g_ref, ar_ref, ai_ref, win_ref, wout_ref, d_ref, h0r_ref,
                      h0i_ref, y_ref, hr_out_ref, hi_out_ref, uf_ref, ub_ref, *, dt):
    i = pl.program_id(0)
    nb = uf_ref.shape[0]
    bw = V7X_MXU_DIM
    ns = SSM_BLOCK_STATES
    db = h0r_ref.shape[1]

    @pl.when(i == 0)
    def _():
        u = _rms(x_ref[...], g_ref[...])
        for n in range(nb):
            uf_ref[n] = u[:, n * bw:(n + 1) * bw]
            ub_ref[n] = u[:, n * bw:(n + 1) * bw].astype(BF16)

    bu = _bdot(ub_ref[i], win_ref[0])
    ar, ai = ar_ref[pl.ds(i, 1), :], ai_ref[pl.ds(i, 1), :]
    hr, hi = h0r_ref[0], h0i_ref[0]
    states = []
    for t in range(dt):
        rows = slice(t * db, (t + 1) * db)
        hr, hi = (ar * hr - ai * hi + bu[rows, :ns], ar * hi + ai * hr + bu[rows, ns:])
        states.append(jnp.concatenate([hr.astype(BF16), hi.astype(BF16)], axis=-1))
    y = _bdot(jnp.concatenate(states, axis=0), wout_ref[0]) + d_ref[0] * uf_ref[i]
    y_ref[...] = jax.nn.gelu(y).astype(y_ref.dtype)
    hr_out_ref[...] = hr
    hi_out_ref[...] = hi


def _s5_sample(x, g, tables, state_re, state_im, o, db, dt):
    d = x.shape[1]
    abr, abi, w_in, w_out, dd = tables
    nb = w_in.shape[0]
    ns = SSM_BLOCK_STATES
    bw = V7X_MXU_DIM
    rows = db * dt
    assert x.shape[0] == rows
    rb = 0
    no = state_re.shape[0]
    h0r = state_re.reshape(no, db, nb * ns)
    h0i = state_im.reshape(no, db, nb * ns)
    blk = lambda a: pl.BlockSpec((1,) + a.shape[1:], lambda i: (i, 0, 0))
    vmem = (2 * rows * d * 4 + rows * d * 6 + 4 * bw * 2 * ns * 2 + 8 * db * ns * 4
            + 8 * rows * 2 * ns * 4)
    return pl.pallas_call(
        functools.partial(_s5_sample_kernel, dt=dt),
        grid=(nb,),
        in_specs=[
            _once((rows, d), lambda i: (rb, 0)),
            pl.BlockSpec((1, d), lambda i: (0, 0)),
            pl.BlockSpec(abr.shape, lambda i: (0, 0)),
            pl.BlockSpec(abi.shape, lambda i: (0, 0)),
            blk(w_in), blk(w_out), blk(dd),
            pl.BlockSpec((1, db, ns), lambda i: (o, 0, i)),
            pl.BlockSpec((1, db, ns), lambda i: (o, 0, i)),
        ],
        out_specs=[
            pl.BlockSpec((rows, bw), lambda i: (0, i)),
            pl.BlockSpec((db, ns), lambda i: (0, i)),
            pl.BlockSpec((db, ns), lambda i: (0, i)),
        ],
        out_shape=[jax.ShapeDtypeStruct((rows, d), BF16),
                   jax.ShapeDtypeStruct((db, nb * ns), F32),
                   jax.ShapeDtypeStruct((db, nb * ns), F32)],
        scratch_shapes=[pltpu.VMEM((nb, rows, bw), F32), pltpu.VMEM((nb, rows, bw), BF16)],
        compiler_params=_params(("arbitrary",), vmem),
        name="s5_sample",
    )(x, g.reshape(1, d), abr, abi, w_in, w_out, dd, h0r, h0i)


def kernel(x_prompt, x_sample, state_ret, state_lru, state_conv, state_ssm_re, state_ssm_im, norm_mix_even, w_in_even, lru_conv_w, lru_conv_b, lru_wa, lru_ba, lru_wx, lru_bx, lru_lambda, w_out_even, norm_mix_odd, ssm_a_re, ssm_a_im, ssm_b_re, ssm_b_im, ssm_c_re, ssm_c_im, ssm_d, ssm_log_dt, w_glu, b_glu, norm_ffn, w_ffn_gu, w_ffn_down, norm_final):
    b, t, d = x_prompt.shape
    db, dt, _ = x_sample.shape
    depth = norm_ffn.shape[0]
    half = d // 2
    past_len = PAST_LEN
    groups, ssm_p = ssm_a_re.shape[1:]
    assert ssm_p == SSM_P and groups * SSM_GROUP == d and groups % SSM_BLOCK_GROUPS == 0

    xp = x_prompt.reshape(b * t, d)
    xs = x_sample.transpose(1, 0, 2).reshape(dt * db, d)

    rets_p, ret_s, lrus_p, lrus_s, convs_p, convs_s = [], None, [], [], [], []
    sres_p, sres_s, sims_p, sims_s = [], [], [], []
    for layer in range(depth):
        if layer % 2 == 0:
            e = layer // 2
            sp = jax.nn.softplus(-lru_lambda[e]).reshape(1, half)
            lru_w = (lru_conv_w[e], lru_conv_b[e], lru_wa[e], lru_ba[e], lru_wx[e], lru_bx[e], sp)

            proj_p = _inproj(xp, norm_mix_even[e], w_in_even, e)
            o_p, ret_p = _ret_prompt(proj_p, b, t, half)
            lo_p, lru_p, conv_p = _lru_prompt(proj_p, b, t, half, *lru_w)
            xp = _outproj(o_p, lo_p, w_out_even, e, xp)

            proj_s = _inproj(xs, norm_mix_even[e], w_in_even, e)
            o_s, ret_s = _ret_sample(proj_s, state_ret, e, ret_s, db, dt, half, past_len)
            lo_s, lru_s, conv_s = _lru_sample(proj_s, state_conv, state_lru, e, db, dt, half,
                                              *lru_w)
            xs = _outproj(o_s, lo_s, w_out_even, e, xs)

            rets_p.append(ret_p)
            lrus_p.append(lru_p.reshape(b, half))
            lrus_s.append(lru_s)
            convs_p.append(conv_p)
            convs_s.append(conv_s)
        else:
            o = layer // 2
            tables = _s5_tables(ssm_a_re[o], ssm_a_im[o], ssm_b_re[o], ssm_b_im[o],
                                ssm_c_re[o], ssm_c_im[o], ssm_d[o], ssm_log_dt[o])
            y_p, sre_p, sim_p = _s5_prompt(xp, norm_mix_odd[o], tables, b, t)
            xp = _glu(y_p, w_glu, o, b_glu[o], xp)
            y_s, sre_s, sim_s = _s5_sample(xs, norm_mix_odd[o], tables, state_ssm_re,
                                           state_ssm_im, o, db, dt)
            xs = _glu(y_s, w_glu, o, b_glu[o], xs)
            sres_p.append(sre_p.reshape(b, groups, ssm_p))
            sims_p.append(sim_p.reshape(b, groups, ssm_p))
            sres_s.append(sre_s.reshape(db, groups, ssm_p))
            sims_s.append(sim_s.reshape(db, groups, ssm_p))
        g_final = norm_final if layer == depth - 1 else None
        xp = _ffn(xp, norm_ffn[layer], w_ffn_gu, w_ffn_down, layer, g_final)
        xs = _ffn(xs, norm_ffn[layer], w_ffn_gu, w_ffn_down, layer, g_final)

    y_prompt = xp.reshape(b, t, d)
    y_sample = xs.reshape(dt, db, d).transpose(1, 0, 2)
    return (y_prompt, y_sample, jnp.stack(rets_p), ret_s, jnp.stack(lrus_p),
            jnp.stack(lrus_s), jnp.stack(convs_p), jnp.stack(convs_s), jnp.stack(sres_p),
            jnp.stack(sres_s), jnp.stack(sims_p), jnp.stack(sims_s))
```

```python
import functools

import jax
import jax.numpy as jnp
import numpy as np
from jax import lax
from jax.experimental import pallas as pl
from jax.experimental.pallas import tpu as pltpu

F32 = jnp.float32
BF16 = jnp.bfloat16

EPS = 1e-6
PAST_LEN = 16384
ROPE_BASE = 10000.0
RET_HEADS = 4
RET_CHUNK = 128
LRU_HEADS = 8
LRU_C = 8.0
CONV_W = 4
SSM_GROUP = 16
SSM_P = 64
SSM_CHUNK = 128

V7X_SUBLANES = 8
V7X_LANES = 128
V7X_MXU_DIM = 256
V7X_VMEM_BYTES = 64 * 1024 * 1024
VMEM_LIMIT_CAP = V7X_VMEM_BYTES - 6 * 1024 * 1024

SSM_BLOCK_GROUPS = V7X_MXU_DIM // SSM_GROUP
SSM_BLOCK_STATES = SSM_BLOCK_GROUPS * SSM_P

RET_SAMPLE_ROWS = 128

S5_ROW_PITCH = 12


def _pick_tile(n, target, mult):
    best = None
    for t in range(mult, min(n, target) + 1, mult):
        if n % t == 0:
            best = t
    assert best is not None, (n, target, mult)
    return best


def _params(semantics, vmem_bytes):
    limit = min(int(vmem_bytes * 1.2) + (6 << 20), VMEM_LIMIT_CAP)
    return pltpu.CompilerParams(dimension_semantics=semantics, vmem_limit_bytes=limit)


def _once(block_shape, index_map):
    return pl.BlockSpec(block_shape, index_map, pipeline_mode=pl.Buffered(1))


def _rms(x, g):
    return x * lax.rsqrt(jnp.mean(x * x, axis=-1, keepdims=True) + EPS) * g


def _norm_rows_into(x_ref, g_ref, xn_ref):
    tm = x_ref.shape[0]
    rows = _pick_tile(tm, 64, 16)

    def body(r, carry):
        sl = pl.ds(pl.multiple_of(r * rows, rows), rows)
        xn_ref[sl, :] = _rms(x_ref[sl, :], g_ref[...]).astype(xn_ref.dtype)
        return carry

    lax.fori_loop(0, tm // rows, body, 0)


def _bdot(a, b):
    return jnp.dot(a, b, preferred_element_type=F32)


def _inproj_kernel(x_ref, g_ref, w_ref, o_ref, xn_ref):
    @pl.when(pl.program_id(1) == 0)
    def _():
        _norm_rows_into(x_ref, g_ref, xn_ref)

    o_ref[...] = _bdot(xn_ref[...], w_ref[...].astype(BF16))


def _inproj(x, g, w, e):
    m, d = x.shape
    n = w.shape[2]
    tm = _pick_tile(m, 1100, 16)
    tn = _pick_tile(n, 512, V7X_LANES)
    vmem = tm * d * 4 + tm * d * 2 + 2 * d * tn * 4 + 2 * tm * tn * 4 + d * tn * 2
    return pl.pallas_call(
        _inproj_kernel,
        grid=(m // tm, n // tn),
        in_specs=[
            _once((tm, d), lambda i, j: (i, 0)),
            _once((1, d), lambda i, j: (0, 0)),
            pl.BlockSpec((None, d, tn), lambda i, j: (e, 0, j)),
        ],
        out_specs=pl.BlockSpec((tm, tn), lambda i, j: (i, j)),
        out_shape=jax.ShapeDtypeStruct((m, n), F32),
        scratch_shapes=[pltpu.VMEM((tm, d), BF16)],
        compiler_params=_params(("parallel", "arbitrary"), vmem),
        name="inproj",
    )(x, g.reshape(1, d), w)


def _outproj_kernel(o_ref, lo_ref, wo_ref, wl_ref, x_ref, out_ref):
    acc = _bdot(o_ref[...], wo_ref[...].astype(BF16))
    acc = acc + _bdot(lo_ref[...], wl_ref[...].astype(BF16))
    out_ref[...] = x_ref[...] + acc


def _outproj(o, lo, w, e, x):
    m, d = x.shape
    half = o.shape[1]
    tm = _pick_tile(m, 1100, 16)
    tn = _pick_tile(d, 512, V7X_LANES)
    vmem = 2 * tm * half * 2 + 4 * half * tn * 4 + 4 * tm * tn * 4 + 2 * half * tn * 2
    return pl.pallas_call(
        _outproj_kernel,
        grid=(m // tm, d // tn),
        in_specs=[
            _once((tm, half), lambda i, j: (i, 0)),
            _once((tm, half), lambda i, j: (i, 0)),
            pl.BlockSpec((None, half, tn), lambda i, j: (e, 0, j)),
            pl.BlockSpec((None, half, tn), lambda i, j: (e, 1, j)),
            pl.BlockSpec((tm, tn), lambda i, j: (i, j)),
        ],
        out_specs=pl.BlockSpec((tm, tn), lambda i, j: (i, j)),
        out_shape=jax.ShapeDtypeStruct((m, d), F32),
        compiler_params=_params(("parallel", "arbitrary"), vmem),
        name="outproj",
    )(o, lo, w, w, x)


def _glu_kernel(y_ref, w1_ref, w2_ref, b1_ref, b2_ref, x_ref, out_ref):
    y = y_ref[...]
    z1 = _bdot(y, w1_ref[...].astype(BF16)) + b1_ref[...]
    z2 = _bdot(y, w2_ref[...].astype(BF16)) + b2_ref[...]
    out_ref[...] = x_ref[...] + z1 * jax.nn.sigmoid(z2)


def _glu(y, w, o, b, x):
    m, d = x.shape
    tm = _pick_tile(m, 1100, 16)
    tn = _pick_tile(d, 256, V7X_LANES)
    nj = d // tn
    vmem = tm * d * 2 + 4 * d * tn * 4 + 4 * tm * tn * 4 + 2 * d * tn * 2 + 2 * tm * tn * 4
    return pl.pallas_call(
        _glu_kernel,
        grid=(m // tm, nj),
        in_specs=[
            _once((tm, d), lambda i, j: (i, 0)),
            pl.BlockSpec((None, d, tn), lambda i, j: (o, 0, j)),
            pl.BlockSpec((None, d, tn), lambda i, j: (o, 0, nj + j)),
            pl.BlockSpec((1, tn), lambda i, j: (0, j)),
            pl.BlockSpec((1, tn), lambda i, j: (0, nj + j)),
            pl.BlockSpec((tm, tn), lambda i, j: (i, j)),
        ],
        out_specs=pl.BlockSpec((tm, tn), lambda i, j: (i, j)),
        out_shape=jax.ShapeDtypeStruct((m, d), F32),
        compiler_params=_params(("parallel", "arbitrary"), vmem),
        name="glu",
    )(y, w, w, b.reshape(1, 2 * d), b.reshape(1, 2 * d), x)


def _ffn_kernel(x_ref, g_ref, gf_ref, wg_ref, wu_ref, wd_ref, out_ref, xn_ref, *, final_norm):
    j = pl.program_id(1)

    @pl.when(j == 0)
    def _():
        _norm_rows_into(x_ref, g_ref, xn_ref)
        out_ref[...] = x_ref[...]

    xn = xn_ref[...]
    gate = _bdot(xn, wg_ref[...].astype(BF16))
    up = _bdot(xn, wu_ref[...].astype(BF16))
    act = (gate * jax.nn.sigmoid(gate) * up).astype(BF16)
    out_ref[...] += _bdot(act, wd_ref[...].astype(BF16))

    if final_norm:
        @pl.when(j == pl.num_programs(1) - 1)
        def _():
            _norm_rows_into(out_ref, gf_ref, out_ref)


def _ffn(x, g, w_gu, w_down, layer, g_final=None):
    m, d = x.shape
    hidden = w_down.shape[1]
    tm = _pick_tile(m, 1100, 16)
    th = _pick_tile(hidden, 256, V7X_LANES)
    nh = hidden // th
    vmem = (2 * tm * d * 4 + tm * d * 2 + 6 * d * th * 4 + 3 * d * th * 2
            + 3 * tm * th * 4)
    gf = g if g_final is None else g_final
    return pl.pallas_call(
        functools.partial(_ffn_kernel, final_norm=g_final is not None),
        grid=(m // tm, nh),
        in_specs=[
            _once((tm, d), lambda i, j: (i, 0)),
            _once((1, d), lambda i, j: (0, 0)),
            _once((1, d), lambda i, j: (0, 0)),
            pl.BlockSpec((None, d, th), lambda i, j: (layer, 0, j)),
            pl.BlockSpec((None, d, th), lambda i, j: (layer, 0, nh + j)),
            pl.BlockSpec((None, th, d), lambda i, j: (layer, j, 0)),
        ],
        out_specs=_once((tm, d), lambda i, j: (i, 0)),
        out_shape=jax.ShapeDtypeStruct((m, d), F32),
        scratch_shapes=[pltpu.VMEM((tm, d), BF16)],
        compiler_params=_params(("parallel", "arbitrary"), vmem),
        name="ffn",
    )(x, g.reshape(1, d), gf.reshape(1, d), w_gu, w_gu, w_down)


def _rope_tables(pos, half):
    inv = 1.0 / np.power(ROPE_BASE, np.linspace(0.0, 1.0, half))
    ang = np.asarray(pos, np.float64)[:, None] * inv[None, :]
    return np.cos(ang).astype(np.float32), np.sin(ang).astype(np.float32)


def _decay_tables(c, dk):
    log_g = np.log1p(-np.exp2(-5.0 - np.arange(RET_HEADS, dtype=np.float64)))
    idx = np.arange(c, dtype=np.float64)
    diff = idx[:, None] - idx[None, :]
    mask = np.where(diff[None] >= 0,
                    np.exp(np.maximum(diff, 0.0)[None] * log_g[:, None, None]), 0.0)
    q_dec = np.exp((idx[None, :] + 1.0) * log_g[:, None])
    k_dec = np.exp((c - 1.0 - idx)[None, :] * log_g[:, None])
    chunk_dec = np.exp(c * log_g)
    q_dec = np.broadcast_to(q_dec[:, :, None], (RET_HEADS, c, dk))
    k_dec = np.broadcast_to(k_dec[:, :, None], (RET_HEADS, c, dk))
    chunk_dec = np.broadcast_to(chunk_dec[:, None, None], (RET_HEADS, 1, dk))
    f32 = lambda a: np.ascontiguousarray(a, dtype=np.float32)
    return f32(mask), f32(q_dec), f32(k_dec), f32(chunk_dec)


def _rotate(x, cos, sin):
    half = x.shape[-1] // 2
    x1, x2 = x[:, :half], x[:, half:]
    return jnp.concatenate([x1 * cos - x2 * sin, x2 * cos + x1 * sin], axis=-1)


def _ret_prompt_kernel(q_ref, k_ref, v_ref, g_ref, cos_ref, sin_ref, mask_ref, qd_ref,
                       kd_ref, cd_ref, o_ref, s_out_ref, s_ref, *, n_chunks, scale):
    c = pl.program_id(1)
    heads, dk, _ = s_ref.shape

    @pl.when(c == 0)
    def _():
        s_ref[...] = jnp.zeros_like(s_ref)

    cos, sin = cos_ref[...], sin_ref[...]
    for h in range(heads):
        cols = slice(h * dk, (h + 1) * dk)
        q = _rotate(q_ref[:, cols], cos, sin)
        k = _rotate(k_ref[:, cols], cos, sin) * scale
        qb, kb, vb = q.astype(BF16), k.astype(BF16), v_ref[:, cols].astype(BF16)
        s = s_ref[h]
        scores = lax.dot_general(qb, kb, (((1,), (1,)), ((), ())),
                                 preferred_element_type=F32) * mask_ref[h]
        o = _bdot(scores.astype(BF16), vb) + _bdot(qb, s.astype(BF16)) * qd_ref[h]
        kdb = (k * kd_ref[h]).astype(BF16)
        s_new = s * cd_ref[h] + lax.dot_general(kdb, vb, (((0,), (0,)), ((), ())),
                                                preferred_element_type=F32)
        s_ref[h] = s_new
        o = o * lax.rsqrt(jnp.mean(o * o, axis=-1, keepdims=True) + EPS)
        gr = g_ref[:, cols]
        o_ref[:, cols] = (o * (gr * jax.nn.sigmoid(gr))).astype(o_ref.dtype)

    @pl.when(c == n_chunks - 1)
    def _():
        s_out_ref[0] = s_ref[...]


def _ret_prompt(proj, b, t, half):
    h = RET_HEADS
    dk = half // h
    c = RET_CHUNK
    assert t % c == 0
    nc = t // c
    cos, sin = _rope_tables(np.arange(t), dk // 2)
    tables = _decay_tables(c, dk)

    def col(off):
        return pl.BlockSpec((c, half), lambda bi, ci: (bi * nc + ci, off))

    const = lambda a: pl.BlockSpec(a.shape, lambda bi, ci: (0, 0, 0))
    vmem = 10 * c * half * 4 + 3 * h * dk * dk * 4 + sum(2 * a.size * 4 for a in tables)
    return pl.pallas_call(
        functools.partial(_ret_prompt_kernel, n_chunks=nc, scale=dk ** -0.5),
        grid=(b, nc),
        in_specs=[
            col(0), col(1), col(2), col(3),
            pl.BlockSpec((c, dk // 2), lambda bi, ci: (ci, 0)),
            pl.BlockSpec((c, dk // 2), lambda bi, ci: (ci, 0)),
            *[const(a) for a in tables],
        ],
        out_specs=[
            pl.BlockSpec((c, half), lambda bi, ci: (bi * nc + ci, 0)),
            pl.BlockSpec((1, h, dk, dk), lambda bi, ci: (bi, 0, 0, 0)),
        ],
        out_shape=[jax.ShapeDtypeStruct((b * t, half), BF16),
                   jax.ShapeDtypeStruct((b, h, dk, dk), F32)],
        scratch_shapes=[pltpu.VMEM((h, dk, dk), F32)],
        compiler_params=_params(("parallel", "arbitrary"), vmem),
        name="ret_prompt",
    )(proj, proj, proj, proj, cos, sin, *tables)


def _ret_sample_kernel(*refs, bb, scale, chained):
    (q_ref, k_ref, v_ref, g_ref, cos_ref, sin_ref, mask_ref, qd_ref, kd_ref, cd_ref,
     s_in_ref) = refs[:11]
    o_ref, s_out_ref, kdt_ref, acc_ref = refs[12 if chained else 11:]
    dt, _, dk = q_ref.shape
    rows = dt * bb
    cos, sin = cos_ref[...], sin_ref[...]
    q = _rotate(q_ref[...].reshape(rows, dk), cos, sin)
    k = _rotate(k_ref[...].reshape(rows, dk), cos, sin) * scale
    qb, kb = q.astype(BF16), k.astype(BF16)
    vb = v_ref[...].reshape(rows, dk).astype(BF16)
    scores = lax.dot_general(qb, kb, (((1,), (1,)), ((), ())),
                             preferred_element_type=F32) * mask_ref[0]
    intra = _bdot(scores.astype(BF16), vb)
    kdt_ref[...] = (k * kd_ref[0]).T
    cd = cd_ref[0]
    row_batch = lax.broadcasted_iota(jnp.int32, (rows, 1), 0) % bb
    col_batch = lax.broadcasted_iota(jnp.int32, (1, rows), 1) % bb

    acc_ref[...] = jnp.zeros_like(acc_ref)

    def body(j, carry):
        s = s_in_ref[0, j, 0]
        acc_ref[...] = jnp.where(row_batch == j, _bdot(qb, s.astype(BF16)), acc_ref[...])
        kdt_j = jnp.where(col_batch == j, kdt_ref[...], 0.0).astype(BF16)
        s_out_ref[0, j, 0] = s * cd + _bdot(kdt_j, vb)
        return carry

    lax.fori_loop(0, bb, body, 0, unroll=4)
    o = intra + acc_ref[...] * qd_ref[0]
    o = o * lax.rsqrt(jnp.mean(o * o, axis=-1, keepdims=True) + EPS)
    gr = g_ref[...].reshape(rows, dk)
    o_ref[...] = (o * (gr * jax.nn.sigmoid(gr))).astype(o_ref.dtype).reshape(dt, bb, dk)


def _ret_sample(proj, state_ret, e, prev_states, db, dt, half, past_len):
    dk = half // RET_HEADS
    h = RET_HEADS
    assert RET_CHUNK % dt == 0 and RET_SAMPLE_ROWS % dt == 0
    rows = RET_SAMPLE_ROWS
    bb = rows // dt
    assert db % bb == 0
    proj3 = proj.reshape(dt, db, proj.shape[1])
    cos, sin = _rope_tables(past_len + np.arange(dt), dk // 2)
    cos, sin = np.repeat(cos, bb, axis=0), np.repeat(sin, bb, axis=0)
    mask, q_dec, k_dec, chunk_dec = _decay_tables(dt, dk)
    mask = np.einsum("hnm,ab->hnamb", mask, np.eye(bb, dtype=np.float32)).reshape(h, rows, rows)
    q_dec, k_dec = np.repeat(q_dec, bb, axis=1), np.repeat(k_dec, bb, axis=1)

    def col(off):
        return pl.BlockSpec((dt, bb, dk), lambda bi, hi: (0, bi, off + hi))

    def per_head(shape):
        return pl.BlockSpec((1,) + shape, lambda bi, hi: (hi, 0, 0))

    state_block = pl.BlockSpec((1, bb, 1, dk, dk), lambda bi, hi: (e, bi, hi, 0, 0))
    chained = prev_states is not None
    operands = [proj3, proj3, proj3, proj3, cos, sin, mask, q_dec, k_dec, chunk_dec, state_ret]
    in_specs = [
        col(0), col(h), col(2 * h), col(3 * h),
        pl.BlockSpec((rows, dk // 2), lambda bi, hi: (0, 0)),
        pl.BlockSpec((rows, dk // 2), lambda bi, hi: (0, 0)),
        per_head((rows, rows)), per_head((rows, dk)), per_head((rows, dk)),
        per_head((1, dk)), state_block,
    ]
    if chained:
        operands.append(prev_states)
        in_specs.append(pl.BlockSpec(memory_space=pl.ANY))
    vmem = 4 * bb * dk * dk * 4 + 12 * rows * dk * 4 + 4 * dk * dk * 4
    o, states = pl.pallas_call(
        functools.partial(_ret_sample_kernel, bb=bb, scale=dk ** -0.5, chained=chained),
        grid=(db // bb, h),
        in_specs=in_specs,
        out_specs=[
            pl.BlockSpec((dt, bb, dk), lambda bi, hi: (0, bi, hi)),
            state_block,
        ],
        out_shape=[jax.ShapeDtypeStruct((dt, db, half), BF16),
                   jax.ShapeDtypeStruct(state_ret.shape, state_ret.dtype)],
        scratch_shapes=[pltpu.VMEM((dk, rows), F32), pltpu.VMEM((rows, dk), F32)],
        input_output_aliases={len(operands) - 1: 1} if chained else {},
        compiler_params=_params(("parallel", "parallel"), vmem),
        name="ret_sample",
    )(*operands)
    return o.reshape(dt * db, half), states


def _lru_gates(xc, wa_ref, ba_ref, wx_ref, bx_ref, sp_ref):
    xcb = xc.astype(BF16)
    blk = wa_ref.shape[1]
    ra, ri = [], []
    for n in range(wa_ref.shape[0]):
        xn = xcb[:, n * blk:(n + 1) * blk]
        ra.append(_bdot(xn, wa_ref[n].astype(BF16)))
        ri.append(_bdot(xn, wx_ref[n].astype(BF16)))
    r = jax.nn.sigmoid(jnp.concatenate(ra, axis=-1) + ba_ref[...])
    i = jax.nn.sigmoid(jnp.concatenate(ri, axis=-1) + bx_ref[...])
    log_a = -LRU_C * r * sp_ref[...]
    a = jnp.exp(log_a)
    mult = jnp.sqrt(jnp.maximum(-jnp.tanh(log_a) * (a * a + 1.0), 0.0))
    return a, mult * i * xc


def _lru_prompt_kernel(xl_ref, yl_ref, cw_ref, cb_ref, wa_ref, ba_ref, wx_ref, bx_ref,
                       sp_ref, lo_ref, h_out_ref, conv_out_ref, xs_ref, a_ref, b_ref,
                       hc_ref, *, n_chunks):
    c = pl.program_id(1)
    tc, w = xl_ref.shape
    sub = V7X_SUBLANES

    @pl.when(c == 0)
    def _():
        xs_ref[0:sub, :] = jnp.zeros((sub, w), F32)
        hc_ref[...] = jnp.zeros_like(hc_ref)

    x = xl_ref[...]
    xs_ref[sub:sub + tc, :] = x
    xc = cb_ref[...] + x * cw_ref[CONV_W - 1:CONV_W, :]
    for i in range(CONV_W - 1):
        back = CONV_W - 1 - i
        xc = xc + xs_ref[sub - back:sub - back + tc, :] * cw_ref[i:i + 1, :]
    xs_ref[0:sub, :] = xs_ref[tc:tc + sub, :]

    a, bt = _lru_gates(xc, wa_ref, ba_ref, wx_ref, bx_ref, sp_ref)

    a3 = a.reshape(tc // sub, sub, w)
    b3 = bt.reshape(tc // sub, sub, w)
    step = lax.broadcasted_iota(jnp.int32, (1, sub, 1), 1)
    for s in (1, 2, 4):
        keep = step >= s
        a_prev = jnp.where(keep, pltpu.roll(a3, s, axis=1), 1.0)
        b_prev = jnp.where(keep, pltpu.roll(b3, s, axis=1), 0.0)
        b3 = a3 * b_prev + b3
        a3 = a3 * a_prev
    a_ref[...] = a3.reshape(tc, w)
    b_ref[...] = b3.reshape(tc, w)

    def body(g, h):
        sl = pl.ds(pl.multiple_of(g * sub, sub), sub)
        hg = b_ref[sl, :] + a_ref[sl, :] * h
        b_ref[sl, :] = hg
        return jnp.broadcast_to(hg[sub - 1:sub, :], (sub, w))

    h_last = lax.fori_loop(0, tc // sub, body, hc_ref[...])
    hc_ref[...] = h_last
    lo_ref[...] = (jax.nn.gelu(yl_ref[...]) * b_ref[...]).astype(lo_ref.dtype)

    @pl.when(c == n_chunks - 1)
    def _():
        h_out_ref[0] = h_last[0:1, :]
        conv_out_ref[0] = xs_ref[sub - (CONV_W - 1):sub, :]


def _lru_prompt(proj, b, t, w, cw, cb, wa, ba, wx, bx, sp):
    tc = _pick_tile(t, 256, 16)
    nc = t // tc
    xl_col = (proj.shape[1] - 2 * w) // w
    vec = lambda: pl.BlockSpec((1, w), lambda bi, ci: (0, 0))
    blocks = lambda: pl.BlockSpec(wa.shape, lambda bi, ci: (0, 0, 0))
    vmem = 4 * tc * w * 4 + 2 * tc * w * 2 + 3 * tc * w * 4 + 8 * tc * w * 4
    return pl.pallas_call(
        functools.partial(_lru_prompt_kernel, n_chunks=nc),
        grid=(b, nc),
        in_specs=[
            pl.BlockSpec((tc, w), lambda bi, ci: (bi * nc + ci, xl_col)),
            pl.BlockSpec((tc, w), lambda bi, ci: (bi * nc + ci, xl_col + 1)),
            pl.BlockSpec((CONV_W, w), lambda bi, ci: (0, 0)),
            vec(), blocks(), vec(), blocks(), vec(), vec(),
        ],
        out_specs=[
            pl.BlockSpec((tc, w), lambda bi, ci: (bi * nc + ci, 0)),
            pl.BlockSpec((1, 1, w), lambda bi, ci: (bi, 0, 0)),
            pl.BlockSpec((1, CONV_W - 1, w), lambda bi, ci: (bi, 0, 0)),
        ],
        out_shape=[jax.ShapeDtypeStruct((b * t, w), BF16),
                   jax.ShapeDtypeStruct((b, 1, w), F32),
                   jax.ShapeDtypeStruct((b, CONV_W - 1, w), F32)],
        scratch_shapes=[pltpu.VMEM((tc + V7X_SUBLANES, w), F32),
                        pltpu.VMEM((tc, w), F32), pltpu.VMEM((tc, w), F32),
                        pltpu.VMEM((V7X_SUBLANES, w), F32)],
        compiler_params=_params(("parallel", "arbitrary"), vmem),
        name="lru_prompt",
    )(proj, proj, cw, cb.reshape(1, w), wa, ba.reshape(1, w), wx, bx.reshape(1, w), sp)


def _lru_sample_kernel(xl_ref, yl_ref, conv0_ref, h0_ref, cw_ref, cb_ref, wa_ref, ba_ref,
                       wx_ref, bx_ref, sp_ref, lo_ref, h_out_ref, conv_out_ref, *, dt):
    db = h0_ref.shape[1]
    w = h0_ref.shape[2]
    taps = CONV_W - 1
    xp = [conv0_ref[0, :, i * w:(i + 1) * w] for i in range(taps)]
    xp += [xl_ref[t * db:(t + 1) * db, :] for t in range(dt)]
    xcs = []
    for t in range(dt):
        xc = cb_ref[...] + xp[t] * cw_ref[0:1, :]
        for i in range(1, CONV_W):
            xc = xc + xp[t + i] * cw_ref[i:i + 1, :]
        xcs.append(xc)
    a, bt = _lru_gates(jnp.concatenate(xcs, axis=0), wa_ref, ba_ref, wx_ref, bx_ref, sp_ref)
    h = h0_ref[0]
    for t in range(dt):
        rows = slice(t * db, (t + 1) * db)
        h = a[rows] * h + bt[rows]
        lo_ref[rows, :] = (jax.nn.gelu(yl_ref[rows, :]) * h).astype(lo_ref.dtype)
    h_out_ref[...] = h
    for i in range(taps):
        conv_out_ref[:, i * w:(i + 1) * w] = xp[dt + i]


def _lru_sample(proj, state_conv, state_lru, e, db, dt, w, cw, cb, wa, ba, wx, bx, sp):
    rows = db * dt
    assert proj.shape[0] == rows
    rb = 0
    xl_col = (proj.shape[1] - 2 * w) // w
    taps = CONV_W - 1
    conv0 = state_conv.reshape(state_conv.shape[0], db, taps * w)
    vec = lambda: pl.BlockSpec((1, w), lambda i: (0, 0))
    blocks = lambda: pl.BlockSpec(wa.shape, lambda i: (0, 0, 0))
    vmem = 16 * rows * w * 4
    lo, h_new, conv_new = pl.pallas_call(
        functools.partial(_lru_sample_kernel, dt=dt),
        grid=(1,),
        in_specs=[
            pl.BlockSpec((rows, w), lambda i: (rb, xl_col)),
            pl.BlockSpec((rows, w), lambda i: (rb, xl_col + 1)),
            pl.BlockSpec((1, db, taps * w), lambda i: (e, 0, 0)),
            pl.BlockSpec((1, db, w), lambda i: (e, 0, 0)),
            pl.BlockSpec((CONV_W, w), lambda i: (0, 0)),
            vec(), blocks(), vec(), blocks(), vec(), vec(),
        ],
        out_specs=[
            pl.BlockSpec((rows, w), lambda i: (0, 0)),
            pl.BlockSpec((db, w), lambda i: (0, 0)),
            pl.BlockSpec((db, taps * w), lambda i: (0, 0)),
        ],
        out_shape=[jax.ShapeDtypeStruct((rows, w), BF16),
                   jax.ShapeDtypeStruct((db, w), F32),
                   jax.ShapeDtypeStruct((db, taps * w), F32)],
        compiler_params=_params(("arbitrary",), vmem),
        name="lru_sample",
    )(proj, proj, conv0, state_lru, cw, cb.reshape(1, w), wa, ba.reshape(1, w), wx,
      bx.reshape(1, w), sp)
    return lo, h_new, conv_new.reshape(db, taps, w)


def _s5_tables(a_re, a_im, b_re, b_im, c_re, c_im, d, log_dt):
    g = a_re.shape[0]
    nb = g // SSM_BLOCK_GROUPS
    dt = jnp.exp(log_dt)[:, None]
    mag = jnp.exp(a_re * dt)
    abr = mag * jnp.cos(a_im * dt)
    abi = mag * jnp.sin(a_im * dt)
    den = a_re * a_re + a_im * a_im
    nr, ni = abr - 1.0, abi
    fr = (nr * a_re + ni * a_im) / den
    fi = (ni * a_re - nr * a_im) / den
    bbr = fr[..., None] * b_re - fi[..., None] * b_im
    bbi = fr[..., None] * b_im + fi[..., None] * b_re
    eye = jnp.eye(SSM_BLOCK_GROUPS, dtype=F32)[None, :, None, :, None]

    def pack_in(bb):
        bb = bb.reshape(nb, SSM_BLOCK_GROUPS, SSM_P, SSM_GROUP).transpose(0, 1, 3, 2)
        return (bb[:, :, :, None, :] * eye).reshape(nb, V7X_MXU_DIM, SSM_BLOCK_STATES)

    def pack_out(cc):
        cc = cc.reshape(nb, SSM_BLOCK_GROUPS, SSM_GROUP, SSM_P).transpose(0, 1, 3, 2)
        return (cc[:, :, :, None, :] * eye).reshape(nb, SSM_BLOCK_STATES, V7X_MXU_DIM)

    w_in = jnp.concatenate([pack_in(bbr), pack_in(bbi)], axis=-1).astype(BF16)
    w_out = jnp.concatenate([pack_out(c_re), pack_out(-c_im)], axis=1).astype(BF16)
    return (abr.reshape(nb, SSM_BLOCK_STATES), abi.reshape(nb, SSM_BLOCK_STATES),
            w_in, w_out, d.reshape(nb, 1, V7X_MXU_DIM))


def _s5_prompt_kernel(x_ref, g_ref, ar_ref, ai_ref, win_ref, wout_ref, d_ref, y_ref,
                      hr_out_ref, hi_out_ref, u_ref, sr_ref, si_ref, hr_ref, hi_ref, *,
                      n_chunks):
    c = pl.program_id(1)
    tc = x_ref.shape[0]
    nb = win_ref.shape[0]
    ns = SSM_BLOCK_STATES
    bw = V7X_MXU_DIM
    ln = V7X_LANES
    slots = V7X_SUBLANES
    nl = sr_ref.shape[0]
    splits = ns // (nl * ln)
    blocks_per_pass = slots // splits
    pitch = S5_ROW_PITCH

    @pl.when(c == 0)
    def _():
        hr_ref[...] = jnp.zeros_like(hr_ref)
        hi_ref[...] = jnp.zeros_like(hi_ref)

    u_ref[...] = _rms(x_ref[...], g_ref[...])

    def slot_rows(slot):
        return pl.ds(slot, tc, stride=pitch)

    for p in range(nb // blocks_per_pass):
        blocks = range(p * blocks_per_pass, (p + 1) * blocks_per_pass)
        for il, i in enumerate(blocks):
            bu = _bdot(u_ref[:, i * bw:(i + 1) * bw].astype(BF16), win_ref[i])
            for sp in range(splits):
                for l in range(nl):
                    col = (sp * nl + l) * ln
                    sr_ref[l, slot_rows(il * splits + sp), :] = bu[:, col:col + ln]
                    si_ref[l, slot_rows(il * splits + sp), :] = bu[:, ns + col:ns + col + ln]

        srows = slice(p * slots, (p + 1) * slots)
        ar = [ar_ref[srows, l * ln:(l + 1) * ln] for l in range(nl)]
        ai = [ai_ref[srows, l * ln:(l + 1) * ln] for l in range(nl)]

        def body(t, carry):
            rows = pl.ds(t * pitch, slots)
            out = []
            for l in range(nl):
                hr, hi = carry[l]
                hr_n = ar[l] * hr - ai[l] * hi + sr_ref[l, rows, :]
                hi_n = ar[l] * hi + ai[l] * hr + si_ref[l, rows, :]
                sr_ref[l, rows, :] = hr_n
                si_ref[l, rows, :] = hi_n
                out.append((hr_n, hi_n))
            return tuple(out)

        init = tuple((hr_ref[srows, l * ln:(l + 1) * ln], hi_ref[srows, l * ln:(l + 1) * ln])
                     for l in range(nl))
        last = lax.fori_loop(0, tc, body, init, unroll=8)
        for l in range(nl):
            hr_ref[srows, l * ln:(l + 1) * ln] = last[l][0]
            hi_ref[srows, l * ln:(l + 1) * ln] = last[l][1]

        for il, i in enumerate(blocks):
            parts = [ref[l, slot_rows(il * splits + sp), :].astype(BF16)
                     for ref in (sr_ref, si_ref) for sp in range(splits) for l in range(nl)]
            cols = slice(i * bw, (i + 1) * bw)
            y = _bdot(jnp.concatenate(parts, axis=-1), wout_ref[i]) + d_ref[i] * u_ref[:, cols]
            y_ref[:, cols] = jax.nn.gelu(y).astype(y_ref.dtype)

    @pl.when(c == n_chunks - 1)
    def _():
        hr_out_ref[0] = hr_ref[...]
        hi_out_ref[0] = hi_ref[...]


def _s5_prompt(x, g, tables, b, t):
    d = x.shape[1]
    abr, abi, w_in, w_out, dd = tables
    nb = w_in.shape[0]
    ns = SSM_BLOCK_STATES
    tc = _pick_tile(t, 256, 16)
    nc = t // tc
    passes = 2
    assert nb % passes == 0 and V7X_SUBLANES % (nb // passes) == 0
    splits = V7X_SUBLANES // (nb // passes)
    slot_lanes = ns // splits
    nl = slot_lanes // V7X_LANES
    abr = abr.reshape(nb * splits, slot_lanes)
    abi = abi.reshape(nb * splits, slot_lanes)
    const = lambda a: _once(a.shape, lambda bi, ci: (0,) * a.ndim)
    scan_bytes = nl * tc * S5_ROW_PITCH * V7X_LANES * 4
    vmem = (5 * tc * d * 4 + 2 * tc * d * 2 + w_in.size * 2 + w_out.size * 2
            + 2 * scan_bytes + 6 * tc * 2 * ns * 4)
    y, hr, hi = pl.pallas_call(
        functools.partial(_s5_prompt_kernel, n_chunks=nc),
        grid=(b, nc),
        in_specs=[
            pl.BlockSpec((tc, d), lambda bi, ci: (bi * nc + ci, 0)),
            pl.BlockSpec((1, d), lambda bi, ci: (0, 0)),
            const(abr), const(abi), const(w_in), const(w_out), const(dd),
        ],
        out_specs=[
            pl.BlockSpec((tc, d), lambda bi, ci: (bi * nc + ci, 0)),
            pl.BlockSpec((1,) + abr.shape, lambda bi, ci: (bi, 0, 0)),
            pl.BlockSpec((1,) + abr.shape, lambda bi, ci: (bi, 0, 0)),
        ],
        out_shape=[jax.ShapeDtypeStruct((b * t, d), BF16),
                   jax.ShapeDtypeStruct((b,) + abr.shape, F32),
                   jax.ShapeDtypeStruct((b,) + abr.shape, F32)],
        scratch_shapes=[pltpu.VMEM((tc, d), F32),
                        pltpu.VMEM((nl, tc * S5_ROW_PITCH, V7X_LANES), F32),
                        pltpu.VMEM((nl, tc * S5_ROW_PITCH, V7X_LANES), F32),
                        pltpu.VMEM(abr.shape, F32), pltpu.VMEM(abr.shape, F32)],
        compiler_params=_params(("parallel", "arbitrary"), vmem),
        name="s5_prompt",
    )(x, g.reshape(1, d), abr, abi, w_in, w_out, dd)
    return y, hr.reshape(b, nb * ns), hi.reshape(b, nb * ns)


def _s5_sample_kernel(x_ref, g_ref, ar_ref, ai_ref, win_ref, wout_ref, d_ref, h0r_ref,
                      h0i_ref, y_ref, hr_out_ref, hi_out_ref, uf_ref, ub_ref, *, dt):
    i = pl.program_id(0)
    nb = uf_ref.shape[0]
    bw = V7X_MXU_DIM
    ns = SSM_BLOCK_STATES
    db = h0r_ref.shape[1]

    @pl.when(i == 0)
    def _():
        u = _rms(x_ref[...], g_ref[...])
        for n in range(nb):
            uf_ref[n] = u[:, n * bw:(n + 1) * bw]
            ub_ref[n] = u[:, n * bw:(n + 1) * bw].astype(BF16)

    bu = _bdot(ub_ref[i], win_ref[0])
    ar, ai = ar_ref[pl.ds(i, 1), :], ai_ref[pl.ds(i, 1), :]
    hr, hi = h0r_ref[0], h0i_ref[0]
    states = []
    for t in range(dt):
        rows = slice(t * db, (t + 1) * db)
        hr, hi = (ar * hr - ai * hi + bu[rows, :ns], ar * hi + ai * hr + bu[rows, ns:])
        states.append(jnp.concatenate([hr.astype(BF16), hi.astype(BF16)], axis=-1))
    y = _bdot(jnp.concatenate(states, axis=0), wout_ref[0]) + d_ref[0] * uf_ref[i]
    y_ref[...] = jax.nn.gelu(y).astype(y_ref.dtype)
    hr_out_ref[...] = hr
    hi_out_ref[...] = hi


def _s5_sample(x, g, tables, state_re, state_im, o, db, dt):
    d = x.shape[1]
    abr, abi, w_in, w_out, dd = tables
    nb = w_in.shape[0]
    ns = SSM_BLOCK_STATES
    bw = V7X_MXU_DIM
    rows = db * dt
    assert x.shape[0] == rows
    rb = 0
    no = state_re.shape[0]
    h0r = state_re.reshape(no, db, nb * ns)
    h0i = state_im.reshape(no, db, nb * ns)
    blk = lambda a: pl.BlockSpec((1,) + a.shape[1:], lambda i: (i, 0, 0))
    vmem = (2 * rows * d * 4 + rows * d * 6 + 4 * bw * 2 * ns * 2 + 8 * db * ns * 4
            + 8 * rows * 2 * ns * 4)
    return pl.pallas_call(
        functools.partial(_s5_sample_kernel, dt=dt),
        grid=(nb,),
        in_specs=[
            _once((rows, d), lambda i: (rb, 0)),
            pl.BlockSpec((1, d), lambda i: (0, 0)),
            pl.BlockSpec(abr.shape, lambda i: (0, 0)),
            pl.BlockSpec(abi.shape, lambda i: (0, 0)),
            blk(w_in), blk(w_out), blk(dd),
            pl.BlockSpec((1, db, ns), lambda i: (o, 0, i)),
            pl.BlockSpec((1, db, ns), lambda i: (o, 0, i)),
        ],
        out_specs=[
            pl.BlockSpec((rows, bw), lambda i: (0, i)),
            pl.BlockSpec((db, ns), lambda i: (0, i)),
            pl.BlockSpec((db, ns), lambda i: (0, i)),
        ],
        out_shape=[jax.ShapeDtypeStruct((rows, d), BF16),
                   jax.ShapeDtypeStruct((db, nb * ns), F32),
                   jax.ShapeDtypeStruct((db, nb * ns), F32)],
        scratch_shapes=[pltpu.VMEM((nb, rows, bw), F32), pltpu.VMEM((nb, rows, bw), BF16)],
        compiler_params=_params(("arbitrary",), vmem),
        name="s5_sample",
    )(x, g.reshape(1, d), abr, abi, w_in, w_out, dd, h0r, h0i)


def kernel(x_prompt, x_sample, state_ret, state_lru, state_conv, state_ssm_re, state_ssm_im, norm_mix_even, w_in_even, lru_conv_w, lru_conv_b, lru_wa, lru_ba, lru_wx, lru_bx, lru_lambda, w_out_even, norm_mix_odd, ssm_a_re, ssm_a_im, ssm_b_re, ssm_b_im, ssm_c_re, ssm_c_im, ssm_d, ssm_log_dt, w_glu, b_glu, norm_ffn, w_ffn_gu, w_ffn_down, norm_final):
    b, t, d = x_prompt.shape
    db, dt, _ = x_sample.shape
    depth = norm_ffn.shape[0]
    half = d // 2
    past_len = PAST_LEN
    groups, ssm_p = ssm_a_re.shape[1:]
    assert ssm_p == SSM_P and groups * SSM_GROUP == d and groups % SSM_BLOCK_GROUPS == 0

    xp = x_prompt.reshape(b * t, d)
    xs = x_sample.transpose(1, 0, 2).reshape(dt * db, d)

    rets_p, ret_s, lrus_p, lrus_s, convs_p, convs_s = [], None, [], [], [], []
    sres_p, sres_s, sims_p, sims_s = [], [], [], []
    for layer in range(depth):
        if layer % 2 == 0:
            e = layer // 2
            sp = jax.nn.softplus(-lru_lambda[e]).reshape(1, half)
            lru_w = (lru_conv_w[e], lru_conv_b[e], lru_wa[e], lru_ba[e], lru_wx[e], lru_bx[e], sp)

            proj_p = _inproj(xp, norm_mix_even[e], w_in_even, e)
            o_p, ret_p = _ret_prompt(proj_p, b, t, half)
            lo_p, lru_p, conv_p = _lru_prompt(proj_p, b, t, half, *lru_w)
            xp = _outproj(o_p, lo_p, w_out_even, e, xp)

            proj_s = _inproj(xs, norm_mix_even[e], w_in_even, e)
            o_s, ret_s = _ret_sample(proj_s, state_ret, e, ret_s, db, dt, half, past_len)
            lo_s, lru_s, conv_s = _lru_sample(proj_s, state_conv, state_lru, e, db, dt, half,
                                              *lru_w)
            xs = _outproj(o_s, lo_s, w_out_even, e, xs)

            rets_p.append(ret_p)
            lrus_p.append(lru_p.reshape(b, half))
            lrus_s.append(lru_s)
            convs_p.append(conv_p)
            convs_s.append(conv_s)
        else:
            o = layer // 2
            tables = _s5_tables(ssm_a_re[o], ssm_a_im[o], ssm_b_re[o], ssm_b_im[o],
                                ssm_c_re[o], ssm_c_im[o], ssm_d[o], ssm_log_dt[o])
            y_p, sre_p, sim_p = _s5_prompt(xp, norm_mix_odd[o], tables, b, t)
            xp = _glu(y_p, w_glu, o, b_glu[o], xp)
            y_s, sre_s, sim_s = _s5_sample(xs, norm_mix_odd[o], tables, state_ssm_re,
                                           state_ssm_im, o, db, dt)
            xs = _glu(y_s, w_glu, o, b_glu[o], xs)
            sres_p.append(sre_p.reshape(b, groups, ssm_p))
            sims_p.append(sim_p.reshape(b, groups, ssm_p))
            sres_s.append(sre_s.reshape(db, groups, ssm_p))
            sims_s.append(sim_s.reshape(db, groups, ssm_p))
        g_final = norm_final if layer == depth - 1 else None
        xp = _ffn(xp, norm_ffn[layer], w_ffn_gu, w_ffn_down, layer, g_final)
        xs = _ffn(xs, norm_ffn[layer], w_ffn_gu, w_ffn_down, layer, g_final)

    y_prompt = xp.reshape(b, t, d)
    y_sample = xs.reshape(dt, db, d).transpose(1, 0, 2)
    return (y_prompt, y_sample, jnp.stack(rets_p), ret_s, jnp.stack(lrus_p),
            jnp.stack(lrus_s), jnp.stack(convs_p), jnp.stack(convs_s), jnp.stack(sres_p),
            jnp.stack(sres_s), jnp.stack(sims_p), jnp.stack(sims_s))
```

```python
import functools

import jax
import jax.numpy as jnp
import numpy as np
from jax import lax
from jax.experimental import pallas as pl
from jax.experimental.pallas import tpu as pltpu

F32 = jnp.float32
BF16 = jnp.bfloat16

EPS = 1e-6
PAST_LEN = 16384
ROPE_BASE = 10000.0
RET_HEADS = 4
RET_CHUNK = 128
LRU_HEADS = 8
LRU_C = 8.0
CONV_W = 4
SSM_GROUP = 16
SSM_P = 64
SSM_CHUNK = 128

V7X_SUBLANES = 8
V7X_LANES = 128
V7X_MXU_DIM = 256
V7X_VMEM_BYTES = 64 * 1024 * 1024
VMEM_LIMIT_CAP = V7X_VMEM_BYTES - 6 * 1024 * 1024

DENSE_ROW_TILE = 2048

SSM_BLOCK_GROUPS = V7X_MXU_DIM // SSM_GROUP
SSM_BLOCK_STATES = SSM_BLOCK_GROUPS * SSM_P

RET_SAMPLE_ROWS = 128

S5_ROW_PITCH = 12


def _pick_tile(n, target, mult):
    best = None
    for t in range(mult, min(n, target) + 1, mult):
        if n % t == 0:
            best = t
    assert best is not None, (n, target, mult)
    return best


def _params(semantics, vmem_bytes):
    limit = min(int(vmem_bytes * 1.2) + (6 << 20), VMEM_LIMIT_CAP)
    return pltpu.CompilerParams(dimension_semantics=semantics, vmem_limit_bytes=limit)


def _once(block_shape, index_map):
    return pl.BlockSpec(block_shape, index_map, pipeline_mode=pl.Buffered(1))


def _rms(x, g):
    return x * lax.rsqrt(jnp.mean(x * x, axis=-1, keepdims=True) + EPS) * g


def _norm_rows_into(x_ref, g_ref, xn_ref):
    tm = x_ref.shape[0]
    rows = _pick_tile(tm, 64, 16)

    def body(r, carry):
        sl = pl.ds(pl.multiple_of(r * rows, rows), rows)
        xn_ref[sl, :] = _rms(x_ref[sl, :], g_ref[...]).astype(xn_ref.dtype)
        return carry

    lax.fori_loop(0, tm // rows, body, 0)


def _bdot(a, b):
    return jnp.dot(a, b, preferred_element_type=F32)


def _inproj_kernel(x_ref, g_ref, w_ref, o_ref, xn_ref):
    @pl.when(pl.program_id(1) == 0)
    def _():
        _norm_rows_into(x_ref, g_ref, xn_ref)

    o_ref[...] = _bdot(xn_ref[...], w_ref[...].astype(BF16))


def _inproj(x, g, w, e):
    m, d = x.shape
    n = w.shape[2]
    tm = _pick_tile(m, DENSE_ROW_TILE, 16)
    tn = _pick_tile(n, 512, V7X_LANES)
    vmem = tm * d * 4 + tm * d * 2 + 2 * d * tn * 4 + 2 * tm * tn * 4 + d * tn * 2
    return pl.pallas_call(
        _inproj_kernel,
        grid=(m // tm, n // tn),
        in_specs=[
            _once((tm, d), lambda i, j: (i, 0)),
            _once((1, d), lambda i, j: (0, 0)),
            pl.BlockSpec((None, d, tn), lambda i, j: (e, 0, j)),
        ],
        out_specs=pl.BlockSpec((tm, tn), lambda i, j: (i, j)),
        out_shape=jax.ShapeDtypeStruct((m, n), F32),
        scratch_shapes=[pltpu.VMEM((tm, d), BF16)],
        compiler_params=_params(("parallel", "arbitrary"), vmem),
        name="inproj",
    )(x, g.reshape(1, d), w)


def _outproj_kernel(o_ref, lo_ref, wo_ref, wl_ref, x_ref, out_ref):
    acc = _bdot(o_ref[...], wo_ref[...].astype(BF16))
    acc = acc + _bdot(lo_ref[...], wl_ref[...].astype(BF16))
    out_ref[...] = x_ref[...] + acc


def _outproj(o, lo, w, e, x):
    m, d = x.shape
    half = o.shape[1]
    tm = _pick_tile(m, DENSE_ROW_TILE, 16)
    tn = _pick_tile(d, 512, V7X_LANES)
    vmem = 2 * tm * half * 2 + 4 * half * tn * 4 + 4 * tm * tn * 4 + 2 * half * tn * 2
    return pl.pallas_call(
        _outproj_kernel,
        grid=(m // tm, d // tn),
        in_specs=[
            _once((tm, half), lambda i, j: (i, 0)),
            _once((tm, half), lambda i, j: (i, 0)),
            pl.BlockSpec((None, half, tn), lambda i, j: (e, 0, j)),
            pl.BlockSpec((None, half, tn), lambda i, j: (e, 1, j)),
            pl.BlockSpec((tm, tn), lambda i, j: (i, j)),
        ],
        out_specs=pl.BlockSpec((tm, tn), lambda i, j: (i, j)),
        out_shape=jax.ShapeDtypeStruct((m, d), F32),
        compiler_params=_params(("parallel", "arbitrary"), vmem),
        name="outproj",
    )(o, lo, w, w, x)


def _glu_kernel(y_ref, w1_ref, w2_ref, b1_ref, b2_ref, x_ref, out_ref):
    y = y_ref[...]
    z1 = _bdot(y, w1_ref[...].astype(BF16)) + b1_ref[...]
    z2 = _bdot(y, w2_ref[...].astype(BF16)) + b2_ref[...]
    out_ref[...] = x_ref[...] + z1 * jax.nn.sigmoid(z2)


def _glu(y, w, o, b, x):
    m, d = x.shape
    tm = _pick_tile(m, DENSE_ROW_TILE, 16)
    tn = _pick_tile(d, 256, V7X_LANES)
    nj = d // tn
    vmem = tm * d * 2 + 4 * d * tn * 4 + 4 * tm * tn * 4 + 2 * d * tn * 2 + 2 * tm * tn * 4
    return pl.pallas_call(
        _glu_kernel,
        grid=(m // tm, nj),
        in_specs=[
            _once((tm, d), lambda i, j: (i, 0)),
            pl.BlockSpec((None, d, tn), lambda i, j: (o, 0, j)),
            pl.BlockSpec((None, d, tn), lambda i, j: (o, 0, nj + j)),
            pl.BlockSpec((1, tn), lambda i, j: (0, j)),
            pl.BlockSpec((1, tn), lambda i, j: (0, nj + j)),
            pl.BlockSpec((tm, tn), lambda i, j: (i, j)),
        ],
        out_specs=pl.BlockSpec((tm, tn), lambda i, j: (i, j)),
        out_shape=jax.ShapeDtypeStruct((m, d), F32),
        compiler_params=_params(("parallel", "arbitrary"), vmem),
        name="glu",
    )(y, w, w, b.reshape(1, 2 * d), b.reshape(1, 2 * d), x)


def _ffn_kernel(x_hbm, g_ref, gf_ref, wg_ref, wu_ref, wd_ref, out_ref, xn_ref, x_sem, *,
                final_norm):
    i, j = pl.program_id(0), pl.program_id(1)

    @pl.when(j == 0)
    def _():
        tm = out_ref.shape[0]
        rows = pl.ds(pl.multiple_of(i * tm, tm), tm)
        load_x = pltpu.make_async_copy(x_hbm.at[rows, :], out_ref, x_sem)
        load_x.start()
        load_x.wait()
        _norm_rows_into(out_ref, g_ref, xn_ref)

    xn = xn_ref[...]
    gate = _bdot(xn, wg_ref[...].astype(BF16))
    up = _bdot(xn, wu_ref[...].astype(BF16))
    act = (gate * jax.nn.sigmoid(gate) * up).astype(BF16)
    out_ref[...] += _bdot(act, wd_ref[...].astype(BF16))

    if final_norm:
        @pl.when(j == pl.num_programs(1) - 1)
        def _():
            _norm_rows_into(out_ref, gf_ref, out_ref)


def _ffn(x, g, w_gu, w_down, layer, g_final=None):
    m, d = x.shape
    hidden = w_down.shape[1]
    tm = _pick_tile(m, DENSE_ROW_TILE, 16)
    th = _pick_tile(hidden, 256, V7X_LANES)
    nh = hidden // th
    vmem = (tm * d * 4 + tm * d * 2 + 6 * d * th * 4 + 3 * d * th * 2
            + 3 * tm * th * 4)
    gf = g if g_final is None else g_final
    return pl.pallas_call(
        functools.partial(_ffn_kernel, final_norm=g_final is not None),
        grid=(m // tm, nh),
        in_specs=[
            pl.BlockSpec(memory_space=pl.ANY),
            _once((1, d), lambda i, j: (0, 0)),
            _once((1, d), lambda i, j: (0, 0)),
            pl.BlockSpec((None, d, th), lambda i, j: (layer, 0, j)),
            pl.BlockSpec((None, d, th), lambda i, j: (layer, 0, nh + j)),
            pl.BlockSpec((None, th, d), lambda i, j: (layer, j, 0)),
        ],
        out_specs=_once((tm, d), lambda i, j: (i, 0)),
        out_shape=jax.ShapeDtypeStruct((m, d), F32),
        scratch_shapes=[pltpu.VMEM((tm, d), BF16), pltpu.SemaphoreType.DMA(())],
        compiler_params=_params(("parallel", "arbitrary"), vmem),
        name="ffn",
    )(x, g.reshape(1, d), gf.reshape(1, d), w_gu, w_gu, w_down)


def _rope_tables(pos, half):
    inv = 1.0 / np.power(ROPE_BASE, np.linspace(0.0, 1.0, half))
    ang = np.asarray(pos, np.float64)[:, None] * inv[None, :]
    return np.cos(ang).astype(np.float32), np.sin(ang).astype(np.float32)


def _decay_tables(c, dk):
    log_g = np.log1p(-np.exp2(-5.0 - np.arange(RET_HEADS, dtype=np.float64)))
    idx = np.arange(c, dtype=np.float64)
    diff = idx[:, None] - idx[None, :]
    mask = np.where(diff[None] >= 0,
                    np.exp(np.maximum(diff, 0.0)[None] * log_g[:, None, None]), 0.0)
    q_dec = np.exp((idx[None, :] + 1.0) * log_g[:, None])
    k_dec = np.exp((c - 1.0 - idx)[None, :] * log_g[:, None])
    chunk_dec = np.exp(c * log_g)
    q_dec = np.broadcast_to(q_dec[:, :, None], (RET_HEADS, c, dk))
    k_dec = np.broadcast_to(k_dec[:, :, None], (RET_HEADS, c, dk))
    chunk_dec = np.broadcast_to(chunk_dec[:, None, None], (RET_HEADS, 1, dk))
    f32 = lambda a: np.ascontiguousarray(a, dtype=np.float32)
    return f32(mask), f32(q_dec), f32(k_dec), f32(chunk_dec)


def _rotate(x, cos, sin):
    half = x.shape[-1] // 2
    x1, x2 = x[:, :half], x[:, half:]
    return jnp.concatenate([x1 * cos - x2 * sin, x2 * cos + x1 * sin], axis=-1)


def _ret_prompt_kernel(q_ref, k_ref, v_ref, g_ref, cos_ref, sin_ref, mask_ref, qd_ref,
                       kd_ref, cd_ref, o_ref, s_out_ref, s_ref, *, n_chunks, scale):
    c = pl.program_id(1)
    heads, dk, _ = s_ref.shape

    @pl.when(c == 0)
    def _():
        s_ref[...] = jnp.zeros_like(s_ref)

    cos, sin = cos_ref[...], sin_ref[...]
    for h in range(heads):
        cols = slice(h * dk, (h + 1) * dk)
        q = _rotate(q_ref[:, cols], cos, sin)
        k = _rotate(k_ref[:, cols], cos, sin) * scale
        qb, kb, vb = q.astype(BF16), k.astype(BF16), v_ref[:, cols].astype(BF16)
        s = s_ref[h]
        scores = lax.dot_general(qb, kb, (((1,), (1,)), ((), ())),
                                 preferred_element_type=F32) * mask_ref[h]
        o = _bdot(scores.astype(BF16), vb) + _bdot(qb, s.astype(BF16)) * qd_ref[h]
        kdb = (k * kd_ref[h]).astype(BF16)
        s_new = s * cd_ref[h] + lax.dot_general(kdb, vb, (((0,), (0,)), ((), ())),
                                                preferred_element_type=F32)
        s_ref[h] = s_new
        o = o * lax.rsqrt(jnp.mean(o * o, axis=-1, keepdims=True) + EPS)
        gr = g_ref[:, cols]
        o_ref[:, cols] = (o * (gr * jax.nn.sigmoid(gr))).astype(o_ref.dtype)

    @pl.when(c == n_chunks - 1)
    def _():
        s_out_ref[0] = s_ref[...]


def _ret_prompt(proj, b, t, half):
    h = RET_HEADS
    dk = half // h
    c = RET_CHUNK
    assert t % c == 0
    nc = t // c
    cos, sin = _rope_tables(np.arange(t), dk // 2)
    tables = _decay_tables(c, dk)

    def col(off):
        return pl.BlockSpec((c, half), lambda bi, ci: (bi * nc + ci, off))

    const = lambda a: pl.BlockSpec(a.shape, lambda bi, ci: (0, 0, 0))
    vmem = 10 * c * half * 4 + 3 * h * dk * dk * 4 + sum(2 * a.size * 4 for a in tables)
    return pl.pallas_call(
        functools.partial(_ret_prompt_kernel, n_chunks=nc, scale=dk ** -0.5),
        grid=(b, nc),
        in_specs=[
            col(0), col(1), col(2), col(3),
            pl.BlockSpec((c, dk // 2), lambda bi, ci: (ci, 0)),
            pl.BlockSpec((c, dk // 2), lambda bi, ci: (ci, 0)),
            *[const(a) for a in tables],
        ],
        out_specs=[
            pl.BlockSpec((c, half), lambda bi, ci: (bi * nc + ci, 0)),
            pl.BlockSpec((1, h, dk, dk), lambda bi, ci: (bi, 0, 0, 0)),
        ],
        out_shape=[jax.ShapeDtypeStruct((b * t, half), BF16),
                   jax.ShapeDtypeStruct((b, h, dk, dk), F32)],
        scratch_shapes=[pltpu.VMEM((h, dk, dk), F32)],
        compiler_params=_params(("parallel", "arbitrary"), vmem),
        name="ret_prompt",
    )(proj, proj, proj, proj, cos, sin, *tables)


def _ret_sample_kernel(*refs, bb, scale, chained):
    (q_ref, k_ref, v_ref, g_ref, cos_ref, sin_ref, mask_ref, qd_ref, kd_ref, cd_ref,
     s_in_ref) = refs[:11]
    o_ref, s_out_ref, kdt_ref, acc_ref = refs[12 if chained else 11:]

    @pl.when(pl.program_id(2) == 0)
    def _():
        _ret_sample_step(q_ref, k_ref, v_ref, g_ref, cos_ref, sin_ref, mask_ref, qd_ref, kd_ref,
                         cd_ref, s_in_ref, o_ref, s_out_ref, kdt_ref, acc_ref, bb, scale)

    @pl.when(pl.program_id(2) > 0)
    def _():
        s_out_ref[...] = jnp.zeros_like(s_out_ref)


def _ret_sample_step(q_ref, k_ref, v_ref, g_ref, cos_ref, sin_ref, mask_ref, qd_ref, kd_ref,
                     cd_ref, s_in_ref, o_ref, s_out_ref, kdt_ref, acc_ref, bb, scale):
    dt, _, dk = q_ref.shape
    rows = dt * bb
    cos, sin = cos_ref[...], sin_ref[...]
    q = _rotate(q_ref[...].reshape(rows, dk), cos, sin)
    k = _rotate(k_ref[...].reshape(rows, dk), cos, sin) * scale
    qb, kb = q.astype(BF16), k.astype(BF16)
    vb = v_ref[...].reshape(rows, dk).astype(BF16)
    scores = lax.dot_general(qb, kb, (((1,), (1,)), ((), ())),
                             preferred_element_type=F32) * mask_ref[0]
    intra = _bdot(scores.astype(BF16), vb)
    kdt_ref[...] = (k * kd_ref[0]).T
    cd = cd_ref[0]
    row_batch = lax.broadcasted_iota(jnp.int32, (rows, 1), 0) % bb
    col_batch = lax.broadcasted_iota(jnp.int32, (1, rows), 1) % bb

    acc_ref[...] = jnp.zeros_like(acc_ref)

    def body(j, carry):
        s = s_in_ref[0, j, 0]
        acc_ref[...] = jnp.where(row_batch == j, _bdot(qb, s.astype(BF16)), acc_ref[...])
        kdt_j = jnp.where(col_batch == j, kdt_ref[...], 0.0).astype(BF16)
        s_out_ref[0, j, 0] = s * cd + _bdot(kdt_j, vb)
        return carry

    lax.fori_loop(0, bb, body, 0, unroll=4)
    o = intra + acc_ref[...] * qd_ref[0]
    o = o * lax.rsqrt(jnp.mean(o * o, axis=-1, keepdims=True) + EPS)
    gr = g_ref[...].reshape(rows, dk)
    o_ref[...] = (o * (gr * jax.nn.sigmoid(gr))).astype(o_ref.dtype).reshape(dt, bb, dk)


def _ret_sample(proj, state_ret, e, prev_states, db, dt, half, past_len):
    dk = half // RET_HEADS
    h = RET_HEADS
    assert RET_CHUNK % dt == 0 and RET_SAMPLE_ROWS % dt == 0
    rows = RET_SAMPLE_ROWS
    bb = rows // dt
    assert db % bb == 0
    proj3 = proj.reshape(dt, db, proj.shape[1])
    cos, sin = _rope_tables(past_len + np.arange(dt), dk // 2)
    cos, sin = np.repeat(cos, bb, axis=0), np.repeat(sin, bb, axis=0)
    mask, q_dec, k_dec, chunk_dec = _decay_tables(dt, dk)
    mask = np.einsum("hnm,ab->hnamb", mask, np.eye(bb, dtype=np.float32)).reshape(h, rows, rows)
    q_dec, k_dec = np.repeat(q_dec, bb, axis=1), np.repeat(k_dec, bb, axis=1)

    def col(off):
        return pl.BlockSpec((dt, bb, dk), lambda bi, hi, fi: (0, bi, off + hi))

    def per_head(shape):
        return pl.BlockSpec((1,) + shape, lambda bi, hi, fi: (hi, 0, 0))

    chained = prev_states is not None
    n_fill = 0 if chained else state_ret.shape[0] - 1 - e
    operands = [proj3, proj3, proj3, proj3, cos, sin, mask, q_dec, k_dec, chunk_dec, state_ret]
    in_specs = [
        col(0), col(h), col(2 * h), col(3 * h),
        pl.BlockSpec((rows, dk // 2), lambda bi, hi, fi: (0, 0)),
        pl.BlockSpec((rows, dk // 2), lambda bi, hi, fi: (0, 0)),
        per_head((rows, rows)), per_head((rows, dk)), per_head((rows, dk)),
        per_head((1, dk)),
        pl.BlockSpec((1, bb, 1, dk, dk), lambda bi, hi, fi: (e, bi, hi, 0, 0)),
    ]
    if chained:
        operands.append(prev_states)
        in_specs.append(pl.BlockSpec(memory_space=pl.ANY))
    vmem = 4 * bb * dk * dk * 4 + 12 * rows * dk * 4 + 4 * dk * dk * 4
    o, states = pl.pallas_call(
        functools.partial(_ret_sample_kernel, bb=bb, scale=dk ** -0.5, chained=chained),
        grid=(db // bb, h, 1 + n_fill),
        in_specs=in_specs,
        out_specs=[
            pl.BlockSpec((dt, bb, dk), lambda bi, hi, fi: (0, bi, hi)),
            pl.BlockSpec((1, bb, 1, dk, dk), lambda bi, hi, fi: (e + fi, bi, hi, 0, 0)),
        ],
        out_shape=[jax.ShapeDtypeStruct((dt, db, half), BF16),
                   jax.ShapeDtypeStruct(state_ret.shape, state_ret.dtype)],
        scratch_shapes=[pltpu.VMEM((dk, rows), F32), pltpu.VMEM((rows, dk), F32)],
        input_output_aliases={len(operands) - 1: 1} if chained else {},
        compiler_params=_params(("parallel", "parallel", "arbitrary"), vmem),
        name="ret_sample",
    )(*operands)
    return o.reshape(dt * db, half), states


def _lru_gates(xc, wa_ref, ba_ref, wx_ref, bx_ref, sp_ref):
    xcb = xc.astype(BF16)
    blk = wa_ref.shape[1]
    ra, ri = [], []
    for n in range(wa_ref.shape[0]):
        xn = xcb[:, n * blk:(n + 1) * blk]
        ra.append(_bdot(xn, wa_ref[n].astype(BF16)))
        ri.append(_bdot(xn, wx_ref[n].astype(BF16)))
    r = jax.nn.sigmoid(jnp.concatenate(ra, axis=-1) + ba_ref[...])
    i = jax.nn.sigmoid(jnp.concatenate(ri, axis=-1) + bx_ref[...])
    log_a = -LRU_C * r * sp_ref[...]
    a = jnp.exp(log_a)
    mult = jnp.sqrt(jnp.maximum(-jnp.tanh(log_a) * (a * a + 1.0), 0.0))
    return a, mult * i * xc


def _lru_prompt_kernel(xl_ref, yl_ref, cw_ref, cb_ref, wa_ref, ba_ref, wx_ref, bx_ref,
                       sp_ref, lo_ref, h_out_ref, conv_out_ref, xs_ref, a_ref, b_ref,
                       hc_ref, *, n_chunks):
    c = pl.program_id(1)
    tc, w = xl_ref.shape
    sub = V7X_SUBLANES

    @pl.when(c == 0)
    def _():
        xs_ref[0:sub, :] = jnp.zeros((sub, w), F32)
        hc_ref[...] = jnp.zeros_like(hc_ref)

    x = xl_ref[...]
    xs_ref[sub:sub + tc, :] = x
    xc = cb_ref[...] + x * cw_ref[CONV_W - 1:CONV_W, :]
    for i in range(CONV_W - 1):
        back = CONV_W - 1 - i
        xc = xc + xs_ref[sub - back:sub - back + tc, :] * cw_ref[i:i + 1, :]
    xs_ref[0:sub, :] = xs_ref[tc:tc + sub, :]

    a, bt = _lru_gates(xc, wa_ref, ba_ref, wx_ref, bx_ref, sp_ref)

    a3 = a.reshape(tc // sub, sub, w)
    b3 = bt.reshape(tc // sub, sub, w)
    step = lax.broadcasted_iota(jnp.int32, (1, sub, 1), 1)
    for s in (1, 2, 4):
        keep = step >= s
        a_prev = jnp.where(keep, pltpu.roll(a3, s, axis=1), 1.0)
        b_prev = jnp.where(keep, pltpu.roll(b3, s, axis=1), 0.0)
        b3 = a3 * b_prev + b3
        a3 = a3 * a_prev
    a_ref[...] = a3.reshape(tc, w)
    b_ref[...] = b3.reshape(tc, w)

    def body(g, h):
        sl = pl.ds(pl.multiple_of(g * sub, sub), sub)
        hg = b_ref[sl, :] + a_ref[sl, :] * h
        b_ref[sl, :] = hg
        return jnp.broadcast_to(hg[sub - 1:sub, :], (sub, w))

    h_last = lax.fori_loop(0, tc // sub, body, hc_ref[...])
    hc_ref[...] = h_last
    lo_ref[...] = (jax.nn.gelu(yl_ref[...]) * b_ref[...]).astype(lo_ref.dtype)

    @pl.when(c == n_chunks - 1)
    def _():
        h_out_ref[0] = h_last[0:1, :]
        conv_out_ref[0] = xs_ref[sub - (CONV_W - 1):sub, :]


def _lru_prompt(proj, b, t, w, cw, cb, wa, ba, wx, bx, sp):
    tc = _pick_tile(t, 256, 16)
    nc = t // tc
    xl_col = (proj.shape[1] - 2 * w) // w
    vec = lambda: pl.BlockSpec((1, w), lambda bi, ci: (0, 0))
    blocks = lambda: pl.BlockSpec(wa.shape, lambda bi, ci: (0, 0, 0))
    vmem = 4 * tc * w * 4 + 2 * tc * w * 2 + 3 * tc * w * 4 + 8 * tc * w * 4
    return pl.pallas_call(
        functools.partial(_lru_prompt_kernel, n_chunks=nc),
        grid=(b, nc),
        in_specs=[
            pl.BlockSpec((tc, w), lambda bi, ci: (bi * nc + ci, xl_col)),
            pl.BlockSpec((tc, w), lambda bi, ci: (bi * nc + ci, xl_col + 1)),
            pl.BlockSpec((CONV_W, w), lambda bi, ci: (0, 0)),
            vec(), blocks(), vec(), blocks(), vec(), vec(),
        ],
        out_specs=[
            pl.BlockSpec((tc, w), lambda bi, ci: (bi * nc + ci, 0)),
            pl.BlockSpec((1, 1, w), lambda bi, ci: (bi, 0, 0)),
            pl.BlockSpec((1, CONV_W - 1, w), lambda bi, ci: (bi, 0, 0)),
        ],
        out_shape=[jax.ShapeDtypeStruct((b * t, w), BF16),
                   jax.ShapeDtypeStruct((b, 1, w), F32),
                   jax.ShapeDtypeStruct((b, CONV_W - 1, w), F32)],
        scratch_shapes=[pltpu.VMEM((tc + V7X_SUBLANES, w), F32),
                        pltpu.VMEM((tc, w), F32), pltpu.VMEM((tc, w), F32),
                        pltpu.VMEM((V7X_SUBLANES, w), F32)],
        compiler_params=_params(("parallel", "arbitrary"), vmem),
        name="lru_prompt",
    )(proj, proj, cw, cb.reshape(1, w), wa, ba.reshape(1, w), wx, bx.reshape(1, w), sp)


def _lru_sample_kernel(xl_ref, yl_ref, conv0_ref, h0_ref, cw_ref, cb_ref, wa_ref, ba_ref,
                       wx_ref, bx_ref, sp_ref, lo_ref, h_out_ref, conv_out_ref, *, dt):
    db = h0_ref.shape[1]
    w = h0_ref.shape[2]
    taps = CONV_W - 1
    xp = [conv0_ref[0, :, i * w:(i + 1) * w] for i in range(taps)]
    xp += [xl_ref[t * db:(t + 1) * db, :] for t in range(dt)]
    xcs = []
    for t in range(dt):
        xc = cb_ref[...] + xp[t] * cw_ref[0:1, :]
        for i in range(1, CONV_W):
            xc = xc + xp[t + i] * cw_ref[i:i + 1, :]
        xcs.append(xc)
    a, bt = _lru_gates(jnp.concatenate(xcs, axis=0), wa_ref, ba_ref, wx_ref, bx_ref, sp_ref)
    h = h0_ref[0]
    for t in range(dt):
        rows = slice(t * db, (t + 1) * db)
        h = a[rows] * h + bt[rows]
        lo_ref[rows, :] = (jax.nn.gelu(yl_ref[rows, :]) * h).astype(lo_ref.dtype)
    h_out_ref[...] = h
    for i in range(taps):
        conv_out_ref[:, i * w:(i + 1) * w] = xp[dt + i]


def _lru_sample(proj, state_conv, state_lru, e, db, dt, w, cw, cb, wa, ba, wx, bx, sp):
    rows = db * dt
    assert proj.shape[0] == rows
    rb = 0
    xl_col = (proj.shape[1] - 2 * w) // w
    taps = CONV_W - 1
    conv0 = state_conv.reshape(state_conv.shape[0], db, taps * w)
    vec = lambda: pl.BlockSpec((1, w), lambda i: (0, 0))
    blocks = lambda: pl.BlockSpec(wa.shape, lambda i: (0, 0, 0))
    vmem = 16 * rows * w * 4
    lo, h_new, conv_new = pl.pallas_call(
        functools.partial(_lru_sample_kernel, dt=dt),
        grid=(1,),
        in_specs=[
            pl.BlockSpec((rows, w), lambda i: (rb, xl_col)),
            pl.BlockSpec((rows, w), lambda i: (rb, xl_col + 1)),
            pl.BlockSpec((1, db, taps * w), lambda i: (e, 0, 0)),
            pl.BlockSpec((1, db, w), lambda i: (e, 0, 0)),
            pl.BlockSpec((CONV_W, w), lambda i: (0, 0)),
            vec(), blocks(), vec(), blocks(), vec(), vec(),
        ],
        out_specs=[
            pl.BlockSpec((rows, w), lambda i: (0, 0)),
            pl.BlockSpec((db, w), lambda i: (0, 0)),
            pl.BlockSpec((db, taps * w), lambda i: (0, 0)),
        ],
        out_shape=[jax.ShapeDtypeStruct((rows, w), BF16),
                   jax.ShapeDtypeStruct((db, w), F32),
                   jax.ShapeDtypeStruct((db, taps * w), F32)],
        compiler_params=_params(("arbitrary",), vmem),
        name="lru_sample",
    )(proj, proj, conv0, state_lru, cw, cb.reshape(1, w), wa, ba.reshape(1, w), wx,
      bx.reshape(1, w), sp)
    return lo, h_new, conv_new.reshape(db, taps, w)


def _s5_tables(a_re, a_im, b_re, b_im, c_re, c_im, d, log_dt):
    g = a_re.shape[0]
    nb = g // SSM_BLOCK_GROUPS
    dt = jnp.exp(log_dt)[:, None]
    mag = jnp.exp(a_re * dt)
    abr = mag * jnp.cos(a_im * dt)
    abi = mag * jnp.sin(a_im * dt)
    den = a_re * a_re + a_im * a_im
    nr, ni = abr - 1.0, abi
    fr = (nr * a_re + ni * a_im) / den
    fi = (ni * a_re - nr * a_im) / den
    bbr = fr[..., None] * b_re - fi[..., None] * b_im
    bbi = fr[..., None] * b_im + fi[..., None] * b_re
    eye = jnp.eye(SSM_BLOCK_GROUPS, dtype=F32)[None, :, None, :, None]

    def pack_in(bb):
        bb = bb.reshape(nb, SSM_BLOCK_GROUPS, SSM_P, SSM_GROUP).transpose(0, 1, 3, 2)
        return (bb[:, :, :, None, :] * eye).reshape(nb, V7X_MXU_DIM, SSM_BLOCK_STATES)

    def pack_out(cc):
        cc = cc.reshape(nb, SSM_BLOCK_GROUPS, SSM_GROUP, SSM_P).transpose(0, 1, 3, 2)
        return (cc[:, :, :, None, :] * eye).reshape(nb, SSM_BLOCK_STATES, V7X_MXU_DIM)

    w_in = jnp.concatenate([pack_in(bbr), pack_in(bbi)], axis=-1).astype(BF16)
    w_out = jnp.concatenate([pack_out(c_re), pack_out(-c_im)], axis=1).astype(BF16)
    return (abr.reshape(nb, SSM_BLOCK_STATES), abi.reshape(nb, SSM_BLOCK_STATES),
            w_in, w_out, d.reshape(nb, 1, V7X_MXU_DIM))


def _s5_prompt_kernel(x_ref, g_ref, ar_ref, ai_ref, win_ref, wout_ref, d_ref, y_ref,
                      hr_out_ref, hi_out_ref, u_ref, sr_ref, si_ref, hr_ref, hi_ref, *,
                      n_chunks):
    c = pl.program_id(1)
    tc = x_ref.shape[0]
    nb = win_ref.shape[0]
    ns = SSM_BLOCK_STATES
    bw = V7X_MXU_DIM
    ln = V7X_LANES
    slots = V7X_SUBLANES
    nl = sr_ref.shape[0]
    splits = ns // (nl * ln)
    blocks_per_pass = slots // splits
    pitch = S5_ROW_PITCH

    @pl.when(c == 0)
    def _():
        hr_ref[...] = jnp.zeros_like(hr_ref)
        hi_ref[...] = jnp.zeros_like(hi_ref)

    u_ref[...] = _rms(x_ref[...], g_ref[...])

    def slot_rows(slot):
        return pl.ds(slot, tc, stride=pitch)

    for p in range(nb // blocks_per_pass):
        blocks = range(p * blocks_per_pass, (p + 1) * blocks_per_pass)
        for il, i in enumerate(blocks):
            bu = _bdot(u_ref[:, i * bw:(i + 1) * bw].astype(BF16), win_ref[i])
            for sp in range(splits):
                for l in range(nl):
                    col = (sp * nl + l) * ln
                    sr_ref[l, slot_rows(il * splits + sp), :] = bu[:, col:col + ln]
                    si_ref[l, slot_rows(il * splits + sp), :] = bu[:, ns + col:ns + col + ln]

        srows = slice(p * slots, (p + 1) * slots)
        ar = [ar_ref[srows, l * ln:(l + 1) * ln] for l in range(nl)]
        ai = [ai_ref[srows, l * ln:(l + 1) * ln] for l in range(nl)]

        def body(t, carry):
            rows = pl.ds(t * pitch, slots)
            out = []
            for l in range(nl):
                hr, hi = carry[l]
                hr_n = ar[l] * hr - ai[l] * hi + sr_ref[l, rows, :]
                hi_n = ar[l] * hi + ai[l] * hr + si_ref[l, rows, :]
                sr_ref[l, rows, :] = hr_n
                si_ref[l, rows, :] = hi_n
                out.append((hr_n, hi_n))
            return tuple(out)

        init = tuple((hr_ref[srows, l * ln:(l + 1) * ln], hi_ref[srows, l * ln:(l + 1) * ln])
                     for l in range(nl))
        last = lax.fori_loop(0, tc, body, init, unroll=8)
        for l in range(nl):
            hr_ref[srows, l * ln:(l + 1) * ln] = last[l][0]
            hi_ref[srows, l * ln:(l + 1) * ln] = last[l][1]

        for il, i in enumerate(blocks):
            parts = [ref[l, slot_rows(il * splits + sp), :].astype(BF16)
                     for ref in (sr_ref, si_ref) for sp in range(splits) for l in range(nl)]
            cols = slice(i * bw, (i + 1) * bw)
            y = _bdot(jnp.concatenate(parts, axis=-1), wout_ref[i]) + d_ref[i] * u_ref[:, cols]
            y_ref[:, cols] = jax.nn.gelu(y).astype(y_ref.dtype)

    @pl.when(c == n_chunks - 1)
    def _():
        hr_out_ref[0] = hr_ref[...]
        hi_out_ref[0] = hi_ref[...]


def _s5_prompt(x, g, tables, b, t):
    d = x.shape[1]
    abr, abi, w_in, w_out, dd = tables
    nb = w_in.shape[0]
    ns = SSM_BLOCK_STATES
    tc = _pick_tile(t, 256, 16)
    nc = t // tc
    passes = 2
    assert nb % passes == 0 and V7X_SUBLANES % (nb // passes) == 0
    splits = V7X_SUBLANES // (nb // passes)
    slot_lanes = ns // splits
    nl = slot_lanes // V7X_LANES
    abr = abr.reshape(nb * splits, slot_lanes)
    abi = abi.reshape(nb * splits, slot_lanes)
    const = lambda a: _once(a.shape, lambda bi, ci: (0,) * a.ndim)
    scan_bytes = nl * tc * S5_ROW_PITCH * V7X_LANES * 4
    vmem = (5 * tc * d * 4 + 2 * tc * d * 2 + w_in.size * 2 + w_out.size * 2
            + 2 * scan_bytes + 6 * tc * 2 * ns * 4)
    y, hr, hi = pl.pallas_call(
        functools.partial(_s5_prompt_kernel, n_chunks=nc),
        grid=(b, nc),
        in_specs=[
            pl.BlockSpec((tc, d), lambda bi, ci: (bi * nc + ci, 0)),
            pl.BlockSpec((1, d), lambda bi, ci: (0, 0)),
            const(abr), const(abi), const(w_in), const(w_out), const(dd),
        ],
        out_specs=[
            pl.BlockSpec((tc, d), lambda bi, ci: (bi * nc + ci, 0)),
            pl.BlockSpec((1,) + abr.shape, lambda bi, ci: (bi, 0, 0)),
            pl.BlockSpec((1,) + abr.shape, lambda bi, ci: (bi, 0, 0)),
        ],
        out_shape=[jax.ShapeDtypeStruct((b * t, d), BF16),
                   jax.ShapeDtypeStruct((b,) + abr.shape, F32),
                   jax.ShapeDtypeStruct((b,) + abr.shape, F32)],
        scratch_shapes=[pltpu.VMEM((tc, d), F32),
                        pltpu.VMEM((nl, tc * S5_ROW_PITCH, V7X_LANES), F32),
                        pltpu.VMEM((nl, tc * S5_ROW_PITCH, V7X_LANES), F32),
                        pltpu.VMEM(abr.shape, F32), pltpu.VMEM(abr.shape, F32)],
        compiler_params=_params(("parallel", "arbitrary"), vmem),
        name="s5_prompt",
    )(x, g.reshape(1, d), abr, abi, w_in, w_out, dd)
    return y, hr.reshape(b, nb * ns), hi.reshape(b, nb * ns)


def _s5_sample_kernel(x_ref, g_ref, ar_ref, ai_ref, win_ref, wout_ref, d_ref, h0r_ref,
                      h0i_ref, y_ref, hr_out_ref, hi_out_ref, uf_ref, ub_ref, *, dt):
    i = pl.program_id(0)
    nb = uf_ref.shape[0]
    bw = V7X_MXU_DIM
    ns = SSM_BLOCK_STATES
    db = h0r_ref.shape[1]

    @pl.when(i == 0)
    def _():
        u = _rms(x_ref[...], g_ref[...])
        for n in range(nb):
            uf_ref[n] = u[:, n * bw:(n + 1) * bw]
            ub_ref[n] = u[:, n * bw:(n + 1) * bw].astype(BF16)

    bu = _bdot(ub_ref[i], win_ref[0])
    ar, ai = ar_ref[pl.ds(i, 1), :], ai_ref[pl.ds(i, 1), :]
    hr, hi = h0r_ref[0], h0i_ref[0]
    states = []
    for t in range(dt):
        rows = slice(t * db, (t + 1) * db)
        hr, hi = (ar * hr - ai * hi + bu[rows, :ns], ar * hi + ai * hr + bu[rows, ns:])
        states.append(jnp.concatenate([hr.astype(BF16), hi.astype(BF16)], axis=-1))
    y = _bdot(jnp.concatenate(states, axis=0), wout_ref[0]) + d_ref[0] * uf_ref[i]
    y_ref[...] = jax.nn.gelu(y).astype(y_ref.dtype)
    hr_out_ref[...] = hr
    hi_out_ref[...] = hi


def _s5_sample(x, g, tables, state_re, state_im, o, db, dt):
    d = x.shape[1]
    abr, abi, w_in, w_out, dd = tables
    nb = w_in.shape[0]
    ns = SSM_BLOCK_STATES
    bw = V7X_MXU_DIM
    rows = db * dt
    assert x.shape[0] == rows
    rb = 0
    no = state_re.shape[0]
    h0r = state_re.reshape(no, db, nb * ns)
    h0i = state_im.reshape(no, db, nb * ns)
    blk = lambda a: pl.BlockSpec((1,) + a.shape[1:], lambda i: (i, 0, 0))
    vmem = (2 * rows * d * 4 + rows * d * 6 + 4 * bw * 2 * ns * 2 + 8 * db * ns * 4
            + 8 * rows * 2 * ns * 4)
    return pl.pallas_call(
        functools.partial(_s5_sample_kernel, dt=dt),
        grid=(nb,),
        in_specs=[
            _once((rows, d), lambda i: (rb, 0)),
            pl.BlockSpec((1, d), lambda i: (0, 0)),
            pl.BlockSpec(abr.shape, lambda i: (0, 0)),
            pl.BlockSpec(abi.shape, lambda i: (0, 0)),
            blk(w_in), blk(w_out), blk(dd),
            pl.BlockSpec((1, db, ns), lambda i: (o, 0, i)),
            pl.BlockSpec((1, db, ns), lambda i: (o, 0, i)),
        ],
        out_specs=[
            pl.BlockSpec((rows, bw), lambda i: (0, i)),
            pl.BlockSpec((db, ns), lambda i: (0, i)),
            pl.BlockSpec((db, ns), lambda i: (0, i)),
        ],
        out_shape=[jax.ShapeDtypeStruct((rows, d), BF16),
                   jax.ShapeDtypeStruct((db, nb * ns), F32),
                   jax.ShapeDtypeStruct((db, nb * ns), F32)],
        scratch_shapes=[pltpu.VMEM((nb, rows, bw), F32), pltpu.VMEM((nb, rows, bw), BF16)],
        compiler_params=_params(("arbitrary",), vmem),
        name="s5_sample",
    )(x, g.reshape(1, d), abr, abi, w_in, w_out, dd, h0r, h0i)


def kernel(x_prompt, x_sample, state_ret, state_lru, state_conv, state_ssm_re, state_ssm_im, norm_mix_even, w_in_even, lru_conv_w, lru_conv_b, lru_wa, lru_ba, lru_wx, lru_bx, lru_lambda, w_out_even, norm_mix_odd, ssm_a_re, ssm_a_im, ssm_b_re, ssm_b_im, ssm_c_re, ssm_c_im, ssm_d, ssm_log_dt, w_glu, b_glu, norm_ffn, w_ffn_gu, w_ffn_down, norm_final):
    b, t, d = x_prompt.shape
    db, dt, _ = x_sample.shape
    depth = norm_ffn.shape[0]
    half = d // 2
    past_len = PAST_LEN
    groups, ssm_p = ssm_a_re.shape[1:]
    assert ssm_p == SSM_P and groups * SSM_GROUP == d and groups % SSM_BLOCK_GROUPS == 0

    xp = x_prompt.reshape(b * t, d)
    xs = x_sample.transpose(1, 0, 2).reshape(dt * db, d)

    rets_p, ret_s, lrus_p, lrus_s, convs_p, convs_s = [], None, [], [], [], []
    sres_p, sres_s, sims_p, sims_s = [], [], [], []
    for layer in range(depth):
        if layer % 2 == 0:
            e = layer // 2
            sp = jax.nn.softplus(-lru_lambda[e]).reshape(1, half)
            lru_w = (lru_conv_w[e], lru_conv_b[e], lru_wa[e], lru_ba[e], lru_wx[e], lru_bx[e], sp)

            proj_p = _inproj(xp, norm_mix_even[e], w_in_even, e)
            o_p, ret_p = _ret_prompt(proj_p, b, t, half)
            lo_p, lru_p, conv_p = _lru_prompt(proj_p, b, t, half, *lru_w)
            xp = _outproj(o_p, lo_p, w_out_even, e, xp)

            proj_s = _inproj(xs, norm_mix_even[e], w_in_even, e)
            o_s, ret_s = _ret_sample(proj_s, state_ret, e, ret_s, db, dt, half, past_len)
            lo_s, lru_s, conv_s = _lru_sample(proj_s, state_conv, state_lru, e, db, dt, half,
                                              *lru_w)
            xs = _outproj(o_s, lo_s, w_out_even, e, xs)

            rets_p.append(ret_p)
            lrus_p.append(lru_p.reshape(b, half))
            lrus_s.append(lru_s)
            convs_p.append(conv_p)
            convs_s.append(conv_s)
        else:
            o = layer // 2
            tables = _s5_tables(ssm_a_re[o], ssm_a_im[o], ssm_b_re[o], ssm_b_im[o],
                                ssm_c_re[o], ssm_c_im[o], ssm_d[o], ssm_log_dt[o])
            y_p, sre_p, sim_p = _s5_prompt(xp, norm_mix_odd[o], tables, b, t)
            xp = _glu(y_p, w_glu, o, b_glu[o], xp)
            y_s, sre_s, sim_s = _s5_sample(xs, norm_mix_odd[o], tables, state_ssm_re,
                                           state_ssm_im, o, db, dt)
            xs = _glu(y_s, w_glu, o, b_glu[o], xs)
            sres_p.append(sre_p.reshape(b, groups, ssm_p))
            sims_p.append(sim_p.reshape(b, groups, ssm_p))
            sres_s.append(sre_s.reshape(db, groups, ssm_p))
            sims_s.append(sim_s.reshape(db, groups, ssm_p))
        g_final = norm_final if layer == depth - 1 else None
        xp = _ffn(xp, norm_ffn[layer], w_ffn_gu, w_ffn_down, layer, g_final)
        xs = _ffn(xs, norm_ffn[layer], w_ffn_gu, w_ffn_down, layer, g_final)

    y_prompt = xp.reshape(b, t, d)
    y_sample = xs.reshape(dt, db, d).transpose(1, 0, 2)
    return (y_prompt, y_sample, jnp.stack(rets_p), ret_s, jnp.stack(lrus_p),
            jnp.stack(lrus_s), jnp.stack(convs_p), jnp.stack(convs_s), jnp.stack(sres_p),
            jnp.stack(sres_s), jnp.stack(sims_p), jnp.stack(sims_s))
```

```python
import functools

import jax
import jax.numpy as jnp
import numpy as np
from jax import lax
from jax.experimental import pallas as pl
from jax.experimental.pallas import tpu as pltpu

F32 = jnp.float32
BF16 = jnp.bfloat16

EPS = 1e-6
PAST_LEN = 16384
ROPE_BASE = 10000.0
RET_HEADS = 4
RET_CHUNK = 128
LRU_HEADS = 8
LRU_C = 8.0
CONV_W = 4
SSM_GROUP = 16
SSM_P = 64
SSM_CHUNK = 128

V7X_SUBLANES = 8
V7X_LANES = 128
V7X_MXU_DIM = 256
V7X_VMEM_BYTES = 64 * 1024 * 1024
VMEM_LIMIT_CAP = V7X_VMEM_BYTES - 6 * 1024 * 1024

DENSE_ROW_TILE = 2048

SSM_BLOCK_GROUPS = V7X_MXU_DIM // SSM_GROUP
SSM_BLOCK_STATES = SSM_BLOCK_GROUPS * SSM_P

RET_SAMPLE_ROWS = 128

S5_ROW_PITCH = 12


def _pick_tile(n, target, mult):
    best = None
    for t in range(mult, min(n, target) + 1, mult):
        if n % t == 0:
            best = t
    assert best is not None, (n, target, mult)
    return best


def _params(semantics, vmem_bytes):
    limit = min(int(vmem_bytes * 1.2) + (6 << 20), VMEM_LIMIT_CAP)
    return pltpu.CompilerParams(dimension_semantics=semantics, vmem_limit_bytes=limit)


def _once(block_shape, index_map):
    return pl.BlockSpec(block_shape, index_map, pipeline_mode=pl.Buffered(1))


def _rms(x, g):
    return x * lax.rsqrt(jnp.mean(x * x, axis=-1, keepdims=True) + EPS) * g


def _norm_rows_into(x_ref, g_ref, xn_ref):
    tm = x_ref.shape[0]
    rows = _pick_tile(tm, 64, 16)

    def body(r, carry):
        sl = pl.ds(pl.multiple_of(r * rows, rows), rows)
        xn_ref[sl, :] = _rms(x_ref[sl, :], g_ref[...]).astype(xn_ref.dtype)
        return carry

    lax.fori_loop(0, tm // rows, body, 0)


def _bdot(a, b):
    return jnp.dot(a, b, preferred_element_type=F32)


def _bdot_nt(a, b):
    return lax.dot_general(a, b, (((1,), (1,)), ((), ())), preferred_element_type=F32)


def _inproj_kernel(x_ref, g_ref, w_ref, o_ref, xn_ref):
    @pl.when(pl.program_id(1) == 0)
    def _():
        _norm_rows_into(x_ref, g_ref, xn_ref)

    o_ref[...] = _bdot(xn_ref[...], w_ref[...].astype(BF16))


def _inproj(x, g, w, e):
    m, d = x.shape
    n = w.shape[2]
    tm = _pick_tile(m, DENSE_ROW_TILE, 16)
    tn = _pick_tile(n, 512, V7X_LANES)
    vmem = tm * d * 4 + tm * d * 2 + 2 * d * tn * 4 + 2 * tm * tn * 4 + d * tn * 2
    return pl.pallas_call(
        _inproj_kernel,
        grid=(m // tm, n // tn),
        in_specs=[
            _once((tm, d), lambda i, j: (i, 0)),
            _once((1, d), lambda i, j: (0, 0)),
            pl.BlockSpec((None, d, tn), lambda i, j: (e, 0, j)),
        ],
        out_specs=pl.BlockSpec((tm, tn), lambda i, j: (i, j)),
        out_shape=jax.ShapeDtypeStruct((m, n), F32),
        scratch_shapes=[pltpu.VMEM((tm, d), BF16)],
        compiler_params=_params(("parallel", "arbitrary"), vmem),
        name="inproj",
    )(x, g.reshape(1, d), w)


def _outproj_kernel(o_ref, lo_ref, wo_ref, wl_ref, x_ref, out_ref):
    acc = _bdot(o_ref[...], wo_ref[...].astype(BF16))
    acc = acc + _bdot(lo_ref[...], wl_ref[...].astype(BF16))
    out_ref[...] = x_ref[...] + acc


def _outproj(o, lo, w, e, x):
    m, d = x.shape
    half = o.shape[1]
    tm = _pick_tile(m, DENSE_ROW_TILE, 16)
    tn = _pick_tile(d, 512, V7X_LANES)
    vmem = 2 * tm * half * 2 + 4 * half * tn * 4 + 4 * tm * tn * 4 + 2 * half * tn * 2
    return pl.pallas_call(
        _outproj_kernel,
        grid=(m // tm, d // tn),
        in_specs=[
            _once((tm, half), lambda i, j: (i, 0)),
            _once((tm, half), lambda i, j: (i, 0)),
            pl.BlockSpec((None, half, tn), lambda i, j: (e, 0, j)),
            pl.BlockSpec((None, half, tn), lambda i, j: (e, 1, j)),
            pl.BlockSpec((tm, tn), lambda i, j: (i, j)),
        ],
        out_specs=pl.BlockSpec((tm, tn), lambda i, j: (i, j)),
        out_shape=jax.ShapeDtypeStruct((m, d), F32),
        compiler_params=_params(("parallel", "arbitrary"), vmem),
        name="outproj",
    )(o, lo, w, w, x)


def _glu_kernel(y_ref, w1_ref, w2_ref, b1_ref, b2_ref, x_ref, out_ref):
    y = y_ref[...]
    z1 = _bdot(y, w1_ref[...].astype(BF16)) + b1_ref[...]
    z2 = _bdot(y, w2_ref[...].astype(BF16)) + b2_ref[...]
    out_ref[...] = x_ref[...] + z1 * jax.nn.sigmoid(z2)


def _glu(y, w, o, b, x):
    m, d = x.shape
    tm = _pick_tile(m, DENSE_ROW_TILE, 16)
    tn = _pick_tile(d, 256, V7X_LANES)
    nj = d // tn
    vmem = tm * d * 2 + 4 * d * tn * 4 + 4 * tm * tn * 4 + 2 * d * tn * 2 + 2 * tm * tn * 4
    return pl.pallas_call(
        _glu_kernel,
        grid=(m // tm, nj),
        in_specs=[
            _once((tm, d), lambda i, j: (i, 0)),
            pl.BlockSpec((None, d, tn), lambda i, j: (o, 0, j)),
            pl.BlockSpec((None, d, tn), lambda i, j: (o, 0, nj + j)),
            pl.BlockSpec((1, tn), lambda i, j: (0, j)),
            pl.BlockSpec((1, tn), lambda i, j: (0, nj + j)),
            pl.BlockSpec((tm, tn), lambda i, j: (i, j)),
        ],
        out_specs=pl.BlockSpec((tm, tn), lambda i, j: (i, j)),
        out_shape=jax.ShapeDtypeStruct((m, d), F32),
        compiler_params=_params(("parallel", "arbitrary"), vmem),
        name="glu",
    )(y, w, w, b.reshape(1, 2 * d), b.reshape(1, 2 * d), x)


def _ffn_kernel(x_hbm, g_ref, gf_ref, wg_ref, wu_ref, wd_ref, out_ref, xn_ref, x_sem, *,
                final_norm):
    i, j = pl.program_id(0), pl.program_id(1)

    @pl.when(j == 0)
    def _():
        tm = out_ref.shape[0]
        rows = pl.ds(pl.multiple_of(i * tm, tm), tm)
        load_x = pltpu.make_async_copy(x_hbm.at[rows, :], out_ref, x_sem)
        load_x.start()
        load_x.wait()
        _norm_rows_into(out_ref, g_ref, xn_ref)

    xn = xn_ref[...]
    gate = _bdot(xn, wg_ref[...].astype(BF16))
    up = _bdot(xn, wu_ref[...].astype(BF16))
    act = (gate * jax.nn.sigmoid(gate) * up).astype(BF16)
    out_ref[...] += _bdot(act, wd_ref[...].astype(BF16))

    if final_norm:
        @pl.when(j == pl.num_programs(1) - 1)
        def _():
            _norm_rows_into(out_ref, gf_ref, out_ref)


def _ffn(x, g, w_gu, w_down, layer, g_final=None):
    m, d = x.shape
    hidden = w_down.shape[1]
    tm = _pick_tile(m, DENSE_ROW_TILE, 16)
    th = _pick_tile(hidden, 256, V7X_LANES)
    nh = hidden // th
    vmem = (tm * d * 4 + tm * d * 2 + 6 * d * th * 4 + 3 * d * th * 2
            + 3 * tm * th * 4)
    gf = g if g_final is None else g_final
    return pl.pallas_call(
        functools.partial(_ffn_kernel, final_norm=g_final is not None),
        grid=(m // tm, nh),
        in_specs=[
            pl.BlockSpec(memory_space=pl.ANY),
            _once((1, d), lambda i, j: (0, 0)),
            _once((1, d), lambda i, j: (0, 0)),
            pl.BlockSpec((None, d, th), lambda i, j: (layer, 0, j)),
            pl.BlockSpec((None, d, th), lambda i, j: (layer, 0, nh + j)),
            pl.BlockSpec((None, th, d), lambda i, j: (layer, j, 0)),
        ],
        out_specs=_once((tm, d), lambda i, j: (i, 0)),
        out_shape=jax.ShapeDtypeStruct((m, d), F32),
        scratch_shapes=[pltpu.VMEM((tm, d), BF16), pltpu.SemaphoreType.DMA(())],
        compiler_params=_params(("parallel", "arbitrary"), vmem),
        name="ffn",
    )(x, g.reshape(1, d), gf.reshape(1, d), w_gu, w_gu, w_down)


def _rope_tables(pos, half):
    inv = 1.0 / np.power(ROPE_BASE, np.linspace(0.0, 1.0, half))
    ang = np.asarray(pos, np.float64)[:, None] * inv[None, :]
    return np.cos(ang).astype(np.float32), np.sin(ang).astype(np.float32)


def _decay_tables(c, dk):
    log_g = np.log1p(-np.exp2(-5.0 - np.arange(RET_HEADS, dtype=np.float64)))
    idx = np.arange(c, dtype=np.float64)
    diff = idx[:, None] - idx[None, :]
    mask = np.where(diff[None] >= 0,
                    np.exp(np.maximum(diff, 0.0)[None] * log_g[:, None, None]), 0.0)
    q_dec = np.exp((idx[None, :] + 1.0) * log_g[:, None])
    k_dec = np.exp((c - 1.0 - idx)[None, :] * log_g[:, None])
    chunk_dec = np.exp(c * log_g)
    q_dec = np.broadcast_to(q_dec[:, :, None], (RET_HEADS, c, dk))
    k_dec = np.broadcast_to(k_dec[:, :, None], (RET_HEADS, c, dk))
    chunk_dec = np.broadcast_to(chunk_dec[:, None, None], (RET_HEADS, 1, dk))
    f32 = lambda a: np.ascontiguousarray(a, dtype=np.float32)
    return f32(mask), f32(q_dec), f32(k_dec), f32(chunk_dec)


def _rotate(x, cos, sin):
    half = x.shape[-1] // 2
    x1, x2 = x[:, :half], x[:, half:]
    return jnp.concatenate([x1 * cos - x2 * sin, x2 * cos + x1 * sin], axis=-1)


def _ret_prompt_kernel(q_ref, k_ref, v_ref, g_ref, cos_ref, sin_ref, mask_ref, qd_ref,
                       kd_ref, cd_ref, o_ref, s_out_ref, s_ref, *, n_chunks, scale):
    c = pl.program_id(1)
    heads, dk, _ = s_ref.shape

    @pl.when(c == 0)
    def _():
        s_ref[...] = jnp.zeros_like(s_ref)

    cos, sin = cos_ref[...], sin_ref[...]
    for h in range(heads):
        cols = slice(h * dk, (h + 1) * dk)
        q = _rotate(q_ref[:, cols], cos, sin)
        k = _rotate(k_ref[:, cols], cos, sin) * scale
        qb, kb, vb = q.astype(BF16), k.astype(BF16), v_ref[:, cols].astype(BF16)
        s = s_ref[h]
        scores = lax.dot_general(qb, kb, (((1,), (1,)), ((), ())),
                                 preferred_element_type=F32) * mask_ref[h]
        o = _bdot(scores.astype(BF16), vb) + _bdot(qb, s.astype(BF16)) * qd_ref[h]
        kdb = (k * kd_ref[h]).astype(BF16)
        s_new = s * cd_ref[h] + lax.dot_general(kdb, vb, (((0,), (0,)), ((), ())),
                                                preferred_element_type=F32)
        s_ref[h] = s_new
        o = o * lax.rsqrt(jnp.mean(o * o, axis=-1, keepdims=True) + EPS)
        gr = g_ref[:, cols]
        o_ref[:, cols] = (o * (gr * jax.nn.sigmoid(gr))).astype(o_ref.dtype)

    @pl.when(c == n_chunks - 1)
    def _():
        s_out_ref[0] = s_ref[...]


def _ret_prompt(proj, b, t, half):
    h = RET_HEADS
    dk = half // h
    c = RET_CHUNK
    assert t % c == 0
    nc = t // c
    cos, sin = _rope_tables(np.arange(t), dk // 2)
    tables = _decay_tables(c, dk)

    def col(off):
        return pl.BlockSpec((c, half), lambda bi, ci: (bi * nc + ci, off))

    const = lambda a: pl.BlockSpec(a.shape, lambda bi, ci: (0, 0, 0))
    vmem = 10 * c * half * 4 + 3 * h * dk * dk * 4 + sum(2 * a.size * 4 for a in tables)
    return pl.pallas_call(
        functools.partial(_ret_prompt_kernel, n_chunks=nc, scale=dk ** -0.5),
        grid=(b, nc),
        in_specs=[
            col(0), col(1), col(2), col(3),
            pl.BlockSpec((c, dk // 2), lambda bi, ci: (ci, 0)),
            pl.BlockSpec((c, dk // 2), lambda bi, ci: (ci, 0)),
            *[const(a) for a in tables],
        ],
        out_specs=[
            pl.BlockSpec((c, half), lambda bi, ci: (bi * nc + ci, 0)),
            pl.BlockSpec((1, h, dk, dk), lambda bi, ci: (bi, 0, 0, 0)),
        ],
        out_shape=[jax.ShapeDtypeStruct((b * t, half), BF16),
                   jax.ShapeDtypeStruct((b, h, dk, dk), F32)],
        scratch_shapes=[pltpu.VMEM((h, dk, dk), F32)],
        compiler_params=_params(("parallel", "arbitrary"), vmem),
        name="ret_prompt",
    )(proj, proj, proj, proj, cos, sin, *tables)


def _ret_sample_kernel(*refs, bb, scale, chained):
    (q_ref, k_ref, v_ref, g_ref, cos_ref, sin_ref, mask_ref, qd_ref, kd_ref, cd_ref,
     s_in_ref) = refs[:11]
    o_ref, s_out_ref, kdt_ref, acc_ref = refs[12 if chained else 11:]

    for later in range(1, s_out_ref.shape[0]):
        s_out_ref[later] = jnp.zeros(s_out_ref.shape[1:], s_out_ref.dtype)
    dt, _, dk = q_ref.shape
    rows = dt * bb
    cos, sin = cos_ref[...], sin_ref[...]
    q = _rotate(q_ref[...].reshape(rows, dk), cos, sin)
    k = _rotate(k_ref[...].reshape(rows, dk), cos, sin) * scale
    qb, kb = q.astype(BF16), k.astype(BF16)
    vb = v_ref[...].reshape(rows, dk).astype(BF16)
    scores = lax.dot_general(qb, kb, (((1,), (1,)), ((), ())),
                             preferred_element_type=F32) * mask_ref[0]
    intra = _bdot(scores.astype(BF16), vb)
    kdt_ref[...] = (k * kd_ref[0]).T
    cd = cd_ref[0]
    row_batch = lax.broadcasted_iota(jnp.int32, (rows, 1), 0) % bb
    col_batch = lax.broadcasted_iota(jnp.int32, (1, rows), 1) % bb

    acc_ref[...] = jnp.zeros_like(acc_ref)

    def body(j, carry):
        s = s_in_ref[0, j, 0]
        acc_ref[...] = jnp.where(row_batch == j, _bdot(qb, s.astype(BF16)), acc_ref[...])
        kdt_j = jnp.where(col_batch == j, kdt_ref[...], 0.0).astype(BF16)
        s_out_ref[0, j, 0] = s * cd + _bdot(kdt_j, vb)
        return carry

    lax.fori_loop(0, bb, body, 0, unroll=4)
    o = intra + acc_ref[...] * qd_ref[0]
    o = o * lax.rsqrt(jnp.mean(o * o, axis=-1, keepdims=True) + EPS)
    gr = g_ref[...].reshape(rows, dk)
    o_ref[...] = (o * (gr * jax.nn.sigmoid(gr))).astype(o_ref.dtype).reshape(dt, bb, dk)


def _ret_sample(proj, state_ret, e, prev_states, db, dt, half, past_len):
    dk = half // RET_HEADS
    h = RET_HEADS
    assert RET_CHUNK % dt == 0 and RET_SAMPLE_ROWS % dt == 0
    rows = RET_SAMPLE_ROWS
    bb = rows // dt
    assert db % bb == 0
    proj3 = proj.reshape(dt, db, proj.shape[1])
    cos, sin = _rope_tables(past_len + np.arange(dt), dk // 2)
    cos, sin = np.repeat(cos, bb, axis=0), np.repeat(sin, bb, axis=0)
    mask, q_dec, k_dec, chunk_dec = _decay_tables(dt, dk)
    mask = np.einsum("hnm,ab->hnamb", mask, np.eye(bb, dtype=np.float32)).reshape(h, rows, rows)
    q_dec, k_dec = np.repeat(q_dec, bb, axis=1), np.repeat(k_dec, bb, axis=1)

    def col(off):
        return pl.BlockSpec((dt, bb, dk), lambda bi, hi: (0, bi, off + hi))

    def per_head(shape):
        return pl.BlockSpec((1,) + shape, lambda bi, hi: (hi, 0, 0))

    chained = prev_states is not None
    if chained:
        out_state_block = pl.BlockSpec((1, bb, 1, dk, dk), lambda bi, hi: (e, bi, hi, 0, 0))
    else:
        assert e == 0
        n_layers = state_ret.shape[0]
        out_state_block = pl.BlockSpec((n_layers, bb, 1, dk, dk),
                                       lambda bi, hi: (0, bi, hi, 0, 0))
    operands = [proj3, proj3, proj3, proj3, cos, sin, mask, q_dec, k_dec, chunk_dec, state_ret]
    in_specs = [
        col(0), col(h), col(2 * h), col(3 * h),
        pl.BlockSpec((rows, dk // 2), lambda bi, hi: (0, 0)),
        pl.BlockSpec((rows, dk // 2), lambda bi, hi: (0, 0)),
        per_head((rows, rows)), per_head((rows, dk)), per_head((rows, dk)),
        per_head((1, dk)),
        pl.BlockSpec((1, bb, 1, dk, dk), lambda bi, hi: (e, bi, hi, 0, 0)),
    ]
    if chained:
        operands.append(prev_states)
        in_specs.append(pl.BlockSpec(memory_space=pl.ANY))
    out_layers = out_state_block.block_shape[0]
    vmem = (2 + 2 * out_layers) * bb * dk * dk * 4 + 12 * rows * dk * 4 + 4 * dk * dk * 4
    o, states = pl.pallas_call(
        functools.partial(_ret_sample_kernel, bb=bb, scale=dk ** -0.5, chained=chained),
        grid=(db // bb, h),
        in_specs=in_specs,
        out_specs=[
            pl.BlockSpec((dt, bb, dk), lambda bi, hi: (0, bi, hi)),
            out_state_block,
        ],
        out_shape=[jax.ShapeDtypeStruct((dt, db, half), BF16),
                   jax.ShapeDtypeStruct(state_ret.shape, state_ret.dtype)],
        scratch_shapes=[pltpu.VMEM((dk, rows), F32), pltpu.VMEM((rows, dk), F32)],
        input_output_aliases={len(operands) - 1: 1} if chained else {},
        compiler_params=_params(("parallel", "parallel"), vmem),
        name="ret_sample",
    )(*operands)
    return o.reshape(dt * db, half), states


def _lru_gates(xc, wa_ref, ba_ref, wx_ref, bx_ref, sp_ref):
    xcb = xc.astype(BF16)
    blk = wa_ref.shape[1]
    ra, ri = [], []
    for n in range(wa_ref.shape[0]):
        xn = xcb[:, n * blk:(n + 1) * blk]
        ra.append(_bdot(xn, wa_ref[n].astype(BF16)))
        ri.append(_bdot(xn, wx_ref[n].astype(BF16)))
    r = jax.nn.sigmoid(jnp.concatenate(ra, axis=-1) + ba_ref[...])
    i = jax.nn.sigmoid(jnp.concatenate(ri, axis=-1) + bx_ref[...])
    log_a = -LRU_C * r * sp_ref[...]
    a = jnp.exp(log_a)
    mult = jnp.sqrt(jnp.maximum(-jnp.tanh(log_a) * (a * a + 1.0), 0.0))
    return a, mult * i * xc


def _lru_prompt_kernel(xl_ref, yl_ref, cw_ref, cb_ref, wa_ref, ba_ref, wx_ref, bx_ref,
                       sp_ref, lo_ref, h_out_ref, conv_out_ref, xs_ref, a_ref, b_ref,
                       hc_ref, *, n_chunks):
    c = pl.program_id(1)
    tc, w = xl_ref.shape
    sub = V7X_SUBLANES

    @pl.when(c == 0)
    def _():
        xs_ref[0:sub, :] = jnp.zeros((sub, w), F32)
        hc_ref[...] = jnp.zeros_like(hc_ref)

    x = xl_ref[...]
    xs_ref[sub:sub + tc, :] = x
    xc = cb_ref[...] + x * cw_ref[CONV_W - 1:CONV_W, :]
    for i in range(CONV_W - 1):
        back = CONV_W - 1 - i
        xc = xc + xs_ref[sub - back:sub - back + tc, :] * cw_ref[i:i + 1, :]
    xs_ref[0:sub, :] = xs_ref[tc:tc + sub, :]

    a, bt = _lru_gates(xc, wa_ref, ba_ref, wx_ref, bx_ref, sp_ref)

    a3 = a.reshape(tc // sub, sub, w)
    b3 = bt.reshape(tc // sub, sub, w)
    step = lax.broadcasted_iota(jnp.int32, (1, sub, 1), 1)
    for s in (1, 2, 4):
        keep = step >= s
        a_prev = jnp.where(keep, pltpu.roll(a3, s, axis=1), 1.0)
        b_prev = jnp.where(keep, pltpu.roll(b3, s, axis=1), 0.0)
        b3 = a3 * b_prev + b3
        a3 = a3 * a_prev
    a_ref[...] = a3.reshape(tc, w)
    b_ref[...] = b3.reshape(tc, w)

    def body(g, h):
        sl = pl.ds(pl.multiple_of(g * sub, sub), sub)
        hg = b_ref[sl, :] + a_ref[sl, :] * h
        b_ref[sl, :] = hg
        return jnp.broadcast_to(hg[sub - 1:sub, :], (sub, w))

    h_last = lax.fori_loop(0, tc // sub, body, hc_ref[...])
    hc_ref[...] = h_last
    lo_ref[...] = (jax.nn.gelu(yl_ref[...]) * b_ref[...]).astype(lo_ref.dtype)

    @pl.when(c == n_chunks - 1)
    def _():
        h_out_ref[0] = h_last[0:1, :]
        conv_out_ref[0] = xs_ref[sub - (CONV_W - 1):sub, :]


def _lru_prompt(proj, b, t, w, cw, cb, wa, ba, wx, bx, sp):
    tc = _pick_tile(t, 256, 16)
    nc = t // tc
    xl_col = (proj.shape[1] - 2 * w) // w
    vec = lambda: pl.BlockSpec((1, w), lambda bi, ci: (0, 0))
    blocks = lambda: pl.BlockSpec(wa.shape, lambda bi, ci: (0, 0, 0))
    vmem = 4 * tc * w * 4 + 2 * tc * w * 2 + 3 * tc * w * 4 + 8 * tc * w * 4
    return pl.pallas_call(
        functools.partial(_lru_prompt_kernel, n_chunks=nc),
        grid=(b, nc),
        in_specs=[
            pl.BlockSpec((tc, w), lambda bi, ci: (bi * nc + ci, xl_col)),
            pl.BlockSpec((tc, w), lambda bi, ci: (bi * nc + ci, xl_col + 1)),
            pl.BlockSpec((CONV_W, w), lambda bi, ci: (0, 0)),
            vec(), blocks(), vec(), blocks(), vec(), vec(),
        ],
        out_specs=[
            pl.BlockSpec((tc, w), lambda bi, ci: (bi * nc + ci, 0)),
            pl.BlockSpec((1, 1, w), lambda bi, ci: (bi, 0, 0)),
            pl.BlockSpec((1, CONV_W - 1, w), lambda bi, ci: (bi, 0, 0)),
        ],
        out_shape=[jax.ShapeDtypeStruct((b * t, w), BF16),
                   jax.ShapeDtypeStruct((b, 1, w), F32),
                   jax.ShapeDtypeStruct((b, CONV_W - 1, w), F32)],
        scratch_shapes=[pltpu.VMEM((tc + V7X_SUBLANES, w), F32),
                        pltpu.VMEM((tc, w), F32), pltpu.VMEM((tc, w), F32),
                        pltpu.VMEM((V7X_SUBLANES, w), F32)],
        compiler_params=_params(("parallel", "arbitrary"), vmem),
        name="lru_prompt",
    )(proj, proj, cw, cb.reshape(1, w), wa, ba.reshape(1, w), wx, bx.reshape(1, w), sp)


def _lru_sample_kernel(xl_ref, yl_ref, conv0_ref, h0_ref, cw_ref, cb_ref, wa_ref, ba_ref,
                       wx_ref, bx_ref, sp_ref, lo_ref, h_out_ref, conv_out_ref, *, dt):
    db = h0_ref.shape[1]
    w = h0_ref.shape[2]
    taps = CONV_W - 1
    xp = [conv0_ref[0, :, i * w:(i + 1) * w] for i in range(taps)]
    xp += [xl_ref[t * db:(t + 1) * db, :] for t in range(dt)]
    xcs = []
    for t in range(dt):
        xc = cb_ref[...] + xp[t] * cw_ref[0:1, :]
        for i in range(1, CONV_W):
            xc = xc + xp[t + i] * cw_ref[i:i + 1, :]
        xcs.append(xc)
    a, bt = _lru_gates(jnp.concatenate(xcs, axis=0), wa_ref, ba_ref, wx_ref, bx_ref, sp_ref)
    h = h0_ref[0]
    for t in range(dt):
        rows = slice(t * db, (t + 1) * db)
        h = a[rows] * h + bt[rows]
        lo_ref[rows, :] = (jax.nn.gelu(yl_ref[rows, :]) * h).astype(lo_ref.dtype)
    h_out_ref[...] = h
    for i in range(taps):
        conv_out_ref[:, i * w:(i + 1) * w] = xp[dt + i]


def _lru_sample(proj, state_conv, state_lru, e, db, dt, w, cw, cb, wa, ba, wx, bx, sp):
    rows = db * dt
    assert proj.shape[0] == rows
    rb = 0
    xl_col = (proj.shape[1] - 2 * w) // w
    taps = CONV_W - 1
    conv0 = state_conv.reshape(state_conv.shape[0], db, taps * w)
    vec = lambda: pl.BlockSpec((1, w), lambda i: (0, 0))
    blocks = lambda: pl.BlockSpec(wa.shape, lambda i: (0, 0, 0))
    vmem = 16 * rows * w * 4
    lo, h_new, conv_new = pl.pallas_call(
        functools.partial(_lru_sample_kernel, dt=dt),
        grid=(1,),
        in_specs=[
            pl.BlockSpec((rows, w), lambda i: (rb, xl_col)),
            pl.BlockSpec((rows, w), lambda i: (rb, xl_col + 1)),
            pl.BlockSpec((1, db, taps * w), lambda i: (e, 0, 0)),
            pl.BlockSpec((1, db, w), lambda i: (e, 0, 0)),
            pl.BlockSpec((CONV_W, w), lambda i: (0, 0)),
            vec(), blocks(), vec(), blocks(), vec(), vec(),
        ],
        out_specs=[
            pl.BlockSpec((rows, w), lambda i: (0, 0)),
            pl.BlockSpec((db, w), lambda i: (0, 0)),
            pl.BlockSpec((db, taps * w), lambda i: (0, 0)),
        ],
        out_shape=[jax.ShapeDtypeStruct((rows, w), BF16),
                   jax.ShapeDtypeStruct((db, w), F32),
                   jax.ShapeDtypeStruct((db, taps * w), F32)],
        compiler_params=_params(("arbitrary",), vmem),
        name="lru_sample",
    )(proj, proj, conv0, state_lru, cw, cb.reshape(1, w), wa, ba.reshape(1, w), wx,
      bx.reshape(1, w), sp)
    return lo, h_new, conv_new.reshape(db, taps, w)


def _s5_tables(a_re, a_im, b_re, b_im, c_re, c_im, d, log_dt):
    g = a_re.shape[0]
    nb = g // SSM_BLOCK_GROUPS
    dt = jnp.exp(log_dt)[:, None]
    mag = jnp.exp(a_re * dt)
    abr = mag * jnp.cos(a_im * dt)
    abi = mag * jnp.sin(a_im * dt)
    den = a_re * a_re + a_im * a_im
    nr, ni = abr - 1.0, abi
    fr = (nr * a_re + ni * a_im) / den
    fi = (ni * a_re - nr * a_im) / den
    bbr = fr[..., None] * b_re - fi[..., None] * b_im
    bbi = fr[..., None] * b_im + fi[..., None] * b_re
    per_lane_block = V7X_LANES // SSM_P
    owner = (jnp.arange(SSM_BLOCK_GROUPS) % per_lane_block)[None, :, None, None, None]
    slot = jnp.arange(per_lane_block)[None, None, None, :, None]

    def pack(gkp):
        gkp = gkp.reshape(nb, SSM_BLOCK_GROUPS, SSM_GROUP, 1, SSM_P)
        return jnp.where(owner == slot, gkp, 0.0).reshape(nb, V7X_MXU_DIM, V7X_LANES)

    p_in = jnp.stack([pack(bbr.transpose(0, 2, 1)), pack(bbi.transpose(0, 2, 1))])
    p_out = jnp.stack([pack(c_re), pack(-c_im)])
    return (abr.reshape(nb, SSM_BLOCK_STATES), abi.reshape(nb, SSM_BLOCK_STATES),
            p_in, p_out, d.reshape(nb, 1, V7X_MXU_DIM))


def _expand_blockdiag(packed_ref, i, w_ref):
    ln = V7X_LANES
    per_lane_block = ln // SSM_P
    w_ref[...] = jnp.zeros_like(w_ref)
    for c in range(2):
        for g in range(SSM_BLOCK_GROUPS):
            rows = slice(g * SSM_GROUP, (g + 1) * SSM_GROUP)
            col = c * SSM_BLOCK_STATES + (g // per_lane_block) * ln
            w_ref[rows, col:col + ln] = packed_ref[c, i, rows, :].astype(w_ref.dtype)


def _s5_prompt_kernel(x_ref, g_ref, ar_ref, ai_ref, pin_ref, pout_ref, d_ref, y_ref,
                      hr_out_ref, hi_out_ref, win_ref, wout_ref, u_ref, sr_ref, si_ref,
                      hr_ref, hi_ref, *, n_chunks):
    c = pl.program_id(1)
    tc = x_ref.shape[0]
    nb = win_ref.shape[0]
    ns = SSM_BLOCK_STATES
    bw = V7X_MXU_DIM
    ln = V7X_LANES
    slots = V7X_SUBLANES
    nl = sr_ref.shape[0]
    splits = ns // (nl * ln)
    blocks_per_pass = slots // splits
    pitch = S5_ROW_PITCH

    @pl.when(c == 0)
    def _():
        hr_ref[...] = jnp.zeros_like(hr_ref)
        hi_ref[...] = jnp.zeros_like(hi_ref)
        for i in range(nb):
            _expand_blockdiag(pin_ref, i, win_ref.at[i])
            _expand_blockdiag(pout_ref, i, wout_ref.at[i])

    u_ref[...] = _rms(x_ref[...], g_ref[...])

    def slot_rows(slot):
        return pl.ds(slot, tc, stride=pitch)

    for p in range(nb // blocks_per_pass):
        blocks = range(p * blocks_per_pass, (p + 1) * blocks_per_pass)
        for il, i in enumerate(blocks):
            bu = _bdot(u_ref[:, i * bw:(i + 1) * bw].astype(BF16), win_ref[i])
            for sp in range(splits):
                for l in range(nl):
                    col = (sp * nl + l) * ln
                    sr_ref[l, slot_rows(il * splits + sp), :] = bu[:, col:col + ln]
                    si_ref[l, slot_rows(il * splits + sp), :] = bu[:, ns + col:ns + col + ln]

        srows = slice(p * slots, (p + 1) * slots)
        ar = [ar_ref[srows, l * ln:(l + 1) * ln] for l in range(nl)]
        ai = [ai_ref[srows, l * ln:(l + 1) * ln] for l in range(nl)]

        def body(t, carry):
            rows = pl.ds(t * pitch, slots)
            out = []
            for l in range(nl):
                hr, hi = carry[l]
                hr_n = ar[l] * hr - ai[l] * hi + sr_ref[l, rows, :]
                hi_n = ar[l] * hi + ai[l] * hr + si_ref[l, rows, :]
                sr_ref[l, rows, :] = hr_n
                si_ref[l, rows, :] = hi_n
                out.append((hr_n, hi_n))
            return tuple(out)

        init = tuple((hr_ref[srows, l * ln:(l + 1) * ln], hi_ref[srows, l * ln:(l + 1) * ln])
                     for l in range(nl))
        last = lax.fori_loop(0, tc, body, init, unroll=8)
        for l in range(nl):
            hr_ref[srows, l * ln:(l + 1) * ln] = last[l][0]
            hi_ref[srows, l * ln:(l + 1) * ln] = last[l][1]

        for il, i in enumerate(blocks):
            parts = [ref[l, slot_rows(il * splits + sp), :].astype(BF16)
                     for ref in (sr_ref, si_ref) for sp in range(splits) for l in range(nl)]
            cols = slice(i * bw, (i + 1) * bw)
            y = _bdot_nt(jnp.concatenate(parts, axis=-1), wout_ref[i]) + d_ref[i] * u_ref[:, cols]
            y_ref[:, cols] = jax.nn.gelu(y).astype(y_ref.dtype)

    @pl.when(c == n_chunks - 1)
    def _():
        hr_out_ref[0] = hr_ref[...]
        hi_out_ref[0] = hi_ref[...]


def _s5_prompt(x, g, tables, b, t):
    d = x.shape[1]
    abr, abi, p_in, p_out, dd = tables
    nb = abr.shape[0]
    ns = SSM_BLOCK_STATES
    w_shape = (nb, V7X_MXU_DIM, 2 * ns)
    tc = _pick_tile(t, 256, 16)
    nc = t // tc
    passes = 2
    assert nb % passes == 0 and V7X_SUBLANES % (nb // passes) == 0
    splits = V7X_SUBLANES // (nb // passes)
    slot_lanes = ns // splits
    nl = slot_lanes // V7X_LANES
    abr = abr.reshape(nb * splits, slot_lanes)
    abi = abi.reshape(nb * splits, slot_lanes)
    const = lambda a: _once(a.shape, lambda bi, ci: (0,) * a.ndim)
    scan_bytes = nl * tc * S5_ROW_PITCH * V7X_LANES * 4
    w_bytes = w_shape[0] * w_shape[1] * w_shape[2] * 2
    vmem = (5 * tc * d * 4 + 2 * tc * d * 2 + 2 * w_bytes + 2 * p_in.size * 4
            + 2 * scan_bytes + 6 * tc * 2 * ns * 4)
    y, hr, hi = pl.pallas_call(
        functools.partial(_s5_prompt_kernel, n_chunks=nc),
        grid=(b, nc),
        in_specs=[
            pl.BlockSpec((tc, d), lambda bi, ci: (bi * nc + ci, 0)),
            pl.BlockSpec((1, d), lambda bi, ci: (0, 0)),
            const(abr), const(abi), const(p_in), const(p_out), const(dd),
        ],
        out_specs=[
            pl.BlockSpec((tc, d), lambda bi, ci: (bi * nc + ci, 0)),
            pl.BlockSpec((1,) + abr.shape, lambda bi, ci: (bi, 0, 0)),
            pl.BlockSpec((1,) + abr.shape, lambda bi, ci: (bi, 0, 0)),
        ],
        out_shape=[jax.ShapeDtypeStruct((b * t, d), BF16),
                   jax.ShapeDtypeStruct((b,) + abr.shape, F32),
                   jax.ShapeDtypeStruct((b,) + abr.shape, F32)],
        scratch_shapes=[pltpu.VMEM(w_shape, BF16), pltpu.VMEM(w_shape, BF16),
                        pltpu.VMEM((tc, d), F32),
                        pltpu.VMEM((nl, tc * S5_ROW_PITCH, V7X_LANES), F32),
                        pltpu.VMEM((nl, tc * S5_ROW_PITCH, V7X_LANES), F32),
                        pltpu.VMEM(abr.shape, F32), pltpu.VMEM(abr.shape, F32)],
        compiler_params=_params(("parallel", "arbitrary"), vmem),
        name="s5_prompt",
    )(x, g.reshape(1, d), abr, abi, p_in, p_out, dd)
    return y, hr.reshape(b, nb * ns), hi.reshape(b, nb * ns)


def _s5_sample_kernel(x_ref, g_ref, ar_ref, ai_ref, pin_ref, pout_ref, d_ref, h0r_ref,
                      h0i_ref, y_ref, hr_out_ref, hi_out_ref, win_ref, wout_ref, uf_ref,
                      ub_ref, *, dt):
    i = pl.program_id(0)
    nb = uf_ref.shape[0]
    bw = V7X_MXU_DIM
    ns = SSM_BLOCK_STATES
    db = h0r_ref.shape[1]

    @pl.when(i == 0)
    def _():
        u = _rms(x_ref[...], g_ref[...])
        for n in range(nb):
            uf_ref[n] = u[:, n * bw:(n + 1) * bw]
            ub_ref[n] = u[:, n * bw:(n + 1) * bw].astype(BF16)

    _expand_blockdiag(pin_ref, 0, win_ref)
    _expand_blockdiag(pout_ref, 0, wout_ref)
    bu = _bdot(ub_ref[i], win_ref[...])
    ar, ai = ar_ref[pl.ds(i, 1), :], ai_ref[pl.ds(i, 1), :]
    hr, hi = h0r_ref[0], h0i_ref[0]
    states = []
    for t in range(dt):
        rows = slice(t * db, (t + 1) * db)
        hr, hi = (ar * hr - ai * hi + bu[rows, :ns], ar * hi + ai * hr + bu[rows, ns:])
        states.append(jnp.concatenate([hr.astype(BF16), hi.astype(BF16)], axis=-1))
    y = _bdot_nt(jnp.concatenate(states, axis=0), wout_ref[...]) + d_ref[0] * uf_ref[i]
    y_ref[...] = jax.nn.gelu(y).astype(y_ref.dtype)
    hr_out_ref[...] = hr
    hi_out_ref[...] = hi


def _s5_sample(x, g, tables, state_re, state_im, o, db, dt):
    d = x.shape[1]
    abr, abi, p_in, p_out, dd = tables
    nb = abr.shape[0]
    ns = SSM_BLOCK_STATES
    bw = V7X_MXU_DIM
    rows = db * dt
    assert x.shape[0] == rows
    rb = 0
    no = state_re.shape[0]
    h0r = state_re.reshape(no, db, nb * ns)
    h0i = state_im.reshape(no, db, nb * ns)
    blk = lambda a: pl.BlockSpec((1,) + a.shape[1:], lambda i: (i, 0, 0))
    packed = lambda a: pl.BlockSpec((2, 1) + a.shape[2:], lambda i: (0, i, 0, 0))
    vmem = (2 * rows * d * 4 + rows * d * 6 + 4 * bw * 2 * ns * 2 + 8 * db * ns * 4
            + 8 * rows * 2 * ns * 4)
    return pl.pallas_call(
        functools.partial(_s5_sample_kernel, dt=dt),
        grid=(nb,),
        in_specs=[
            _once((rows, d), lambda i: (rb, 0)),
            pl.BlockSpec((1, d), lambda i: (0, 0)),
            pl.BlockSpec(abr.shape, lambda i: (0, 0)),
            pl.BlockSpec(abi.shape, lambda i: (0, 0)),
            packed(p_in), packed(p_out), blk(dd),
            pl.BlockSpec((1, db, ns), lambda i: (o, 0, i)),
            pl.BlockSpec((1, db, ns), lambda i: (o, 0, i)),
        ],
        out_specs=[
            pl.BlockSpec((rows, bw), lambda i: (0, i)),
            pl.BlockSpec((db, ns), lambda i: (0, i)),
            pl.BlockSpec((db, ns), lambda i: (0, i)),
        ],
        out_shape=[jax.ShapeDtypeStruct((rows, d), BF16),
                   jax.ShapeDtypeStruct((db, nb * ns), F32),
                   jax.ShapeDtypeStruct((db, nb * ns), F32)],
        scratch_shapes=[pltpu.VMEM((bw, 2 * ns), BF16), pltpu.VMEM((bw, 2 * ns), BF16),
                        pltpu.VMEM((nb, rows, bw), F32), pltpu.VMEM((nb, rows, bw), BF16)],
        compiler_params=_params(("arbitrary",), vmem),
        name="s5_sample",
    )(x, g.reshape(1, d), abr, abi, p_in, p_out, dd, h0r, h0i)


def kernel(x_prompt, x_sample, state_ret, state_lru, state_conv, state_ssm_re, state_ssm_im, norm_mix_even, w_in_even, lru_conv_w, lru_conv_b, lru_wa, lru_ba, lru_wx, lru_bx, lru_lambda, w_out_even, norm_mix_odd, ssm_a_re, ssm_a_im, ssm_b_re, ssm_b_im, ssm_c_re, ssm_c_im, ssm_d, ssm_log_dt, w_glu, b_glu, norm_ffn, w_ffn_gu, w_ffn_down, norm_final):
    b, t, d = x_prompt.shape
    db, dt, _ = x_sample.shape
    depth = norm_ffn.shape[0]
    half = d // 2
    past_len = PAST_LEN
    groups, ssm_p = ssm_a_re.shape[1:]
    assert ssm_p == SSM_P and groups * SSM_GROUP == d and groups % SSM_BLOCK_GROUPS == 0

    xp = x_prompt.reshape(b * t, d)
    xs = x_sample.transpose(1, 0, 2).reshape(dt * db, d)

    rets_p, ret_s, lrus_p, lrus_s, convs_p, convs_s = [], None, [], [], [], []
    sres_p, sres_s, sims_p, sims_s = [], [], [], []
    for layer in range(depth):
        if layer % 2 == 0:
            e = layer // 2
            sp = jax.nn.softplus(-lru_lambda[e]).reshape(1, half)
            lru_w = (lru_conv_w[e], lru_conv_b[e], lru_wa[e], lru_ba[e], lru_wx[e], lru_bx[e], sp)

            proj_p = _inproj(xp, norm_mix_even[e], w_in_even, e)
            o_p, ret_p = _ret_prompt(proj_p, b, t, half)
            lo_p, lru_p, conv_p = _lru_prompt(proj_p, b, t, half, *lru_w)
            xp = _outproj(o_p, lo_p, w_out_even, e, xp)

            proj_s = _inproj(xs, norm_mix_even[e], w_in_even, e)
            o_s, ret_s = _ret_sample(proj_s, state_ret, e, ret_s, db, dt, half, past_len)
            lo_s, lru_s, conv_s = _lru_sample(proj_s, state_conv, state_lru, e, db, dt, half,
                                              *lru_w)
            xs = _outproj(o_s, lo_s, w_out_even, e, xs)

            rets_p.append(ret_p)
            lrus_p.append(lru_p.reshape(b, half))
            lrus_s.append(lru_s)
            convs_p.append(conv_p)
            convs_s.append(conv_s)
        else:
            o = layer // 2
            tables = _s5_tables(ssm_a_re[o], ssm_a_im[o], ssm_b_re[o], ssm_b_im[o],
                                ssm_c_re[o], ssm_c_im[o], ssm_d[o], ssm_log_dt[o])
            y_p, sre_p, sim_p = _s5_prompt(xp, norm_mix_odd[o], tables, b, t)
            xp = _glu(y_p, w_glu, o, b_glu[o], xp)
            y_s, sre_s, sim_s = _s5_sample(xs, norm_mix_odd[o], tables, state_ssm_re,
                                           state_ssm_im, o, db, dt)
            xs = _glu(y_s, w_glu, o, b_glu[o], xs)
            sres_p.append(sre_p.reshape(b, groups, ssm_p))
            sims_p.append(sim_p.reshape(b, groups, ssm_p))
            sres_s.append(sre_s.reshape(db, groups, ssm_p))
            sims_s.append(sim_s.reshape(db, groups, ssm_p))
        g_final = norm_final if layer == depth - 1 else None
        xp = _ffn(xp, norm_ffn[layer], w_ffn_gu, w_ffn_down, layer, g_final)
        xs = _ffn(xs, norm_ffn[layer], w_ffn_gu, w_ffn_down, layer, g_final)

    y_prompt = xp.reshape(b, t, d)
    y_sample = xs.reshape(dt, db, d).transpose(1, 0, 2)
    return (y_prompt, y_sample, jnp.stack(rets_p), ret_s, jnp.stack(lrus_p),
            jnp.stack(lrus_s), jnp.stack(convs_p), jnp.stack(convs_s), jnp.stack(sres_p),
            jnp.stack(sres_s), jnp.stack(sims_p), jnp.stack(sims_s))
```

```python
import functools

import jax
import jax.numpy as jnp
import numpy as np
from jax import lax
from jax.experimental import pallas as pl
from jax.experimental.pallas import tpu as pltpu

F32 = jnp.float32
BF16 = jnp.bfloat16

EPS = 1e-6
PAST_LEN = 16384
ROPE_BASE = 10000.0
RET_HEADS = 4
RET_CHUNK = 128
LRU_HEADS = 8
LRU_C = 8.0
CONV_W = 4
SSM_GROUP = 16
SSM_P = 64
SSM_CHUNK = 128

V7X_SUBLANES = 8
V7X_LANES = 128
V7X_MXU_DIM = 256
V7X_VMEM_BYTES = 64 * 1024 * 1024
VMEM_LIMIT_CAP = V7X_VMEM_BYTES - 6 * 1024 * 1024

DENSE_ROW_TILE = 2048

TILE_COPY_CHUNKS = 8

SSM_BLOCK_GROUPS = V7X_MXU_DIM // SSM_GROUP
SSM_BLOCK_STATES = SSM_BLOCK_GROUPS * SSM_P

RET_SAMPLE_ROWS = 128

S5_ROW_PITCH = 12


def _pick_tile(n, target, mult):
    best = None
    for t in range(mult, min(n, target) + 1, mult):
        if n % t == 0:
            best = t
    assert best is not None, (n, target, mult)
    return best


def _params(semantics, vmem_bytes):
    limit = min(int(vmem_bytes * 1.2) + (6 << 20), VMEM_LIMIT_CAP)
    return pltpu.CompilerParams(dimension_semantics=semantics, vmem_limit_bytes=limit)


def _once(block_shape, index_map):
    return pl.BlockSpec(block_shape, index_map, pipeline_mode=pl.Buffered(1))


def _rms(x, g):
    return x * lax.rsqrt(jnp.mean(x * x, axis=-1, keepdims=True) + EPS) * g


def _norm_rows_into(x_ref, g_ref, xn_ref):
    tm = x_ref.shape[0]
    rows = _pick_tile(tm, 64, 16)

    def body(r, carry):
        sl = pl.ds(pl.multiple_of(r * rows, rows), rows)
        xn_ref[sl, :] = _rms(x_ref[sl, :], g_ref[...]).astype(xn_ref.dtype)
        return carry

    lax.fori_loop(0, tm // rows, body, 0)


def _row_chunk_copies(src, dst, sems):
    n = sems.shape[0]
    rc = src.shape[0] // n
    return [pltpu.make_async_copy(src.at[pl.ds(c * rc, rc), :], dst.at[pl.ds(c * rc, rc), :],
                                  sems.at[c]) for c in range(n)]


def _load_and_norm_rows(x_hbm_rows, x_ref, g_ref, xn_ref, sems):
    copies = _row_chunk_copies(x_hbm_rows, x_ref, sems)
    rc = x_ref.shape[0] // len(copies)
    for cp in copies:
        cp.start()
    for c, cp in enumerate(copies):
        cp.wait()
        rows = pl.ds(c * rc, rc)
        _norm_rows_into(x_ref.at[rows, :], g_ref, xn_ref.at[rows, :])


def _bdot(a, b):
    return jnp.dot(a, b, preferred_element_type=F32)


def _bdot_nt(a, b):
    return lax.dot_general(a, b, (((1,), (1,)), ((), ())), preferred_element_type=F32)


def _inproj_kernel(x_hbm, g_ref, w_ref, o_ref, x_ref, xn_ref, x_sems):
    @pl.when(pl.program_id(1) == 0)
    def _():
        tm = x_ref.shape[0]
        tile = pl.ds(pl.multiple_of(pl.program_id(0) * tm, tm), tm)
        _load_and_norm_rows(x_hbm.at[tile, :], x_ref, g_ref, xn_ref, x_sems)

    o_ref[...] = _bdot(xn_ref[...], w_ref[...].astype(BF16))


def _inproj(x, g, w, e):
    m, d = x.shape
    n = w.shape[2]
    tm = _pick_tile(m, DENSE_ROW_TILE, 16)
    tn = _pick_tile(n, 512, V7X_LANES)
    vmem = tm * d * 4 + tm * d * 2 + 2 * d * tn * 4 + 2 * tm * tn * 4 + d * tn * 2
    return pl.pallas_call(
        _inproj_kernel,
        grid=(m // tm, n // tn),
        in_specs=[
            pl.BlockSpec(memory_space=pl.ANY),
            _once((1, d), lambda i, j: (0, 0)),
            pl.BlockSpec((None, d, tn), lambda i, j: (e, 0, j)),
        ],
        out_specs=pl.BlockSpec((tm, tn), lambda i, j: (i, j)),
        out_shape=jax.ShapeDtypeStruct((m, n), F32),
        scratch_shapes=[pltpu.VMEM((tm, d), F32), pltpu.VMEM((tm, d), BF16),
                        pltpu.SemaphoreType.DMA((TILE_COPY_CHUNKS,))],
        compiler_params=_params(("parallel", "arbitrary"), vmem),
        name="inproj",
    )(x, g.reshape(1, d), w)


def _outproj_kernel(o_ref, lo_ref, wo_ref, wl_ref, x_ref, out_ref):
    acc = _bdot(o_ref[...], wo_ref[...].astype(BF16))
    acc = acc + _bdot(lo_ref[...], wl_ref[...].astype(BF16))
    out_ref[...] = x_ref[...] + acc


def _outproj(o, lo, w, e, x):
    m, d = x.shape
    half = o.shape[1]
    tm = _pick_tile(m, DENSE_ROW_TILE, 16)
    tn = _pick_tile(d, 512, V7X_LANES)
    vmem = 4 * tm * half * 2 + 4 * half * tn * 4 + 4 * tm * tn * 4 + 2 * half * tn * 2
    return pl.pallas_call(
        _outproj_kernel,
        grid=(m // tm, d // tn),
        in_specs=[
            pl.BlockSpec((tm, half), lambda i, j: (i, 0)),
            pl.BlockSpec((tm, half), lambda i, j: (i, 0)),
            pl.BlockSpec((None, half, tn), lambda i, j: (e, 0, j)),
            pl.BlockSpec((None, half, tn), lambda i, j: (e, 1, j)),
            pl.BlockSpec((tm, tn), lambda i, j: (i, j)),
        ],
        out_specs=pl.BlockSpec((tm, tn), lambda i, j: (i, j)),
        out_shape=jax.ShapeDtypeStruct((m, d), F32),
        compiler_params=_params(("parallel", "arbitrary"), vmem),
        name="outproj",
    )(o, lo, w, w, x)


def _glu_kernel(y_ref, w1_ref, w2_ref, b1_ref, b2_ref, x_ref, out_ref):
    y = y_ref[...]
    z1 = _bdot(y, w1_ref[...].astype(BF16)) + b1_ref[...]
    z2 = _bdot(y, w2_ref[...].astype(BF16)) + b2_ref[...]
    out_ref[...] = x_ref[...] + z1 * jax.nn.sigmoid(z2)


def _glu(y, w, o, b, x):
    m, d = x.shape
    tm = _pick_tile(m, DENSE_ROW_TILE, 16)
    tn = _pick_tile(d, 256, V7X_LANES)
    nj = d // tn
    vmem = 2 * tm * d * 2 + 4 * d * tn * 4 + 4 * tm * tn * 4 + 2 * d * tn * 2 + 2 * tm * tn * 4
    return pl.pallas_call(
        _glu_kernel,
        grid=(m // tm, nj),
        in_specs=[
            pl.BlockSpec((tm, d), lambda i, j: (i, 0)),
            pl.BlockSpec((None, d, tn), lambda i, j: (o, 0, j)),
            pl.BlockSpec((None, d, tn), lambda i, j: (o, 0, nj + j)),
            pl.BlockSpec((1, tn), lambda i, j: (0, j)),
            pl.BlockSpec((1, tn), lambda i, j: (0, nj + j)),
            pl.BlockSpec((tm, tn), lambda i, j: (i, j)),
        ],
        out_specs=pl.BlockSpec((tm, tn), lambda i, j: (i, j)),
        out_shape=jax.ShapeDtypeStruct((m, d), F32),
        compiler_params=_params(("parallel", "arbitrary"), vmem),
        name="glu",
    )(y, w, w, b.reshape(1, 2 * d), b.reshape(1, 2 * d), x)


def _ffn_kernel(x_hbm, g_ref, gf_ref, wg_ref, wu_ref, wd_ref, out_ref, xn_ref, x_sems, *,
                final_norm):
    i, j = pl.program_id(0), pl.program_id(1)

    @pl.when(j == 0)
    def _():
        tm = out_ref.shape[0]
        tile = pl.ds(pl.multiple_of(i * tm, tm), tm)
        _load_and_norm_rows(x_hbm.at[tile, :], out_ref, g_ref, xn_ref, x_sems)

    xn = xn_ref[...]
    gate = _bdot(xn, wg_ref[...].astype(BF16))
    up = _bdot(xn, wu_ref[...].astype(BF16))
    act = (gate * jax.nn.sigmoid(gate) * up).astype(BF16)
    out_ref[...] += _bdot(act, wd_ref[...].astype(BF16))

    if final_norm:
        @pl.when(j == pl.num_programs(1) - 1)
        def _():
            _norm_rows_into(out_ref, gf_ref, out_ref)


def _ffn(x, g, w_gu, w_down, layer, g_final=None):
    m, d = x.shape
    hidden = w_down.shape[1]
    tm = _pick_tile(m, DENSE_ROW_TILE, 16)
    th = _pick_tile(hidden, 256, V7X_LANES)
    nh = hidden // th
    vmem = (tm * d * 4 + tm * d * 2 + 6 * d * th * 4 + 3 * d * th * 2
            + 3 * tm * th * 4)
    gf = g if g_final is None else g_final
    return pl.pallas_call(
        functools.partial(_ffn_kernel, final_norm=g_final is not None),
        grid=(m // tm, nh),
        in_specs=[
            pl.BlockSpec(memory_space=pl.ANY),
            _once((1, d), lambda i, j: (0, 0)),
            _once((1, d), lambda i, j: (0, 0)),
            pl.BlockSpec((None, d, th), lambda i, j: (layer, 0, j)),
            pl.BlockSpec((None, d, th), lambda i, j: (layer, 0, nh + j)),
            pl.BlockSpec((None, th, d), lambda i, j: (layer, j, 0)),
        ],
        out_specs=_once((tm, d), lambda i, j: (i, 0)),
        out_shape=jax.ShapeDtypeStruct((m, d), F32),
        scratch_shapes=[pltpu.VMEM((tm, d), BF16),
                        pltpu.SemaphoreType.DMA((TILE_COPY_CHUNKS,))],
        compiler_params=_params(("parallel", "arbitrary"), vmem),
        name="ffn",
    )(x, g.reshape(1, d), gf.reshape(1, d), w_gu, w_gu, w_down)


def _rope_tables(pos, half):
    inv = 1.0 / np.power(ROPE_BASE, np.linspace(0.0, 1.0, half))
    ang = np.asarray(pos, np.float64)[:, None] * inv[None, :]
    return np.cos(ang).astype(np.float32), np.sin(ang).astype(np.float32)


def _decay_tables(c, dk):
    log_g = np.log1p(-np.exp2(-5.0 - np.arange(RET_HEADS, dtype=np.float64)))
    idx = np.arange(c, dtype=np.float64)
    diff = idx[:, None] - idx[None, :]
    mask = np.where(diff[None] >= 0,
                    np.exp(np.maximum(diff, 0.0)[None] * log_g[:, None, None]), 0.0)
    q_dec = np.exp((idx[None, :] + 1.0) * log_g[:, None])
    k_dec = np.exp((c - 1.0 - idx)[None, :] * log_g[:, None])
    chunk_dec = np.exp(c * log_g)
    q_dec = np.broadcast_to(q_dec[:, :, None], (RET_HEADS, c, dk))
    k_dec = np.broadcast_to(k_dec[:, :, None], (RET_HEADS, c, dk))
    chunk_dec = np.broadcast_to(chunk_dec[:, None, None], (RET_HEADS, 1, dk))
    f32 = lambda a: np.ascontiguousarray(a, dtype=np.float32)
    return f32(mask), f32(q_dec), f32(k_dec), f32(chunk_dec)


def _rotate(x, cos, sin):
    half = x.shape[-1] // 2
    x1, x2 = x[:, :half], x[:, half:]
    return jnp.concatenate([x1 * cos - x2 * sin, x2 * cos + x1 * sin], axis=-1)


def _ret_prompt_kernel(q_ref, k_ref, v_ref, g_ref, cos_ref, sin_ref, mask_ref, qd_ref,
                       kd_ref, cd_ref, o_ref, s_out_ref, s_ref, *, n_chunks, scale):
    c = pl.program_id(1)
    heads, dk, _ = s_ref.shape

    @pl.when(c == 0)
    def _():
        s_ref[...] = jnp.zeros_like(s_ref)

    cos, sin = cos_ref[...], sin_ref[...]
    for h in range(heads):
        cols = slice(h * dk, (h + 1) * dk)
        q = _rotate(q_ref[:, cols], cos, sin)
        k = _rotate(k_ref[:, cols], cos, sin) * scale
        qb, kb, vb = q.astype(BF16), k.astype(BF16), v_ref[:, cols].astype(BF16)
        s = s_ref[h]
        scores = lax.dot_general(qb, kb, (((1,), (1,)), ((), ())),
                                 preferred_element_type=F32) * mask_ref[h]
        o = _bdot(scores.astype(BF16), vb) + _bdot(qb, s.astype(BF16)) * qd_ref[h]
        kdb = (k * kd_ref[h]).astype(BF16)
        s_new = s * cd_ref[h] + lax.dot_general(kdb, vb, (((0,), (0,)), ((), ())),
                                                preferred_element_type=F32)
        s_ref[h] = s_new
        o = o * lax.rsqrt(jnp.mean(o * o, axis=-1, keepdims=True) + EPS)
        gr = g_ref[:, cols]
        o_ref[:, cols] = (o * (gr * jax.nn.sigmoid(gr))).astype(o_ref.dtype)

    @pl.when(c == n_chunks - 1)
    def _():
        s_out_ref[0] = s_ref[...]


def _ret_prompt(proj, b, t, half):
    h = RET_HEADS
    dk = half // h
    c = RET_CHUNK
    assert t % c == 0
    nc = t // c
    cos, sin = _rope_tables(np.arange(t), dk // 2)
    tables = _decay_tables(c, dk)

    def col(off):
        return pl.BlockSpec((c, half), lambda bi, ci: (bi * nc + ci, off))

    const = lambda a: pl.BlockSpec(a.shape, lambda bi, ci: (0, 0, 0))
    vmem = 10 * c * half * 4 + 3 * h * dk * dk * 4 + sum(2 * a.size * 4 for a in tables)
    return pl.pallas_call(
        functools.partial(_ret_prompt_kernel, n_chunks=nc, scale=dk ** -0.5),
        grid=(b, nc),
        in_specs=[
            col(0), col(1), col(2), col(3),
            pl.BlockSpec((c, dk // 2), lambda bi, ci: (ci, 0)),
            pl.BlockSpec((c, dk // 2), lambda bi, ci: (ci, 0)),
            *[const(a) for a in tables],
        ],
        out_specs=[
            pl.BlockSpec((c, half), lambda bi, ci: (bi * nc + ci, 0)),
            pl.BlockSpec((1, h, dk, dk), lambda bi, ci: (bi, 0, 0, 0)),
        ],
        out_shape=[jax.ShapeDtypeStruct((b * t, half), BF16),
                   jax.ShapeDtypeStruct((b, h, dk, dk), F32)],
        scratch_shapes=[pltpu.VMEM((h, dk, dk), F32)],
        compiler_params=_params(("parallel", "arbitrary"), vmem),
        name="ret_prompt",
    )(proj, proj, proj, proj, cos, sin, *tables)


def _ret_sample_kernel(*refs, bb, scale, chained):
    (q_ref, k_ref, v_ref, g_ref, cos_ref, sin_ref, mask_ref, qd_ref, kd_ref, cd_ref,
     s_in_ref) = refs[:11]
    o_ref, s_out_ref, kdt_ref, acc_ref = refs[12 if chained else 11:]

    for later in range(1, s_out_ref.shape[0]):
        s_out_ref[later] = jnp.zeros(s_out_ref.shape[1:], s_out_ref.dtype)
    dt, _, dk = q_ref.shape
    rows = dt * bb
    cos, sin = cos_ref[...], sin_ref[...]
    q = _rotate(q_ref[...].reshape(rows, dk), cos, sin)
    k = _rotate(k_ref[...].reshape(rows, dk), cos, sin) * scale
    qb, kb = q.astype(BF16), k.astype(BF16)
    vb = v_ref[...].reshape(rows, dk).astype(BF16)
    scores = lax.dot_general(qb, kb, (((1,), (1,)), ((), ())),
                             preferred_element_type=F32) * mask_ref[0]
    intra = _bdot(scores.astype(BF16), vb)
    kdt_ref[...] = (k * kd_ref[0]).T
    cd = cd_ref[0]
    row_batch = lax.broadcasted_iota(jnp.int32, (rows, 1), 0) % bb
    col_batch = lax.broadcasted_iota(jnp.int32, (1, rows), 1) % bb

    acc_ref[...] = jnp.zeros_like(acc_ref)

    def body(j, carry):
        s = s_in_ref[0, j, 0]
        acc_ref[...] = jnp.where(row_batch == j, _bdot(qb, s.astype(BF16)), acc_ref[...])
        kdt_j = jnp.where(col_batch == j, kdt_ref[...], 0.0).astype(BF16)
        s_out_ref[0, j, 0] = s * cd + _bdot(kdt_j, vb)
        return carry

    lax.fori_loop(0, bb, body, 0, unroll=4)
    o = intra + acc_ref[...] * qd_ref[0]
    o = o * lax.rsqrt(jnp.mean(o * o, axis=-1, keepdims=True) + EPS)
    gr = g_ref[...].reshape(rows, dk)
    o_ref[...] = (o * (gr * jax.nn.sigmoid(gr))).astype(o_ref.dtype).reshape(dt, bb, dk)


def _ret_sample(proj, state_ret, e, prev_states, db, dt, half, past_len):
    dk = half // RET_HEADS
    h = RET_HEADS
    assert RET_CHUNK % dt == 0 and RET_SAMPLE_ROWS % dt == 0
    rows = RET_SAMPLE_ROWS
    bb = rows // dt
    assert db % bb == 0
    proj3 = proj.reshape(dt, db, proj.shape[1])
    cos, sin = _rope_tables(past_len + np.arange(dt), dk // 2)
    cos, sin = np.repeat(cos, bb, axis=0), np.repeat(sin, bb, axis=0)
    mask, q_dec, k_dec, chunk_dec = _decay_tables(dt, dk)
    mask = np.einsum("hnm,ab->hnamb", mask, np.eye(bb, dtype=np.float32)).reshape(h, rows, rows)
    q_dec, k_dec = np.repeat(q_dec, bb, axis=1), np.repeat(k_dec, bb, axis=1)

    def col(off):
        return pl.BlockSpec((dt, bb, dk), lambda bi, hi: (0, bi, off + hi))

    def per_head(shape):
        return pl.BlockSpec((1,) + shape, lambda bi, hi: (hi, 0, 0))

    chained = prev_states is not None
    if chained:
        out_state_block = pl.BlockSpec((1, bb, 1, dk, dk), lambda bi, hi: (e, bi, hi, 0, 0))
    else:
        assert e == 0
        n_layers = state_ret.shape[0]
        out_state_block = pl.BlockSpec((n_layers, bb, 1, dk, dk),
                                       lambda bi, hi: (0, bi, hi, 0, 0))
    operands = [proj3, proj3, proj3, proj3, cos, sin, mask, q_dec, k_dec, chunk_dec, state_ret]
    in_specs = [
        col(0), col(h), col(2 * h), col(3 * h),
        pl.BlockSpec((rows, dk // 2), lambda bi, hi: (0, 0)),
        pl.BlockSpec((rows, dk // 2), lambda bi, hi: (0, 0)),
        per_head((rows, rows)), per_head((rows, dk)), per_head((rows, dk)),
        per_head((1, dk)),
        pl.BlockSpec((1, bb, 1, dk, dk), lambda bi, hi: (e, bi, hi, 0, 0)),
    ]
    if chained:
        operands.append(prev_states)
        in_specs.append(pl.BlockSpec(memory_space=pl.ANY))
    out_layers = out_state_block.block_shape[0]
    vmem = (2 + 2 * out_layers) * bb * dk * dk * 4 + 12 * rows * dk * 4 + 4 * dk * dk * 4
    o, states = pl.pallas_call(
        functools.partial(_ret_sample_kernel, bb=bb, scale=dk ** -0.5, chained=chained),
        grid=(db // bb, h),
        in_specs=in_specs,
        out_specs=[
            pl.BlockSpec((dt, bb, dk), lambda bi, hi: (0, bi, hi)),
            out_state_block,
        ],
        out_shape=[jax.ShapeDtypeStruct((dt, db, half), BF16),
                   jax.ShapeDtypeStruct(state_ret.shape, state_ret.dtype)],
        scratch_shapes=[pltpu.VMEM((dk, rows), F32), pltpu.VMEM((rows, dk), F32)],
        input_output_aliases={len(operands) - 1: 1} if chained else {},
        compiler_params=_params(("parallel", "parallel"), vmem),
        name="ret_sample",
    )(*operands)
    return o.reshape(dt * db, half), states


def _lru_gates(xc, wa_ref, ba_ref, wx_ref, bx_ref, sp_ref):
    xcb = xc.astype(BF16)
    blk = wa_ref.shape[1]
    ra, ri = [], []
    for n in range(wa_ref.shape[0]):
        xn = xcb[:, n * blk:(n + 1) * blk]
        ra.append(_bdot(xn, wa_ref[n].astype(BF16)))
        ri.append(_bdot(xn, wx_ref[n].astype(BF16)))
    r = jax.nn.sigmoid(jnp.concatenate(ra, axis=-1) + ba_ref[...])
    i = jax.nn.sigmoid(jnp.concatenate(ri, axis=-1) + bx_ref[...])
    log_a = -LRU_C * r * sp_ref[...]
    a = jnp.exp(log_a)
    mult = jnp.sqrt(jnp.maximum(-jnp.tanh(log_a) * (a * a + 1.0), 0.0))
    return a, mult * i * xc


def _lru_prompt_kernel(xl_ref, yl_ref, cw_ref, cb_ref, wa_ref, ba_ref, wx_ref, bx_ref,
                       sp_ref, lo_ref, h_out_ref, conv_out_ref, xs_ref, a_ref, b_ref,
                       hc_ref, *, n_chunks):
    c = pl.program_id(1)
    tc, w = xl_ref.shape
    sub = V7X_SUBLANES

    @pl.when(c == 0)
    def _():
        xs_ref[0:sub, :] = jnp.zeros((sub, w), F32)
        hc_ref[...] = jnp.zeros_like(hc_ref)

    x = xl_ref[...]
    xs_ref[sub:sub + tc, :] = x
    xc = cb_ref[...] + x * cw_ref[CONV_W - 1:CONV_W, :]
    for i in range(CONV_W - 1):
        back = CONV_W - 1 - i
        xc = xc + xs_ref[sub - back:sub - back + tc, :] * cw_ref[i:i + 1, :]
    xs_ref[0:sub, :] = xs_ref[tc:tc + sub, :]

    a, bt = _lru_gates(xc, wa_ref, ba_ref, wx_ref, bx_ref, sp_ref)

    a3 = a.reshape(tc // sub, sub, w)
    b3 = bt.reshape(tc // sub, sub, w)
    step = lax.broadcasted_iota(jnp.int32, (1, sub, 1), 1)
    for s in (1, 2, 4):
        keep = step >= s
        a_prev = jnp.where(keep, pltpu.roll(a3, s, axis=1), 1.0)
        b_prev = jnp.where(keep, pltpu.roll(b3, s, axis=1), 0.0)
        b3 = a3 * b_prev + b3
        a3 = a3 * a_prev
    a_ref[...] = a3.reshape(tc, w)
    b_ref[...] = b3.reshape(tc, w)

    def body(g, h):
        sl = pl.ds(pl.multiple_of(g * sub, sub), sub)
        hg = b_ref[sl, :] + a_ref[sl, :] * h
        b_ref[sl, :] = hg
        return jnp.broadcast_to(hg[sub - 1:sub, :], (sub, w))

    h_last = lax.fori_loop(0, tc // sub, body, hc_ref[...])
    hc_ref[...] = h_last
    lo_ref[...] = (jax.nn.gelu(yl_ref[...]) * b_ref[...]).astype(lo_ref.dtype)

    @pl.when(c == n_chunks - 1)
    def _():
        h_out_ref[0] = h_last[0:1, :]
        conv_out_ref[0] = xs_ref[sub - (CONV_W - 1):sub, :]


def _lru_prompt(proj, b, t, w, cw, cb, wa, ba, wx, bx, sp):
    tc = _pick_tile(t, 256, 16)
    nc = t // tc
    xl_col = (proj.shape[1] - 2 * w) // w
    vec = lambda: pl.BlockSpec((1, w), lambda bi, ci: (0, 0))
    blocks = lambda: pl.BlockSpec(wa.shape, lambda bi, ci: (0, 0, 0))
    vmem = 4 * tc * w * 4 + 2 * tc * w * 2 + 3 * tc * w * 4 + 8 * tc * w * 4
    return pl.pallas_call(
        functools.partial(_lru_prompt_kernel, n_chunks=nc),
        grid=(b, nc),
        in_specs=[
            pl.BlockSpec((tc, w), lambda bi, ci: (bi * nc + ci, xl_col)),
            pl.BlockSpec((tc, w), lambda bi, ci: (bi * nc + ci, xl_col + 1)),
            pl.BlockSpec((CONV_W, w), lambda bi, ci: (0, 0)),
            vec(), blocks(), vec(), blocks(), vec(), vec(),
        ],
        out_specs=[
            pl.BlockSpec((tc, w), lambda bi, ci: (bi * nc + ci, 0)),
            pl.BlockSpec((1, 1, w), lambda bi, ci: (bi, 0, 0)),
            pl.BlockSpec((1, CONV_W - 1, w), lambda bi, ci: (bi, 0, 0)),
        ],
        out_shape=[jax.ShapeDtypeStruct((b * t, w), BF16),
                   jax.ShapeDtypeStruct((b, 1, w), F32),
                   jax.ShapeDtypeStruct((b, CONV_W - 1, w), F32)],
        scratch_shapes=[pltpu.VMEM((tc + V7X_SUBLANES, w), F32),
                        pltpu.VMEM((tc, w), F32), pltpu.VMEM((tc, w), F32),
                        pltpu.VMEM((V7X_SUBLANES, w), F32)],
        compiler_params=_params(("parallel", "arbitrary"), vmem),
        name="lru_prompt",
    )(proj, proj, cw, cb.reshape(1, w), wa, ba.reshape(1, w), wx, bx.reshape(1, w), sp)


def _lru_sample_kernel(xl_ref, yl_ref, conv0_ref, h0_ref, cw_ref, cb_ref, wa_ref, ba_ref,
                       wx_ref, bx_ref, sp_ref, lo_ref, h_out_ref, conv_out_ref, *, dt):
    db = h0_ref.shape[1]
    w = h0_ref.shape[2]
    taps = CONV_W - 1
    xp = [conv0_ref[0, :, i * w:(i + 1) * w] for i in range(taps)]
    xp += [xl_ref[t * db:(t + 1) * db, :] for t in range(dt)]
    xcs = []
    for t in range(dt):
        xc = cb_ref[...] + xp[t] * cw_ref[0:1, :]
        for i in range(1, CONV_W):
            xc = xc + xp[t + i] * cw_ref[i:i + 1, :]
        xcs.append(xc)
    a, bt = _lru_gates(jnp.concatenate(xcs, axis=0), wa_ref, ba_ref, wx_ref, bx_ref, sp_ref)
    h = h0_ref[0]
    for t in range(dt):
        rows = slice(t * db, (t + 1) * db)
        h = a[rows] * h + bt[rows]
        lo_ref[rows, :] = (jax.nn.gelu(yl_ref[rows, :]) * h).astype(lo_ref.dtype)
    h_out_ref[...] = h
    for i in range(taps):
        conv_out_ref[:, i * w:(i + 1) * w] = xp[dt + i]


def _lru_sample(proj, state_conv, state_lru, e, db, dt, w, cw, cb, wa, ba, wx, bx, sp):
    rows = db * dt
    assert proj.shape[0] == rows
    rb = 0
    xl_col = (proj.shape[1] - 2 * w) // w
    taps = CONV_W - 1
    conv0 = state_conv.reshape(state_conv.shape[0], db, taps * w)
    vec = lambda: pl.BlockSpec((1, w), lambda i: (0, 0))
    blocks = lambda: pl.BlockSpec(wa.shape, lambda i: (0, 0, 0))
    vmem = 16 * rows * w * 4
    lo, h_new, conv_new = pl.pallas_call(
        functools.partial(_lru_sample_kernel, dt=dt),
        grid=(1,),
        in_specs=[
            pl.BlockSpec((rows, w), lambda i: (rb, xl_col)),
            pl.BlockSpec((rows, w), lambda i: (rb, xl_col + 1)),
            pl.BlockSpec((1, db, taps * w), lambda i: (e, 0, 0)),
            pl.BlockSpec((1, db, w), lambda i: (e, 0, 0)),
            pl.BlockSpec((CONV_W, w), lambda i: (0, 0)),
            vec(), blocks(), vec(), blocks(), vec(), vec(),
        ],
        out_specs=[
            pl.BlockSpec((rows, w), lambda i: (0, 0)),
            pl.BlockSpec((db, w), lambda i: (0, 0)),
            pl.BlockSpec((db, taps * w), lambda i: (0, 0)),
        ],
        out_shape=[jax.ShapeDtypeStruct((rows, w), BF16),
                   jax.ShapeDtypeStruct((db, w), F32),
                   jax.ShapeDtypeStruct((db, taps * w), F32)],
        compiler_params=_params(("arbitrary",), vmem),
        name="lru_sample",
    )(proj, proj, conv0, state_lru, cw, cb.reshape(1, w), wa, ba.reshape(1, w), wx,
      bx.reshape(1, w), sp)
    return lo, h_new, conv_new.reshape(db, taps, w)


def _s5_tables(a_re, a_im, b_re, b_im, c_re, c_im, d, log_dt):
    g = a_re.shape[0]
    nb = g // SSM_BLOCK_GROUPS
    dt = jnp.exp(log_dt)[:, None]
    mag = jnp.exp(a_re * dt)
    abr = mag * jnp.cos(a_im * dt)
    abi = mag * jnp.sin(a_im * dt)
    den = a_re * a_re + a_im * a_im
    nr, ni = abr - 1.0, abi
    fr = (nr * a_re + ni * a_im) / den
    fi = (ni * a_re - nr * a_im) / den
    bbr = fr[..., None] * b_re - fi[..., None] * b_im
    bbi = fr[..., None] * b_im + fi[..., None] * b_re
    per_lane_block = V7X_LANES // SSM_P
    owner = (jnp.arange(SSM_BLOCK_GROUPS) % per_lane_block)[None, :, None, None, None]
    slot = jnp.arange(per_lane_block)[None, None, None, :, None]

    def pack(gkp):
        gkp = gkp.reshape(nb, SSM_BLOCK_GROUPS, SSM_GROUP, 1, SSM_P)
        return jnp.where(owner == slot, gkp, 0.0).reshape(nb, V7X_MXU_DIM, V7X_LANES)

    p_in = jnp.stack([pack(bbr.transpose(0, 2, 1)), pack(bbi.transpose(0, 2, 1))])
    p_out = jnp.stack([pack(c_re), pack(-c_im)])
    return (abr.reshape(nb, SSM_BLOCK_STATES), abi.reshape(nb, SSM_BLOCK_STATES),
            p_in, p_out, d.reshape(nb, 1, V7X_MXU_DIM))


def _expand_blockdiag(packed_ref, i, w_ref):
    ln = V7X_LANES
    per_lane_block = ln // SSM_P
    w_ref[...] = jnp.zeros_like(w_ref)
    for c in range(2):
        for g in range(SSM_BLOCK_GROUPS):
            rows = slice(g * SSM_GROUP, (g + 1) * SSM_GROUP)
            col = c * SSM_BLOCK_STATES + (g // per_lane_block) * ln
            w_ref[rows, col:col + ln] = packed_ref[c, i, rows, :].astype(w_ref.dtype)


def _s5_prompt_kernel(x_ref, g_ref, ar_ref, ai_ref, pin_ref, pout_ref, d_ref, y_ref,
                      hr_out_ref, hi_out_ref, win_ref, wout_ref, u_ref, sr_ref, si_ref,
                      hr_ref, hi_ref, *, n_chunks):
    c = pl.program_id(1)
    tc = x_ref.shape[0]
    nb = win_ref.shape[0]
    ns = SSM_BLOCK_STATES
    bw = V7X_MXU_DIM
    ln = V7X_LANES
    slots = V7X_SUBLANES
    nl = sr_ref.shape[0]
    splits = ns // (nl * ln)
    blocks_per_pass = slots // splits
    pitch = S5_ROW_PITCH

    @pl.when(c == 0)
    def _():
        hr_ref[...] = jnp.zeros_like(hr_ref)
        hi_ref[...] = jnp.zeros_like(hi_ref)
        for i in range(nb):
            _expand_blockdiag(pin_ref, i, win_ref.at[i])
            _expand_blockdiag(pout_ref, i, wout_ref.at[i])

    u_ref[...] = _rms(x_ref[...], g_ref[...])

    def slot_rows(slot):
        return pl.ds(slot, tc, stride=pitch)

    for p in range(nb // blocks_per_pass):
        blocks = range(p * blocks_per_pass, (p + 1) * blocks_per_pass)
        for il, i in enumerate(blocks):
            bu = _bdot(u_ref[:, i * bw:(i + 1) * bw].astype(BF16), win_ref[i])
            for sp in range(splits):
                for l in range(nl):
                    col = (sp * nl + l) * ln
                    sr_ref[l, slot_rows(il * splits + sp), :] = bu[:, col:col + ln]
                    si_ref[l, slot_rows(il * splits + sp), :] = bu[:, ns + col:ns + col + ln]

        srows = slice(p * slots, (p + 1) * slots)
        ar = [ar_ref[srows, l * ln:(l + 1) * ln] for l in range(nl)]
        ai = [ai_ref[srows, l * ln:(l + 1) * ln] for l in range(nl)]

        def body(t, carry):
            rows = pl.ds(t * pitch, slots)
            out = []
            for l in range(nl):
                hr, hi = carry[l]
                hr_n = ar[l] * hr - ai[l] * hi + sr_ref[l, rows, :]
                hi_n = ar[l] * hi + ai[l] * hr + si_ref[l, rows, :]
                sr_ref[l, rows, :] = hr_n
                si_ref[l, rows, :] = hi_n
                out.append((hr_n, hi_n))
            return tuple(out)

        init = tuple((hr_ref[srows, l * ln:(l + 1) * ln], hi_ref[srows, l * ln:(l + 1) * ln])
                     for l in range(nl))
        last = lax.fori_loop(0, tc, body, init, unroll=8)
        for l in range(nl):
            hr_ref[srows, l * ln:(l + 1) * ln] = last[l][0]
            hi_ref[srows, l * ln:(l + 1) * ln] = last[l][1]

        for il, i in enumerate(blocks):
            parts = [ref[l, slot_rows(il * splits + sp), :].astype(BF16)
                     for ref in (sr_ref, si_ref) for sp in range(splits) for l in range(nl)]
            cols = slice(i * bw, (i + 1) * bw)
            y = _bdot_nt(jnp.concatenate(parts, axis=-1), wout_ref[i]) + d_ref[i] * u_ref[:, cols]
            y_ref[:, cols] = jax.nn.gelu(y).astype(y_ref.dtype)

    @pl.when(c == n_chunks - 1)
    def _():
        hr_out_ref[0] = hr_ref[...]
        hi_out_ref[0] = hi_ref[...]


def _s5_prompt(x, g, tables, b, t):
    d = x.shape[1]
    abr, abi, p_in, p_out, dd = tables
    nb = abr.shape[0]
    ns = SSM_BLOCK_STATES
    w_shape = (nb, V7X_MXU_DIM, 2 * ns)
    tc = _pick_tile(t, 256, 16)
    nc = t // tc
    passes = 2
    assert nb % passes == 0 and V7X_SUBLANES % (nb // passes) == 0
    splits = V7X_SUBLANES // (nb // passes)
    slot_lanes = ns // splits
    nl = slot_lanes // V7X_LANES
    abr = abr.reshape(nb * splits, slot_lanes)
    abi = abi.reshape(nb * splits, slot_lanes)
    const = lambda a: _once(a.shape, lambda bi, ci: (0,) * a.ndim)
    scan_bytes = nl * tc * S5_ROW_PITCH * V7X_LANES * 4
    w_bytes = w_shape[0] * w_shape[1] * w_shape[2] * 2
    vmem = (5 * tc * d * 4 + 2 * tc * d * 2 + 2 * w_bytes + 2 * p_in.size * 4
            + 2 * scan_bytes + 6 * tc * 2 * ns * 4)
    y, hr, hi = pl.pallas_call(
        functools.partial(_s5_prompt_kernel, n_chunks=nc),
        grid=(b, nc),
        in_specs=[
            pl.BlockSpec((tc, d), lambda bi, ci: (bi * nc + ci, 0)),
            pl.BlockSpec((1, d), lambda bi, ci: (0, 0)),
            const(abr), const(abi), const(p_in), const(p_out), const(dd),
        ],
        out_specs=[
            pl.BlockSpec((tc, d), lambda bi, ci: (bi * nc + ci, 0)),
            pl.BlockSpec((1,) + abr.shape, lambda bi, ci: (bi, 0, 0)),
            pl.BlockSpec((1,) + abr.shape, lambda bi, ci: (bi, 0, 0)),
        ],
        out_shape=[jax.ShapeDtypeStruct((b * t, d), BF16),
                   jax.ShapeDtypeStruct((b,) + abr.shape, F32),
                   jax.ShapeDtypeStruct((b,) + abr.shape, F32)],
        scratch_shapes=[pltpu.VMEM(w_shape, BF16), pltpu.VMEM(w_shape, BF16),
                        pltpu.VMEM((tc, d), F32),
                        pltpu.VMEM((nl, tc * S5_ROW_PITCH, V7X_LANES), F32),
                        pltpu.VMEM((nl, tc * S5_ROW_PITCH, V7X_LANES), F32),
                        pltpu.VMEM(abr.shape, F32), pltpu.VMEM(abr.shape, F32)],
        compiler_params=_params(("parallel", "arbitrary"), vmem),
        name="s5_prompt",
    )(x, g.reshape(1, d), abr, abi, p_in, p_out, dd)
    return y, hr.reshape(b, nb * ns), hi.reshape(b, nb * ns)


def _s5_sample_kernel(x_ref, g_ref, ar_ref, ai_ref, pin_ref, pout_ref, d_ref, h0r_ref,
                      h0i_ref, y_ref, hr_out_ref, hi_out_ref, win_ref, wout_ref, uf_ref,
                      ub_ref, *, dt):
    i = pl.program_id(0)
    nb = uf_ref.shape[0]
    bw = V7X_MXU_DIM
    ns = SSM_BLOCK_STATES
    db = h0r_ref.shape[1]

    @pl.when(i == 0)
    def _():
        u = _rms(x_ref[...], g_ref[...])
        for n in range(nb):
            uf_ref[n] = u[:, n * bw:(n + 1) * bw]
            ub_ref[n] = u[:, n * bw:(n + 1) * bw].astype(BF16)

    _expand_blockdiag(pin_ref, 0, win_ref)
    _expand_blockdiag(pout_ref, 0, wout_ref)
    bu = _bdot(ub_ref[i], win_ref[...])
    ar, ai = ar_ref[pl.ds(i, 1), :], ai_ref[pl.ds(i, 1), :]
    hr, hi = h0r_ref[0], h0i_ref[0]
    states = []
    for t in range(dt):
        rows = slice(t * db, (t + 1) * db)
        hr, hi = (ar * hr - ai * hi + bu[rows, :ns], ar * hi + ai * hr + bu[rows, ns:])
        states.append(jnp.concatenate([hr.astype(BF16), hi.astype(BF16)], axis=-1))
    y = _bdot_nt(jnp.concatenate(states, axis=0), wout_ref[...]) + d_ref[0] * uf_ref[i]
    y_ref[...] = jax.nn.gelu(y).astype(y_ref.dtype)
    hr_out_ref[...] = hr
    hi_out_ref[...] = hi


def _s5_sample(x, g, tables, state_re, state_im, o, db, dt):
    d = x.shape[1]
    abr, abi, p_in, p_out, dd = tables
    nb = abr.shape[0]
    ns = SSM_BLOCK_STATES
    bw = V7X_MXU_DIM
    rows = db * dt
    assert x.shape[0] == rows
    rb = 0
    no = state_re.shape[0]
    h0r = state_re.reshape(no, db, nb * ns)
    h0i = state_im.reshape(no, db, nb * ns)
    blk = lambda a: pl.BlockSpec((1,) + a.shape[1:], lambda i: (i, 0, 0))
    packed = lambda a: pl.BlockSpec((2, 1) + a.shape[2:], lambda i: (0, i, 0, 0))
    vmem = (2 * rows * d * 4 + rows * d * 6 + 4 * bw * 2 * ns * 2 + 8 * db * ns * 4
            + 8 * rows * 2 * ns * 4)
    return pl.pallas_call(
        functools.partial(_s5_sample_kernel, dt=dt),
        grid=(nb,),
        in_specs=[
            _once((rows, d), lambda i: (rb, 0)),
            pl.BlockSpec((1, d), lambda i: (0, 0)),
            pl.BlockSpec(abr.shape, lambda i: (0, 0)),
            pl.BlockSpec(abi.shape, lambda i: (0, 0)),
            packed(p_in), packed(p_out), blk(dd),
            pl.BlockSpec((1, db, ns), lambda i: (o, 0, i)),
            pl.BlockSpec((1, db, ns), lambda i: (o, 0, i)),
        ],
        out_specs=[
            pl.BlockSpec((rows, bw), lambda i: (0, i)),
            pl.BlockSpec((db, ns), lambda i: (0, i)),
            pl.BlockSpec((db, ns), lambda i: (0, i)),
        ],
        out_shape=[jax.ShapeDtypeStruct((rows, d), BF16),
                   jax.ShapeDtypeStruct((db, nb * ns), F32),
                   jax.ShapeDtypeStruct((db, nb * ns), F32)],
        scratch_shapes=[pltpu.VMEM((bw, 2 * ns), BF16), pltpu.VMEM((bw, 2 * ns), BF16),
                        pltpu.VMEM((nb, rows, bw), F32), pltpu.VMEM((nb, rows, bw), BF16)],
        compiler_params=_params(("arbitrary",), vmem),
        name="s5_sample",
    )(x, g.reshape(1, d), abr, abi, p_in, p_out, dd, h0r, h0i)


def kernel(x_prompt, x_sample, state_ret, state_lru, state_conv, state_ssm_re, state_ssm_im, norm_mix_even, w_in_even, lru_conv_w, lru_conv_b, lru_wa, lru_ba, lru_wx, lru_bx, lru_lambda, w_out_even, norm_mix_odd, ssm_a_re, ssm_a_im, ssm_b_re, ssm_b_im, ssm_c_re, ssm_c_im, ssm_d, ssm_log_dt, w_glu, b_glu, norm_ffn, w_ffn_gu, w_ffn_down, norm_final):
    b, t, d = x_prompt.shape
    db, dt, _ = x_sample.shape
    depth = norm_ffn.shape[0]
    half = d // 2
    past_len = PAST_LEN
    groups, ssm_p = ssm_a_re.shape[1:]
    assert ssm_p == SSM_P and groups * SSM_GROUP == d and groups % SSM_BLOCK_GROUPS == 0

    xp = x_prompt.reshape(b * t, d)
    xs = x_sample.transpose(1, 0, 2).reshape(dt * db, d)

    rets_p, ret_s, lrus_p, lrus_s, convs_p, convs_s = [], None, [], [], [], []
    sres_p, sres_s, sims_p, sims_s = [], [], [], []
    for layer in range(depth):
        if layer % 2 == 0:
            e = layer // 2
            sp = jax.nn.softplus(-lru_lambda[e]).reshape(1, half)
            lru_w = (lru_conv_w[e], lru_conv_b[e], lru_wa[e], lru_ba[e], lru_wx[e], lru_bx[e], sp)

            proj_p = _inproj(xp, norm_mix_even[e], w_in_even, e)
            o_p, ret_p = _ret_prompt(proj_p, b, t, half)
            lo_p, lru_p, conv_p = _lru_prompt(proj_p, b, t, half, *lru_w)
            xp = _outproj(o_p, lo_p, w_out_even, e, xp)

            proj_s = _inproj(xs, norm_mix_even[e], w_in_even, e)
            o_s, ret_s = _ret_sample(proj_s, state_ret, e, ret_s, db, dt, half, past_len)
            lo_s, lru_s, conv_s = _lru_sample(proj_s, state_conv, state_lru, e, db, dt, half,
                                              *lru_w)
            xs = _outproj(o_s, lo_s, w_out_even, e, xs)

            rets_p.append(ret_p)
            lrus_p.append(lru_p.reshape(b, half))
            lrus_s.append(lru_s)
            convs_p.append(conv_p)
            convs_s.append(conv_s)
        else:
            o = layer // 2
            tables = _s5_tables(ssm_a_re[o], ssm_a_im[o], ssm_b_re[o], ssm_b_im[o],
                                ssm_c_re[o], ssm_c_im[o], ssm_d[o], ssm_log_dt[o])
            y_p, sre_p, sim_p = _s5_prompt(xp, norm_mix_odd[o], tables, b, t)
            xp = _glu(y_p, w_glu, o, b_glu[o], xp)
            y_s, sre_s, sim_s = _s5_sample(xs, norm_mix_odd[o], tables, state_ssm_re,
                                           state_ssm_im, o, db, dt)
            xs = _glu(y_s, w_glu, o, b_glu[o], xs)
            sres_p.append(sre_p.reshape(b, groups, ssm_p))
            sims_p.append(sim_p.reshape(b, groups, ssm_p))
            sres_s.append(sre_s.reshape(db, groups, ssm_p))
            sims_s.append(sim_s.reshape(db, groups, ssm_p))
        g_final = norm_final if layer == depth - 1 else None
        xp = _ffn(xp, norm_ffn[layer], w_ffn_gu, w_ffn_down, layer, g_final)
        xs = _ffn(xs, norm_ffn[layer], w_ffn_gu, w_ffn_down, layer, g_final)

    y_prompt = xp.reshape(b, t, d)
    y_sample = xs.reshape(dt, db, d).transpose(1, 0, 2)
    return (y_prompt, y_sample, jnp.stack(rets_p), ret_s, jnp.stack(lrus_p),
            jnp.stack(lrus_s), jnp.stack(convs_p), jnp.stack(convs_s), jnp.stack(sres_p),
            jnp.stack(sres_s), jnp.stack(sims_p), jnp.stack(sims_s))
```

```python
import functools

import jax
import jax.numpy as jnp
import numpy as np
from jax import lax
from jax.experimental import pallas as pl
from jax.experimental.pallas import tpu as pltpu

F32 = jnp.float32
BF16 = jnp.bfloat16

EPS = 1e-6
PAST_LEN = 16384
ROPE_BASE = 10000.0
RET_HEADS = 4
RET_CHUNK = 128
LRU_HEADS = 8
LRU_C = 8.0
CONV_W = 4
SSM_GROUP = 16
SSM_P = 64
SSM_CHUNK = 128

V7X_SUBLANES = 8
V7X_LANES = 128
V7X_MXU_DIM = 256
V7X_VMEM_BYTES = 64 * 1024 * 1024
VMEM_LIMIT_CAP = V7X_VMEM_BYTES - 6 * 1024 * 1024

DENSE_ROW_TILE = 2048

DENSE_VMEM_BUDGET = 44 * 1024 * 1024

TILE_COPY_CHUNKS = 8

SSM_BLOCK_GROUPS = V7X_MXU_DIM // SSM_GROUP
SSM_BLOCK_STATES = SSM_BLOCK_GROUPS * SSM_P

RET_SAMPLE_ROWS = 128

S5_ROW_PITCH = 12


def _pick_tile(n, target, mult):
    best = None
    for t in range(mult, min(n, target) + 1, mult):
        if n % t == 0:
            best = t
    assert best is not None, (n, target, mult)
    return best


def _widest_fitting_tile(n, targets, vmem_for):
    for target in targets:
        tile = _pick_tile(n, target, V7X_LANES)
        if vmem_for(tile) <= DENSE_VMEM_BUDGET:
            return tile
    return tile


def _params(semantics, vmem_bytes):
    limit = min(int(vmem_bytes * 1.2) + (6 << 20), VMEM_LIMIT_CAP)
    return pltpu.CompilerParams(dimension_semantics=semantics, vmem_limit_bytes=limit)


def _once(block_shape, index_map):
    return pl.BlockSpec(block_shape, index_map, pipeline_mode=pl.Buffered(1))


def _rms(x, g):
    return x * lax.rsqrt(jnp.mean(x * x, axis=-1, keepdims=True) + EPS) * g


def _norm_rows_into(x_ref, g_ref, xn_ref):
    tm = x_ref.shape[0]
    rows = _pick_tile(tm, 64, 16)

    def body(r, carry):
        sl = pl.ds(pl.multiple_of(r * rows, rows), rows)
        xn_ref[sl, :] = _rms(x_ref[sl, :], g_ref[...]).astype(xn_ref.dtype)
        return carry

    lax.fori_loop(0, tm // rows, body, 0)


def _row_chunk_copies(src, dst, sems):
    n = sems.shape[0]
    rc = src.shape[0] // n
    return [pltpu.make_async_copy(src.at[pl.ds(c * rc, rc), :], dst.at[pl.ds(c * rc, rc), :],
                                  sems.at[c]) for c in range(n)]


def _load_and_norm_rows(x_hbm_rows, x_ref, g_ref, xn_ref, sems):
    copies = _row_chunk_copies(x_hbm_rows, x_ref, sems)
    rc = x_ref.shape[0] // len(copies)
    for cp in copies:
        cp.start()
    for c, cp in enumerate(copies):
        cp.wait()
        rows = pl.ds(c * rc, rc)
        _norm_rows_into(x_ref.at[rows, :], g_ref, xn_ref.at[rows, :])


def _bdot(a, b):
    return jnp.dot(a, b, preferred_element_type=F32)


def _bdot_nt(a, b):
    return lax.dot_general(a, b, (((1,), (1,)), ((), ())), preferred_element_type=F32)


def _inproj_kernel(x_hbm, g_ref, w_ref, o_ref, x_ref, xn_ref, x_sems):
    @pl.when(pl.program_id(1) == 0)
    def _():
        tm = x_ref.shape[0]
        tile = pl.ds(pl.multiple_of(pl.program_id(0) * tm, tm), tm)
        _load_and_norm_rows(x_hbm.at[tile, :], x_ref, g_ref, xn_ref, x_sems)

    o_ref[...] = _bdot(xn_ref[...], w_ref[...].astype(BF16))


def _inproj(x, g, w, e):
    m, d = x.shape
    n = w.shape[2]
    tm = _pick_tile(m, DENSE_ROW_TILE, 16)
    tn = _pick_tile(n, 512, V7X_LANES)
    vmem = tm * d * 4 + tm * d * 2 + 2 * d * tn * 4 + 2 * tm * tn * 4 + d * tn * 2
    return pl.pallas_call(
        _inproj_kernel,
        grid=(m // tm, n // tn),
        in_specs=[
            pl.BlockSpec(memory_space=pl.ANY),
            _once((1, d), lambda i, j: (0, 0)),
            pl.BlockSpec((None, d, tn), lambda i, j: (e, 0, j)),
        ],
        out_specs=pl.BlockSpec((tm, tn), lambda i, j: (i, j)),
        out_shape=jax.ShapeDtypeStruct((m, n), F32),
        scratch_shapes=[pltpu.VMEM((tm, d), F32), pltpu.VMEM((tm, d), BF16),
                        pltpu.SemaphoreType.DMA((TILE_COPY_CHUNKS,))],
        compiler_params=_params(("parallel", "arbitrary"), vmem),
        name="inproj",
    )(x, g.reshape(1, d), w)


def _outproj_kernel(o_ref, lo_ref, wo_ref, wl_ref, x_ref, out_ref):
    acc = _bdot(o_ref[...], wo_ref[...].astype(BF16))
    acc = acc + _bdot(lo_ref[...], wl_ref[...].astype(BF16))
    out_ref[...] = x_ref[...] + acc


def _outproj(o, lo, w, e, x):
    m, d = x.shape
    half = o.shape[1]
    tm = _pick_tile(m, DENSE_ROW_TILE, 16)
    tn = _pick_tile(d, 512, V7X_LANES)
    vmem = 4 * tm * half * 2 + 4 * half * tn * 4 + 4 * tm * tn * 4 + 2 * half * tn * 2
    return pl.pallas_call(
        _outproj_kernel,
        grid=(m // tm, d // tn),
        in_specs=[
            pl.BlockSpec((tm, half), lambda i, j: (i, 0)),
            pl.BlockSpec((tm, half), lambda i, j: (i, 0)),
            pl.BlockSpec((None, half, tn), lambda i, j: (e, 0, j)),
            pl.BlockSpec((None, half, tn), lambda i, j: (e, 1, j)),
            pl.BlockSpec((tm, tn), lambda i, j: (i, j)),
        ],
        out_specs=pl.BlockSpec((tm, tn), lambda i, j: (i, j)),
        out_shape=jax.ShapeDtypeStruct((m, d), F32),
        compiler_params=_params(("parallel", "arbitrary"), vmem),
        name="outproj",
    )(o, lo, w, w, x)


def _glu_kernel(y_ref, w1_ref, w2_ref, b1_ref, b2_ref, x_ref, out_ref):
    y = y_ref[...]
    z1 = _bdot(y, w1_ref[...].astype(BF16)) + b1_ref[...]
    z2 = _bdot(y, w2_ref[...].astype(BF16)) + b2_ref[...]
    out_ref[...] = x_ref[...] + z1 * jax.nn.sigmoid(z2)


def _glu(y, w, o, b, x):
    m, d = x.shape
    tm = _pick_tile(m, DENSE_ROW_TILE, 16)

    def vmem_for(tn):
        return 2 * tm * d * 2 + 4 * d * tn * 4 + 4 * tm * tn * 4 + 2 * d * tn * 2 + 2 * tm * tn * 4

    tn = _widest_fitting_tile(d, (512, 256), vmem_for)
    nj = d // tn
    vmem = vmem_for(tn)
    return pl.pallas_call(
        _glu_kernel,
        grid=(m // tm, nj),
        in_specs=[
            pl.BlockSpec((tm, d), lambda i, j: (i, 0)),
            pl.BlockSpec((None, d, tn), lambda i, j: (o, 0, j)),
            pl.BlockSpec((None, d, tn), lambda i, j: (o, 0, nj + j)),
            pl.BlockSpec((1, tn), lambda i, j: (0, j)),
            pl.BlockSpec((1, tn), lambda i, j: (0, nj + j)),
            pl.BlockSpec((tm, tn), lambda i, j: (i, j)),
        ],
        out_specs=pl.BlockSpec((tm, tn), lambda i, j: (i, j)),
        out_shape=jax.ShapeDtypeStruct((m, d), F32),
        compiler_params=_params(("parallel", "arbitrary"), vmem),
        name="glu",
    )(y, w, w, b.reshape(1, 2 * d), b.reshape(1, 2 * d), x)


def _ffn_kernel(x_hbm, g_ref, gf_ref, wg_ref, wu_ref, wd_ref, out_ref, xn_ref, x_sems, *,
                final_norm):
    i, j = pl.program_id(0), pl.program_id(1)

    @pl.when(j == 0)
    def _():
        tm = out_ref.shape[0]
        tile = pl.ds(pl.multiple_of(i * tm, tm), tm)
        _load_and_norm_rows(x_hbm.at[tile, :], out_ref, g_ref, xn_ref, x_sems)

    xn = xn_ref[...]
    gate = _bdot(xn, wg_ref[...].astype(BF16))
    up = _bdot(xn, wu_ref[...].astype(BF16))
    act = (gate * jax.nn.sigmoid(gate) * up).astype(BF16)
    out_ref[...] += _bdot(act, wd_ref[...].astype(BF16))

    if final_norm:
        @pl.when(j == pl.num_programs(1) - 1)
        def _():
            _norm_rows_into(out_ref, gf_ref, out_ref)


def _ffn(x, g, w_gu, w_down, layer, g_final=None):
    m, d = x.shape
    hidden = w_down.shape[1]
    tm = _pick_tile(m, DENSE_ROW_TILE, 16)

    def vmem_for(th):
        return 2 * tm * d * 4 + tm * d * 2 + 6 * d * th * 4 + 3 * d * th * 2 + 3 * tm * th * 4

    th = _widest_fitting_tile(hidden, (512, 256), vmem_for)
    nh = hidden // th
    vmem = vmem_for(th)
    gf = g if g_final is None else g_final
    return pl.pallas_call(
        functools.partial(_ffn_kernel, final_norm=g_final is not None),
        grid=(m // tm, nh),
        in_specs=[
            pl.BlockSpec(memory_space=pl.ANY),
            _once((1, d), lambda i, j: (0, 0)),
            _once((1, d), lambda i, j: (0, 0)),
            pl.BlockSpec((None, d, th), lambda i, j: (layer, 0, j)),
            pl.BlockSpec((None, d, th), lambda i, j: (layer, 0, nh + j)),
            pl.BlockSpec((None, th, d), lambda i, j: (layer, j, 0)),
        ],
        out_specs=pl.BlockSpec((tm, d), lambda i, j: (i, 0)),
        out_shape=jax.ShapeDtypeStruct((m, d), F32),
        scratch_shapes=[pltpu.VMEM((tm, d), BF16),
                        pltpu.SemaphoreType.DMA((TILE_COPY_CHUNKS,))],
        compiler_params=_params(("parallel", "arbitrary"), vmem),
        name="ffn",
    )(x, g.reshape(1, d), gf.reshape(1, d), w_gu, w_gu, w_down)


def _rope_tables(pos, half):
    inv = 1.0 / np.power(ROPE_BASE, np.linspace(0.0, 1.0, half))
    ang = np.asarray(pos, np.float64)[:, None] * inv[None, :]
    return np.cos(ang).astype(np.float32), np.sin(ang).astype(np.float32)


def _decay_tables(c, dk):
    log_g = np.log1p(-np.exp2(-5.0 - np.arange(RET_HEADS, dtype=np.float64)))
    idx = np.arange(c, dtype=np.float64)
    diff = idx[:, None] - idx[None, :]
    mask = np.where(diff[None] >= 0,
                    np.exp(np.maximum(diff, 0.0)[None] * log_g[:, None, None]), 0.0)
    q_dec = np.exp((idx[None, :] + 1.0) * log_g[:, None])
    k_dec = np.exp((c - 1.0 - idx)[None, :] * log_g[:, None])
    chunk_dec = np.exp(c * log_g)
    q_dec = np.broadcast_to(q_dec[:, :, None], (RET_HEADS, c, dk))
    k_dec = np.broadcast_to(k_dec[:, :, None], (RET_HEADS, c, dk))
    chunk_dec = np.broadcast_to(chunk_dec[:, None, None], (RET_HEADS, 1, dk))
    f32 = lambda a: np.ascontiguousarray(a, dtype=np.float32)
    return f32(mask), f32(q_dec), f32(k_dec), f32(chunk_dec)


def _rotate(x, cos, sin):
    half = x.shape[-1] // 2
    x1, x2 = x[:, :half], x[:, half:]
    return jnp.concatenate([x1 * cos - x2 * sin, x2 * cos + x1 * sin], axis=-1)


def _ret_prompt_kernel(q_ref, k_ref, v_ref, g_ref, cos_ref, sin_ref, mask_ref, qd_ref,
                       kd_ref, cd_ref, o_ref, s_out_ref, s_ref, *, n_chunks, scale):
    c = pl.program_id(1)
    heads, dk, _ = s_ref.shape

    @pl.when(c == 0)
    def _():
        s_ref[...] = jnp.zeros_like(s_ref)

    cos, sin = cos_ref[...], sin_ref[...]
    for h in range(heads):
        cols = slice(h * dk, (h + 1) * dk)
        q = _rotate(q_ref[:, cols], cos, sin)
        k = _rotate(k_ref[:, cols], cos, sin) * scale
        qb, kb, vb = q.astype(BF16), k.astype(BF16), v_ref[:, cols].astype(BF16)
        s = s_ref[h]
        scores = lax.dot_general(qb, kb, (((1,), (1,)), ((), ())),
                                 preferred_element_type=F32) * mask_ref[h]
        o = _bdot(scores.astype(BF16), vb) + _bdot(qb, s.astype(BF16)) * qd_ref[h]
        kdb = (k * kd_ref[h]).astype(BF16)
        s_new = s * cd_ref[h] + lax.dot_general(kdb, vb, (((0,), (0,)), ((), ())),
                                                preferred_element_type=F32)
        s_ref[h] = s_new
        o = o * lax.rsqrt(jnp.mean(o * o, axis=-1, keepdims=True) + EPS)
        gr = g_ref[:, cols]
        o_ref[:, cols] = (o * (gr * jax.nn.sigmoid(gr))).astype(o_ref.dtype)

    @pl.when(c == n_chunks - 1)
    def _():
        s_out_ref[0] = s_ref[...]


def _ret_prompt(proj, b, t, half):
    h = RET_HEADS
    dk = half // h
    c = RET_CHUNK
    assert t % c == 0
    nc = t // c
    cos, sin = _rope_tables(np.arange(t), dk // 2)
    tables = _decay_tables(c, dk)

    def col(off):
        return pl.BlockSpec((c, half), lambda bi, ci: (bi * nc + ci, off))

    const = lambda a: pl.BlockSpec(a.shape, lambda bi, ci: (0, 0, 0))
    vmem = 10 * c * half * 4 + 3 * h * dk * dk * 4 + sum(2 * a.size * 4 for a in tables)
    return pl.pallas_call(
        functools.partial(_ret_prompt_kernel, n_chunks=nc, scale=dk ** -0.5),
        grid=(b, nc),
        in_specs=[
            col(0), col(1), col(2), col(3),
            pl.BlockSpec((c, dk // 2), lambda bi, ci: (ci, 0)),
            pl.BlockSpec((c, dk // 2), lambda bi, ci: (ci, 0)),
            *[const(a) for a in tables],
        ],
        out_specs=[
            pl.BlockSpec((c, half), lambda bi, ci: (bi * nc + ci, 0)),
            pl.BlockSpec((1, h, dk, dk), lambda bi, ci: (bi, 0, 0, 0)),
        ],
        out_shape=[jax.ShapeDtypeStruct((b * t, half), BF16),
                   jax.ShapeDtypeStruct((b, h, dk, dk), F32)],
        scratch_shapes=[pltpu.VMEM((h, dk, dk), F32)],
        compiler_params=_params(("parallel", "arbitrary"), vmem),
        name="ret_prompt",
    )(proj, proj, proj, proj, cos, sin, *tables)


def _ret_sample_kernel(*refs, bb, scale, chained):
    (q_ref, k_ref, v_ref, g_ref, cos_ref, sin_ref, mask_ref, qd_ref, kd_ref, cd_ref,
     s_in_ref) = refs[:11]
    o_ref, s_out_ref, kdt_ref, acc_ref = refs[12 if chained else 11:]

    for later in range(1, s_out_ref.shape[0]):
        s_out_ref[later] = jnp.zeros(s_out_ref.shape[1:], s_out_ref.dtype)
    dt, _, dk = q_ref.shape
    rows = dt * bb
    cos, sin = cos_ref[...], sin_ref[...]
    q = _rotate(q_ref[...].reshape(rows, dk), cos, sin)
    k = _rotate(k_ref[...].reshape(rows, dk), cos, sin) * scale
    qb, kb = q.astype(BF16), k.astype(BF16)
    vb = v_ref[...].reshape(rows, dk).astype(BF16)
    scores = lax.dot_general(qb, kb, (((1,), (1,)), ((), ())),
                             preferred_element_type=F32) * mask_ref[0]
    intra = _bdot(scores.astype(BF16), vb)
    kdt_ref[...] = (k * kd_ref[0]).T
    cd = cd_ref[0]
    row_batch = lax.broadcasted_iota(jnp.int32, (rows, 1), 0) % bb
    col_batch = lax.broadcasted_iota(jnp.int32, (1, rows), 1) % bb

    acc_ref[...] = jnp.zeros_like(acc_ref)

    def body(j, carry):
        s = s_in_ref[0, j, 0]
        acc_ref[...] = jnp.where(row_batch == j, _bdot(qb, s.astype(BF16)), acc_ref[...])
        kdt_j = jnp.where(col_batch == j, kdt_ref[...], 0.0).astype(BF16)
        s_out_ref[0, j, 0] = s * cd + _bdot(kdt_j, vb)
        return carry

    lax.fori_loop(0, bb, body, 0, unroll=4)
    o = intra + acc_ref[...] * qd_ref[0]
    o = o * lax.rsqrt(jnp.mean(o * o, axis=-1, keepdims=True) + EPS)
    gr = g_ref[...].reshape(rows, dk)
    o_ref[...] = (o * (gr * jax.nn.sigmoid(gr))).astype(o_ref.dtype).reshape(dt, bb, dk)


def _ret_sample(proj, state_ret, e, prev_states, db, dt, half, past_len):
    dk = half // RET_HEADS
    h = RET_HEADS
    assert RET_CHUNK % dt == 0 and RET_SAMPLE_ROWS % dt == 0
    rows = RET_SAMPLE_ROWS
    bb = rows // dt
    assert db % bb == 0
    proj3 = proj.reshape(dt, db, proj.shape[1])
    cos, sin = _rope_tables(past_len + np.arange(dt), dk // 2)
    cos, sin = np.repeat(cos, bb, axis=0), np.repeat(sin, bb, axis=0)
    mask, q_dec, k_dec, chunk_dec = _decay_tables(dt, dk)
    mask = np.einsum("hnm,ab->hnamb", mask, np.eye(bb, dtype=np.float32)).reshape(h, rows, rows)
    q_dec, k_dec = np.repeat(q_dec, bb, axis=1), np.repeat(k_dec, bb, axis=1)

    def col(off):
        return pl.BlockSpec((dt, bb, dk), lambda bi, hi: (0, bi, off + hi))

    def per_head(shape):
        return pl.BlockSpec((1,) + shape, lambda bi, hi: (hi, 0, 0))

    chained = prev_states is not None
    if chained:
        out_state_block = pl.BlockSpec((1, bb, 1, dk, dk), lambda bi, hi: (e, bi, hi, 0, 0))
    else:
        assert e == 0
        n_layers = state_ret.shape[0]
        out_state_block = pl.BlockSpec((n_layers, bb, 1, dk, dk),
                                       lambda bi, hi: (0, bi, hi, 0, 0))
    operands = [proj3, proj3, proj3, proj3, cos, sin, mask, q_dec, k_dec, chunk_dec, state_ret]
    in_specs = [
        col(0), col(h), col(2 * h), col(3 * h),
        pl.BlockSpec((rows, dk // 2), lambda bi, hi: (0, 0)),
        pl.BlockSpec((rows, dk // 2), lambda bi, hi: (0, 0)),
        per_head((rows, rows)), per_head((rows, dk)), per_head((rows, dk)),
        per_head((1, dk)),
        pl.BlockSpec((1, bb, 1, dk, dk), lambda bi, hi: (e, bi, hi, 0, 0)),
    ]
    if chained:
        operands.append(prev_states)
        in_specs.append(pl.BlockSpec(memory_space=pl.ANY))
    out_layers = out_state_block.block_shape[0]
    vmem = (2 + 2 * out_layers) * bb * dk * dk * 4 + 12 * rows * dk * 4 + 4 * dk * dk * 4
    o, states = pl.pallas_call(
        functools.partial(_ret_sample_kernel, bb=bb, scale=dk ** -0.5, chained=chained),
        grid=(db // bb, h),
        in_specs=in_specs,
        out_specs=[
            pl.BlockSpec((dt, bb, dk), lambda bi, hi: (0, bi, hi)),
            out_state_block,
        ],
        out_shape=[jax.ShapeDtypeStruct((dt, db, half), BF16),
                   jax.ShapeDtypeStruct(state_ret.shape, state_ret.dtype)],
        scratch_shapes=[pltpu.VMEM((dk, rows), F32), pltpu.VMEM((rows, dk), F32)],
        input_output_aliases={len(operands) - 1: 1} if chained else {},
        compiler_params=_params(("parallel", "parallel"), vmem),
        name="ret_sample",
    )(*operands)
    return o.reshape(dt * db, half), states


def _lru_gates(xc, wa_ref, ba_ref, wx_ref, bx_ref, sp_ref):
    xcb = xc.astype(BF16)
    blk = wa_ref.shape[1]
    ra, ri = [], []
    for n in range(wa_ref.shape[0]):
        xn = xcb[:, n * blk:(n + 1) * blk]
        ra.append(_bdot(xn, wa_ref[n].astype(BF16)))
        ri.append(_bdot(xn, wx_ref[n].astype(BF16)))
    r = jax.nn.sigmoid(jnp.concatenate(ra, axis=-1) + ba_ref[...])
    i = jax.nn.sigmoid(jnp.concatenate(ri, axis=-1) + bx_ref[...])
    log_a = -LRU_C * r * sp_ref[...]
    a = jnp.exp(log_a)
    gap = jnp.maximum(-jnp.tanh(log_a) * (a * a + 1.0), 0.0)
    mult = jnp.where(gap > 0.0, gap * lax.rsqrt(gap), 0.0)
    return a, mult * i * xc


def _lru_prompt_kernel(xl_ref, yl_ref, cw_ref, cb_ref, wa_ref, ba_ref, wx_ref, bx_ref,
                       sp_ref, lo_ref, h_out_ref, conv_out_ref, xs_ref, a_ref, b_ref,
                       hc_ref, *, n_chunks):
    c = pl.program_id(1)
    tc, w = xl_ref.shape
    sub = V7X_SUBLANES

    @pl.when(c == 0)
    def _():
        xs_ref[0:sub, :] = jnp.zeros((sub, w), F32)
        hc_ref[...] = jnp.zeros_like(hc_ref)

    x = xl_ref[...]
    xs_ref[sub:sub + tc, :] = x
    xc = cb_ref[...] + x * cw_ref[CONV_W - 1:CONV_W, :]
    for i in range(CONV_W - 1):
        back = CONV_W - 1 - i
        xc = xc + xs_ref[sub - back:sub - back + tc, :] * cw_ref[i:i + 1, :]
    xs_ref[0:sub, :] = xs_ref[tc:tc + sub, :]

    a, bt = _lru_gates(xc, wa_ref, ba_ref, wx_ref, bx_ref, sp_ref)

    a3 = a.reshape(tc // sub, sub, w)
    b3 = bt.reshape(tc // sub, sub, w)
    step = lax.broadcasted_iota(jnp.int32, (1, sub, 1), 1)
    for s in (1, 2, 4):
        keep = step >= s
        a_prev = jnp.where(keep, pltpu.roll(a3, s, axis=1), 1.0)
        b_prev = jnp.where(keep, pltpu.roll(b3, s, axis=1), 0.0)
        b3 = a3 * b_prev + b3
        a3 = a3 * a_prev
    a_ref[...] = a3.reshape(tc, w)
    b_ref[...] = b3.reshape(tc, w)

    def body(g, h):
        sl = pl.ds(pl.multiple_of(g * sub, sub), sub)
        hg = b_ref[sl, :] + a_ref[sl, :] * h
        b_ref[sl, :] = hg
        return jnp.broadcast_to(hg[sub - 1:sub, :], (sub, w))

    h_last = lax.fori_loop(0, tc // sub, body, hc_ref[...])
    hc_ref[...] = h_last
    lo_ref[...] = (jax.nn.gelu(yl_ref[...]) * b_ref[...]).astype(lo_ref.dtype)

    @pl.when(c == n_chunks - 1)
    def _():
        h_out_ref[0] = h_last[0:1, :]
        conv_out_ref[0] = xs_ref[sub - (CONV_W - 1):sub, :]


def _lru_prompt(proj, b, t, w, cw, cb, wa, ba, wx, bx, sp):
    tc = _pick_tile(t, 256, 16)
    nc = t // tc
    xl_col = (proj.shape[1] - 2 * w) // w
    vec = lambda: pl.BlockSpec((1, w), lambda bi, ci: (0, 0))
    blocks = lambda: pl.BlockSpec(wa.shape, lambda bi, ci: (0, 0, 0))
    vmem = 4 * tc * w * 4 + 2 * tc * w * 2 + 3 * tc * w * 4 + 8 * tc * w * 4
    return pl.pallas_call(
        functools.partial(_lru_prompt_kernel, n_chunks=nc),
        grid=(b, nc),
        in_specs=[
            pl.BlockSpec((tc, w), lambda bi, ci: (bi * nc + ci, xl_col)),
            pl.BlockSpec((tc, w), lambda bi, ci: (bi * nc + ci, xl_col + 1)),
            pl.BlockSpec((CONV_W, w), lambda bi, ci: (0, 0)),
            vec(), blocks(), vec(), blocks(), vec(), vec(),
        ],
        out_specs=[
            pl.BlockSpec((tc, w), lambda bi, ci: (bi * nc + ci, 0)),
            pl.BlockSpec((1, 1, w), lambda bi, ci: (bi, 0, 0)),
            pl.BlockSpec((1, CONV_W - 1, w), lambda bi, ci: (bi, 0, 0)),
        ],
        out_shape=[jax.ShapeDtypeStruct((b * t, w), BF16),
                   jax.ShapeDtypeStruct((b, 1, w), F32),
                   jax.ShapeDtypeStruct((b, CONV_W - 1, w), F32)],
        scratch_shapes=[pltpu.VMEM((tc + V7X_SUBLANES, w), F32),
                        pltpu.VMEM((tc, w), F32), pltpu.VMEM((tc, w), F32),
                        pltpu.VMEM((V7X_SUBLANES, w), F32)],
        compiler_params=_params(("parallel", "arbitrary"), vmem),
        name="lru_prompt",
    )(proj, proj, cw, cb.reshape(1, w), wa, ba.reshape(1, w), wx, bx.reshape(1, w), sp)


def _lru_sample_kernel(xl_ref, yl_ref, conv0_ref, h0_ref, cw_ref, cb_ref, wa_ref, ba_ref,
                       wx_ref, bx_ref, sp_ref, lo_ref, h_out_ref, conv_out_ref, *, dt):
    db = h0_ref.shape[1]
    w = h0_ref.shape[2]
    taps = CONV_W - 1
    xp = [conv0_ref[0, :, i * w:(i + 1) * w] for i in range(taps)]
    xp += [xl_ref[t * db:(t + 1) * db, :] for t in range(dt)]
    xcs = []
    for t in range(dt):
        xc = cb_ref[...] + xp[t] * cw_ref[0:1, :]
        for i in range(1, CONV_W):
            xc = xc + xp[t + i] * cw_ref[i:i + 1, :]
        xcs.append(xc)
    a, bt = _lru_gates(jnp.concatenate(xcs, axis=0), wa_ref, ba_ref, wx_ref, bx_ref, sp_ref)
    h = h0_ref[0]
    for t in range(dt):
        rows = slice(t * db, (t + 1) * db)
        h = a[rows] * h + bt[rows]
        lo_ref[rows, :] = (jax.nn.gelu(yl_ref[rows, :]) * h).astype(lo_ref.dtype)
    h_out_ref[...] = h
    for i in range(taps):
        conv_out_ref[:, i * w:(i + 1) * w] = xp[dt + i]


def _lru_sample(proj, state_conv, state_lru, e, db, dt, w, cw, cb, wa, ba, wx, bx, sp):
    rows = db * dt
    assert proj.shape[0] == rows
    rb = 0
    xl_col = (proj.shape[1] - 2 * w) // w
    taps = CONV_W - 1
    conv0 = state_conv.reshape(state_conv.shape[0], db, taps * w)
    vec = lambda: pl.BlockSpec((1, w), lambda i: (0, 0))
    blocks = lambda: pl.BlockSpec(wa.shape, lambda i: (0, 0, 0))
    vmem = 16 * rows * w * 4
    lo, h_new, conv_new = pl.pallas_call(
        functools.partial(_lru_sample_kernel, dt=dt),
        grid=(1,),
        in_specs=[
            pl.BlockSpec((rows, w), lambda i: (rb, xl_col)),
            pl.BlockSpec((rows, w), lambda i: (rb, xl_col + 1)),
            pl.BlockSpec((1, db, taps * w), lambda i: (e, 0, 0)),
            pl.BlockSpec((1, db, w), lambda i: (e, 0, 0)),
            pl.BlockSpec((CONV_W, w), lambda i: (0, 0)),
            vec(), blocks(), vec(), blocks(), vec(), vec(),
        ],
        out_specs=[
            pl.BlockSpec((rows, w), lambda i: (0, 0)),
            pl.BlockSpec((db, w), lambda i: (0, 0)),
            pl.BlockSpec((db, taps * w), lambda i: (0, 0)),
        ],
        out_shape=[jax.ShapeDtypeStruct((rows, w), BF16),
                   jax.ShapeDtypeStruct((db, w), F32),
                   jax.ShapeDtypeStruct((db, taps * w), F32)],
        compiler_params=_params(("arbitrary",), vmem),
        name="lru_sample",
    )(proj, proj, conv0, state_lru, cw, cb.reshape(1, w), wa, ba.reshape(1, w), wx,
      bx.reshape(1, w), sp)
    return lo, h_new, conv_new.reshape(db, taps, w)


def _s5_tables(a_re, a_im, b_re, b_im, c_re, c_im, d, log_dt):
    g = a_re.shape[0]
    nb = g // SSM_BLOCK_GROUPS
    dt = jnp.exp(log_dt)[:, None]
    mag = jnp.exp(a_re * dt)
    abr = mag * jnp.cos(a_im * dt)
    abi = mag * jnp.sin(a_im * dt)
    den = a_re * a_re + a_im * a_im
    nr, ni = abr - 1.0, abi
    fr = (nr * a_re + ni * a_im) / den
    fi = (ni * a_re - nr * a_im) / den
    bbr = fr[..., None] * b_re - fi[..., None] * b_im
    bbi = fr[..., None] * b_im + fi[..., None] * b_re
    per_lane_block = V7X_LANES // SSM_P
    owner = (jnp.arange(SSM_BLOCK_GROUPS) % per_lane_block)[None, :, None, None, None]
    slot = jnp.arange(per_lane_block)[None, None, None, :, None]

    def pack(gkp):
        gkp = gkp.reshape(nb, SSM_BLOCK_GROUPS, SSM_GROUP, 1, SSM_P)
        return jnp.where(owner == slot, gkp, 0.0).reshape(nb, V7X_MXU_DIM, V7X_LANES)

    p_in = jnp.stack([pack(bbr.transpose(0, 2, 1)), pack(bbi.transpose(0, 2, 1))])
    p_out = jnp.stack([pack(c_re), pack(-c_im)])
    return (abr.reshape(nb, SSM_BLOCK_STATES), abi.reshape(nb, SSM_BLOCK_STATES),
            p_in, p_out, d.reshape(nb, 1, V7X_MXU_DIM))


def _expand_blockdiag(packed_ref, i, w_ref):
    ln = V7X_LANES
    per_lane_block = ln // SSM_P
    w_ref[...] = jnp.zeros_like(w_ref)
    for c in range(2):
        for g in range(SSM_BLOCK_GROUPS):
            rows = slice(g * SSM_GROUP, (g + 1) * SSM_GROUP)
            col = c * SSM_BLOCK_STATES + (g // per_lane_block) * ln
            w_ref[rows, col:col + ln] = packed_ref[c, i, rows, :].astype(w_ref.dtype)


def _s5_prompt_kernel(x_ref, g_ref, ar_ref, ai_ref, pin_ref, pout_ref, d_ref, y_ref,
                      hr_out_ref, hi_out_ref, win_ref, wout_ref, u_ref, sr_ref, si_ref,
                      hr_ref, hi_ref, *, n_chunks):
    c = pl.program_id(1)
    tc = x_ref.shape[0]
    nb = win_ref.shape[0]
    ns = SSM_BLOCK_STATES
    bw = V7X_MXU_DIM
    ln = V7X_LANES
    slots = V7X_SUBLANES
    nl = sr_ref.shape[0]
    splits = ns // (nl * ln)
    blocks_per_pass = slots // splits
    pitch = S5_ROW_PITCH

    @pl.when(c == 0)
    def _():
        hr_ref[...] = jnp.zeros_like(hr_ref)
        hi_ref[...] = jnp.zeros_like(hi_ref)
        for i in range(nb):
            _expand_blockdiag(pin_ref, i, win_ref.at[i])
            _expand_blockdiag(pout_ref, i, wout_ref.at[i])

    u_ref[...] = _rms(x_ref[...], g_ref[...])

    def slot_rows(slot):
        return pl.ds(slot, tc, stride=pitch)

    for p in range(nb // blocks_per_pass):
        blocks = range(p * blocks_per_pass, (p + 1) * blocks_per_pass)
        for il, i in enumerate(blocks):
            bu = _bdot(u_ref[:, i * bw:(i + 1) * bw].astype(BF16), win_ref[i])
            for sp in range(splits):
                for l in range(nl):
                    col = (sp * nl + l) * ln
                    sr_ref[l, slot_rows(il * splits + sp), :] = bu[:, col:col + ln]
                    si_ref[l, slot_rows(il * splits + sp), :] = bu[:, ns + col:ns + col + ln]

        srows = slice(p * slots, (p + 1) * slots)
        ar = [ar_ref[srows, l * ln:(l + 1) * ln] for l in range(nl)]
        ai = [ai_ref[srows, l * ln:(l + 1) * ln] for l in range(nl)]

        def body(t, carry):
            rows = pl.ds(t * pitch, slots)
            out = []
            for l in range(nl):
                hr, hi = carry[l]
                hr_n = ar[l] * hr - ai[l] * hi + sr_ref[l, rows, :]
                hi_n = ar[l] * hi + ai[l] * hr + si_ref[l, rows, :]
                sr_ref[l, rows, :] = hr_n
                si_ref[l, rows, :] = hi_n
                out.append((hr_n, hi_n))
            return tuple(out)

        init = tuple((hr_ref[srows, l * ln:(l + 1) * ln], hi_ref[srows, l * ln:(l + 1) * ln])
                     for l in range(nl))
        last = lax.fori_loop(0, tc, body, init, unroll=8)
        for l in range(nl):
            hr_ref[srows, l * ln:(l + 1) * ln] = last[l][0]
            hi_ref[srows, l * ln:(l + 1) * ln] = last[l][1]

        for il, i in enumerate(blocks):
            parts = [ref[l, slot_rows(il * splits + sp), :].astype(BF16)
                     for ref in (sr_ref, si_ref) for sp in range(splits) for l in range(nl)]
            cols = slice(i * bw, (i + 1) * bw)
            y = _bdot_nt(jnp.concatenate(parts, axis=-1), wout_ref[i]) + d_ref[i] * u_ref[:, cols]
            y_ref[:, cols] = jax.nn.gelu(y).astype(y_ref.dtype)

    @pl.when(c == n_chunks - 1)
    def _():
        hr_out_ref[0] = hr_ref[...]
        hi_out_ref[0] = hi_ref[...]


def _s5_prompt(x, g, tables, b, t):
    d = x.shape[1]
    abr, abi, p_in, p_out, dd = tables
    nb = abr.shape[0]
    ns = SSM_BLOCK_STATES
    w_shape = (nb, V7X_MXU_DIM, 2 * ns)
    tc = _pick_tile(t, 256, 16)
    nc = t // tc
    passes = 2
    assert nb % passes == 0 and V7X_SUBLANES % (nb // passes) == 0
    splits = V7X_SUBLANES // (nb // passes)
    slot_lanes = ns // splits
    nl = slot_lanes // V7X_LANES
    abr = abr.reshape(nb * splits, slot_lanes)
    abi = abi.reshape(nb * splits, slot_lanes)
    const = lambda a: _once(a.shape, lambda bi, ci: (0,) * a.ndim)
    scan_bytes = nl * tc * S5_ROW_PITCH * V7X_LANES * 4
    w_bytes = w_shape[0] * w_shape[1] * w_shape[2] * 2
    vmem = (5 * tc * d * 4 + 2 * tc * d * 2 + 2 * w_bytes + 2 * p_in.size * 4
            + 2 * scan_bytes + 6 * tc * 2 * ns * 4)
    y, hr, hi = pl.pallas_call(
        functools.partial(_s5_prompt_kernel, n_chunks=nc),
        grid=(b, nc),
        in_specs=[
            pl.BlockSpec((tc, d), lambda bi, ci: (bi * nc + ci, 0)),
            pl.BlockSpec((1, d), lambda bi, ci: (0, 0)),
            const(abr), const(abi), const(p_in), const(p_out), const(dd),
        ],
        out_specs=[
            pl.BlockSpec((tc, d), lambda bi, ci: (bi * nc + ci, 0)),
            pl.BlockSpec((1,) + abr.shape, lambda bi, ci: (bi, 0, 0)),
            pl.BlockSpec((1,) + abr.shape, lambda bi, ci: (bi, 0, 0)),
        ],
        out_shape=[jax.ShapeDtypeStruct((b * t, d), BF16),
                   jax.ShapeDtypeStruct((b,) + abr.shape, F32),
                   jax.ShapeDtypeStruct((b,) + abr.shape, F32)],
        scratch_shapes=[pltpu.VMEM(w_shape, BF16), pltpu.VMEM(w_shape, BF16),
                        pltpu.VMEM((tc, d), F32),
                        pltpu.VMEM((nl, tc * S5_ROW_PITCH, V7X_LANES), F32),
                        pltpu.VMEM((nl, tc * S5_ROW_PITCH, V7X_LANES), F32),
                        pltpu.VMEM(abr.shape, F32), pltpu.VMEM(abr.shape, F32)],
        compiler_params=_params(("parallel", "arbitrary"), vmem),
        name="s5_prompt",
    )(x, g.reshape(1, d), abr, abi, p_in, p_out, dd)
    return y, hr.reshape(b, nb * ns), hi.reshape(b, nb * ns)


def _s5_sample_kernel(x_ref, g_ref, ar_ref, ai_ref, pin_ref, pout_ref, d_ref, h0r_ref,
                      h0i_ref, y_ref, hr_out_ref, hi_out_ref, win_ref, wout_ref, uf_ref,
                      ub_ref, *, dt):
    i = pl.program_id(0)
    nb = uf_ref.shape[0]
    bw = V7X_MXU_DIM
    ns = SSM_BLOCK_STATES
    db = h0r_ref.shape[1]

    @pl.when(i == 0)
    def _():
        u = _rms(x_ref[...], g_ref[...])
        for n in range(nb):
            uf_ref[n] = u[:, n * bw:(n + 1) * bw]
            ub_ref[n] = u[:, n * bw:(n + 1) * bw].astype(BF16)

    _expand_blockdiag(pin_ref, 0, win_ref)
    _expand_blockdiag(pout_ref, 0, wout_ref)
    bu = _bdot(ub_ref[i], win_ref[...])
    ar, ai = ar_ref[pl.ds(i, 1), :], ai_ref[pl.ds(i, 1), :]
    hr, hi = h0r_ref[0], h0i_ref[0]
    states = []
    for t in range(dt):
        rows = slice(t * db, (t + 1) * db)
        hr, hi = (ar * hr - ai * hi + bu[rows, :ns], ar * hi + ai * hr + bu[rows, ns:])
        states.append(jnp.concatenate([hr.astype(BF16), hi.astype(BF16)], axis=-1))
    y = _bdot_nt(jnp.concatenate(states, axis=0), wout_ref[...]) + d_ref[0] * uf_ref[i]
    y_ref[...] = jax.nn.gelu(y).astype(y_ref.dtype)
    hr_out_ref[...] = hr
    hi_out_ref[...] = hi


def _s5_sample(x, g, tables, state_re, state_im, o, db, dt):
    d = x.shape[1]
    abr, abi, p_in, p_out, dd = tables
    nb = abr.shape[0]
    ns = SSM_BLOCK_STATES
    bw = V7X_MXU_DIM
    rows = db * dt
    assert x.shape[0] == rows
    rb = 0
    no = state_re.shape[0]
    h0r = state_re.reshape(no, db, nb * ns)
    h0i = state_im.reshape(no, db, nb * ns)
    blk = lambda a: pl.BlockSpec((1,) + a.shape[1:], lambda i: (i, 0, 0))
    packed = lambda a: pl.BlockSpec((2, 1) + a.shape[2:], lambda i: (0, i, 0, 0))
    vmem = (2 * rows * d * 4 + rows * d * 6 + 4 * bw * 2 * ns * 2 + 8 * db * ns * 4
            + 8 * rows * 2 * ns * 4)
    return pl.pallas_call(
        functools.partial(_s5_sample_kernel, dt=dt),
        grid=(nb,),
        in_specs=[
            _once((rows, d), lambda i: (rb, 0)),
            pl.BlockSpec((1, d), lambda i: (0, 0)),
            pl.BlockSpec(abr.shape, lambda i: (0, 0)),
            pl.BlockSpec(abi.shape, lambda i: (0, 0)),
            packed(p_in), packed(p_out), blk(dd),
            pl.BlockSpec((1, db, ns), lambda i: (o, 0, i)),
            pl.BlockSpec((1, db, ns), lambda i: (o, 0, i)),
        ],
        out_specs=[
            pl.BlockSpec((rows, bw), lambda i: (0, i)),
            pl.BlockSpec((db, ns), lambda i: (0, i)),
            pl.BlockSpec((db, ns), lambda i: (0, i)),
        ],
        out_shape=[jax.ShapeDtypeStruct((rows, d), BF16),
                   jax.ShapeDtypeStruct((db, nb * ns), F32),
                   jax.ShapeDtypeStruct((db, nb * ns), F32)],
        scratch_shapes=[pltpu.VMEM((bw, 2 * ns), BF16), pltpu.VMEM((bw, 2 * ns), BF16),
                        pltpu.VMEM((nb, rows, bw), F32), pltpu.VMEM((nb, rows, bw), BF16)],
        compiler_params=_params(("arbitrary",), vmem),
        name="s5_sample",
    )(x, g.reshape(1, d), abr, abi, p_in, p_out, dd, h0r, h0i)


def kernel(x_prompt, x_sample, state_ret, state_lru, state_conv, state_ssm_re, state_ssm_im, norm_mix_even, w_in_even, lru_conv_w, lru_conv_b, lru_wa, lru_ba, lru_wx, lru_bx, lru_lambda, w_out_even, norm_mix_odd, ssm_a_re, ssm_a_im, ssm_b_re, ssm_b_im, ssm_c_re, ssm_c_im, ssm_d, ssm_log_dt, w_glu, b_glu, norm_ffn, w_ffn_gu, w_ffn_down, norm_final):
    b, t, d = x_prompt.shape
    db, dt, _ = x_sample.shape
    depth = norm_ffn.shape[0]
    half = d // 2
    past_len = PAST_LEN
    groups, ssm_p = ssm_a_re.shape[1:]
    assert ssm_p == SSM_P and groups * SSM_GROUP == d and groups % SSM_BLOCK_GROUPS == 0

    xp = x_prompt.reshape(b * t, d)
    xs = x_sample.transpose(1, 0, 2).reshape(dt * db, d)

    rets_p, ret_s, lrus_p, lrus_s, convs_p, convs_s = [], None, [], [], [], []
    sres_p, sres_s, sims_p, sims_s = [], [], [], []
    for layer in range(depth):
        if layer % 2 == 0:
            e = layer // 2
            sp = jax.nn.softplus(-lru_lambda[e]).reshape(1, half)
            lru_w = (lru_conv_w[e], lru_conv_b[e], lru_wa[e], lru_ba[e], lru_wx[e], lru_bx[e], sp)

            proj_p = _inproj(xp, norm_mix_even[e], w_in_even, e)
            o_p, ret_p = _ret_prompt(proj_p, b, t, half)
            lo_p, lru_p, conv_p = _lru_prompt(proj_p, b, t, half, *lru_w)
            xp = _outproj(o_p, lo_p, w_out_even, e, xp)

            proj_s = _inproj(xs, norm_mix_even[e], w_in_even, e)
            o_s, ret_s = _ret_sample(proj_s, state_ret, e, ret_s, db, dt, half, past_len)
            lo_s, lru_s, conv_s = _lru_sample(proj_s, state_conv, state_lru, e, db, dt, half,
                                              *lru_w)
            xs = _outproj(o_s, lo_s, w_out_even, e, xs)

            rets_p.append(ret_p)
            lrus_p.append(lru_p.reshape(b, half))
            lrus_s.append(lru_s)
            convs_p.append(conv_p)
            convs_s.append(conv_s)
        else:
            o = layer // 2
            tables = _s5_tables(ssm_a_re[o], ssm_a_im[o], ssm_b_re[o], ssm_b_im[o],
                                ssm_c_re[o], ssm_c_im[o], ssm_d[o], ssm_log_dt[o])
            y_p, sre_p, sim_p = _s5_prompt(xp, norm_mix_odd[o], tables, b, t)
            xp = _glu(y_p, w_glu, o, b_glu[o], xp)
            y_s, sre_s, sim_s = _s5_sample(xs, norm_mix_odd[o], tables, state_ssm_re,
                                           state_ssm_im, o, db, dt)
            xs = _glu(y_s, w_glu, o, b_glu[o], xs)
            sres_p.append(sre_p.reshape(b, groups, ssm_p))
            sims_p.append(sim_p.reshape(b, groups, ssm_p))
            sres_s.append(sre_s.reshape(db, groups, ssm_p))
            sims_s.append(sim_s.reshape(db, groups, ssm_p))
        g_final = norm_final if layer == depth - 1 else None
        xp = _ffn(xp, norm_ffn[layer], w_ffn_gu, w_ffn_down, layer, g_final)
        xs = _ffn(xs, norm_ffn[layer], w_ffn_gu, w_ffn_down, layer, g_final)

    y_prompt = xp.reshape(b, t, d)
    y_sample = xs.reshape(dt, db, d).transpose(1, 0, 2)
    return (y_prompt, y_sample, jnp.stack(rets_p), ret_s, jnp.stack(lrus_p),
            jnp.stack(lrus_s), jnp.stack(convs_p), jnp.stack(convs_s), jnp.stack(sres_p),
            jnp.stack(sres_s), jnp.stack(sims_p), jnp.stack(sims_s))
```

```python
import functools

import jax
import jax.numpy as jnp
import numpy as np
from jax import lax
from jax.experimental import pallas as pl
from jax.experimental.pallas import tpu as pltpu

F32 = jnp.float32
BF16 = jnp.bfloat16

EPS = 1e-6
PAST_LEN = 16384
ROPE_BASE = 10000.0
RET_HEADS = 4
RET_CHUNK = 128
LRU_HEADS = 8
LRU_C = 8.0
CONV_W = 4
SSM_GROUP = 16
SSM_P = 64
SSM_CHUNK = 128

V7X_SUBLANES = 8
V7X_LANES = 128
V7X_MXU_DIM = 256
V7X_VMEM_BYTES = 64 * 1024 * 1024
VMEM_LIMIT_CAP = V7X_VMEM_BYTES - 6 * 1024 * 1024

DENSE_ROW_TILE = 2048

TILE_COPY_CHUNKS = 8

SSM_BLOCK_GROUPS = V7X_MXU_DIM // SSM_GROUP
SSM_BLOCK_STATES = SSM_BLOCK_GROUPS * SSM_P

RET_CHUNKS_PER_STEP = 4

RET_SAMPLE_ROWS = 128

S5_ROW_PITCH = 12


def _pick_tile(n, target, mult):
    best = None
    for t in range(mult, min(n, target) + 1, mult):
        if n % t == 0:
            best = t
    assert best is not None, (n, target, mult)
    return best


def _params(semantics, vmem_bytes):
    limit = min(int(vmem_bytes * 1.2) + (6 << 20), VMEM_LIMIT_CAP)
    return pltpu.CompilerParams(dimension_semantics=semantics, vmem_limit_bytes=limit)


def _once(block_shape, index_map):
    return pl.BlockSpec(block_shape, index_map, pipeline_mode=pl.Buffered(1))


def _rms(x, g):
    return x * lax.rsqrt(jnp.mean(x * x, axis=-1, keepdims=True) + EPS) * g


def _norm_rows_into(x_ref, g_ref, xn_ref):
    tm = x_ref.shape[0]
    rows = _pick_tile(tm, 64, 16)

    def body(r, carry):
        sl = pl.ds(pl.multiple_of(r * rows, rows), rows)
        xn_ref[sl, :] = _rms(x_ref[sl, :], g_ref[...]).astype(xn_ref.dtype)
        return carry

    lax.fori_loop(0, tm // rows, body, 0)


def _row_chunk_copies(src, dst, sems):
    n = sems.shape[0]
    rc = src.shape[0] // n
    return [pltpu.make_async_copy(src.at[pl.ds(c * rc, rc), :], dst.at[pl.ds(c * rc, rc), :],
                                  sems.at[c]) for c in range(n)]


def _load_and_norm_rows(x_hbm_rows, x_ref, g_ref, xn_ref, sems):
    copies = _row_chunk_copies(x_hbm_rows, x_ref, sems)
    rc = x_ref.shape[0] // len(copies)
    for cp in copies:
        cp.start()
    for c, cp in enumerate(copies):
        cp.wait()
        rows = pl.ds(c * rc, rc)
        _norm_rows_into(x_ref.at[rows, :], g_ref, xn_ref.at[rows, :])


def _bdot(a, b):
    return jnp.dot(a, b, preferred_element_type=F32)


def _bdot_nt(a, b):
    return lax.dot_general(a, b, (((1,), (1,)), ((), ())), preferred_element_type=F32)


def _inproj_kernel(x_hbm, g_ref, w_ref, o_ref, x_ref, xn_ref, x_sems):
    @pl.when(pl.program_id(1) == 0)
    def _():
        tm = x_ref.shape[0]
        tile = pl.ds(pl.multiple_of(pl.program_id(0) * tm, tm), tm)
        _load_and_norm_rows(x_hbm.at[tile, :], x_ref, g_ref, xn_ref, x_sems)

    o_ref[...] = _bdot(xn_ref[...], w_ref[...].astype(BF16))


def _inproj(x, g, w, e):
    m, d = x.shape
    n = w.shape[2]
    tm = _pick_tile(m, DENSE_ROW_TILE, 16)
    tn = _pick_tile(n, 512, V7X_LANES)
    vmem = tm * d * 4 + tm * d * 2 + 2 * d * tn * 4 + 2 * tm * tn * 4 + d * tn * 2
    return pl.pallas_call(
        _inproj_kernel,
        grid=(m // tm, n // tn),
        in_specs=[
            pl.BlockSpec(memory_space=pl.ANY),
            _once((1, d), lambda i, j: (0, 0)),
            pl.BlockSpec((None, d, tn), lambda i, j: (e, 0, j)),
        ],
        out_specs=pl.BlockSpec((tm, tn), lambda i, j: (i, j)),
        out_shape=jax.ShapeDtypeStruct((m, n), F32),
        scratch_shapes=[pltpu.VMEM((tm, d), F32), pltpu.VMEM((tm, d), BF16),
                        pltpu.SemaphoreType.DMA((TILE_COPY_CHUNKS,))],
        compiler_params=_params(("parallel", "arbitrary"), vmem),
        name="inproj",
    )(x, g.reshape(1, d), w)


def _outproj_kernel(o_ref, lo_ref, wo_ref, wl_ref, x_ref, out_ref):
    acc = _bdot(o_ref[...], wo_ref[...].astype(BF16))
    acc = acc + _bdot(lo_ref[...], wl_ref[...].astype(BF16))
    out_ref[...] = x_ref[...] + acc


def _outproj(o, lo, w, e, x):
    m, d = x.shape
    half = o.shape[1]
    tm = _pick_tile(m, DENSE_ROW_TILE, 16)
    tn = _pick_tile(d, 512, V7X_LANES)
    vmem = 4 * tm * half * 2 + 4 * half * tn * 4 + 4 * tm * tn * 4 + 2 * half * tn * 2
    return pl.pallas_call(
        _outproj_kernel,
        grid=(m // tm, d // tn),
        in_specs=[
            pl.BlockSpec((tm, half), lambda i, j: (i, 0)),
            pl.BlockSpec((tm, half), lambda i, j: (i, 0)),
            pl.BlockSpec((None, half, tn), lambda i, j: (e, 0, j)),
            pl.BlockSpec((None, half, tn), lambda i, j: (e, 1, j)),
            pl.BlockSpec((tm, tn), lambda i, j: (i, j)),
        ],
        out_specs=pl.BlockSpec((tm, tn), lambda i, j: (i, j)),
        out_shape=jax.ShapeDtypeStruct((m, d), F32),
        compiler_params=_params(("parallel", "arbitrary"), vmem),
        name="outproj",
    )(o, lo, w, w, x)


def _glu_kernel(y_ref, w1_ref, w2_ref, b1_ref, b2_ref, x_ref, out_ref):
    y = y_ref[...]
    z1 = _bdot(y, w1_ref[...].astype(BF16)) + b1_ref[...]
    z2 = _bdot(y, w2_ref[...].astype(BF16)) + b2_ref[...]
    out_ref[...] = x_ref[...] + z1 * jax.nn.sigmoid(z2)


def _glu(y, w, o, b, x):
    m, d = x.shape
    tm = _pick_tile(m, DENSE_ROW_TILE, 16)

    tn = _pick_tile(d, 256, V7X_LANES)
    nj = d // tn
    vmem = 2 * tm * d * 2 + 4 * d * tn * 4 + 4 * tm * tn * 4 + 2 * d * tn * 2 + 2 * tm * tn * 4
    return pl.pallas_call(
        _glu_kernel,
        grid=(m // tm, nj),
        in_specs=[
            pl.BlockSpec((tm, d), lambda i, j: (i, 0)),
            pl.BlockSpec((None, d, tn), lambda i, j: (o, 0, j)),
            pl.BlockSpec((None, d, tn), lambda i, j: (o, 0, nj + j)),
            pl.BlockSpec((1, tn), lambda i, j: (0, j)),
            pl.BlockSpec((1, tn), lambda i, j: (0, nj + j)),
            pl.BlockSpec((tm, tn), lambda i, j: (i, j)),
        ],
        out_specs=pl.BlockSpec((tm, tn), lambda i, j: (i, j)),
        out_shape=jax.ShapeDtypeStruct((m, d), F32),
        compiler_params=_params(("parallel", "arbitrary"), vmem),
        name="glu",
    )(y, w, w, b.reshape(1, 2 * d), b.reshape(1, 2 * d), x)


def _ffn_kernel(x_hbm, g_ref, gf_ref, wg_ref, wu_ref, wd_ref, out_ref, xn_ref, x_sems, *,
                final_norm):
    i, j = pl.program_id(0), pl.program_id(1)

    @pl.when(j == 0)
    def _():
        tm = out_ref.shape[0]
        tile = pl.ds(pl.multiple_of(i * tm, tm), tm)
        _load_and_norm_rows(x_hbm.at[tile, :], out_ref, g_ref, xn_ref, x_sems)

    xn = xn_ref[...]
    gate = _bdot(xn, wg_ref[...].astype(BF16))
    up = _bdot(xn, wu_ref[...].astype(BF16))
    act = (gate * jax.nn.sigmoid(gate) * up).astype(BF16)
    out_ref[...] += _bdot(act, wd_ref[...].astype(BF16))

    if final_norm:
        @pl.when(j == pl.num_programs(1) - 1)
        def _():
            _norm_rows_into(out_ref, gf_ref, out_ref)


def _ffn(x, g, w_gu, w_down, layer, g_final=None):
    m, d = x.shape
    hidden = w_down.shape[1]
    tm = _pick_tile(m, DENSE_ROW_TILE, 16)

    th = _pick_tile(hidden, 256, V7X_LANES)
    nh = hidden // th
    vmem = (tm * d * 4 + tm * d * 2 + 6 * d * th * 4 + 3 * d * th * 2
            + 3 * tm * th * 4)
    gf = g if g_final is None else g_final
    return pl.pallas_call(
        functools.partial(_ffn_kernel, final_norm=g_final is not None),
        grid=(m // tm, nh),
        in_specs=[
            pl.BlockSpec(memory_space=pl.ANY),
            _once((1, d), lambda i, j: (0, 0)),
            _once((1, d), lambda i, j: (0, 0)),
            pl.BlockSpec((None, d, th), lambda i, j: (layer, 0, j)),
            pl.BlockSpec((None, d, th), lambda i, j: (layer, 0, nh + j)),
            pl.BlockSpec((None, th, d), lambda i, j: (layer, j, 0)),
        ],
        out_specs=_once((tm, d), lambda i, j: (i, 0)),
        out_shape=jax.ShapeDtypeStruct((m, d), F32),
        scratch_shapes=[pltpu.VMEM((tm, d), BF16),
                        pltpu.SemaphoreType.DMA((TILE_COPY_CHUNKS,))],
        compiler_params=_params(("parallel", "arbitrary"), vmem),
        name="ffn",
    )(x, g.reshape(1, d), gf.reshape(1, d), w_gu, w_gu, w_down)


def _rope_tables(pos, half):
    inv = 1.0 / np.power(ROPE_BASE, np.linspace(0.0, 1.0, half))
    ang = np.asarray(pos, np.float64)[:, None] * inv[None, :]
    return np.cos(ang).astype(np.float32), np.sin(ang).astype(np.float32)


def _decay_tables(c, dk):
    log_g = np.log1p(-np.exp2(-5.0 - np.arange(RET_HEADS, dtype=np.float64)))
    idx = np.arange(c, dtype=np.float64)
    diff = idx[:, None] - idx[None, :]
    mask = np.where(diff[None] >= 0,
                    np.exp(np.maximum(diff, 0.0)[None] * log_g[:, None, None]), 0.0)
    q_dec = np.exp((idx[None, :] + 1.0) * log_g[:, None])
    k_dec = np.exp((c - 1.0 - idx)[None, :] * log_g[:, None])
    chunk_dec = np.exp(c * log_g)
    q_dec = np.broadcast_to(q_dec[:, :, None], (RET_HEADS, c, dk))
    k_dec = np.broadcast_to(k_dec[:, :, None], (RET_HEADS, c, dk))
    chunk_dec = np.broadcast_to(chunk_dec[:, None, None], (RET_HEADS, 1, dk))
    f32 = lambda a: np.ascontiguousarray(a, dtype=np.float32)
    return f32(mask), f32(q_dec), f32(k_dec), f32(chunk_dec)


def _rotate(x, cos, sin):
    half = x.shape[-1] // 2
    x1, x2 = x[:, :half], x[:, half:]
    return jnp.concatenate([x1 * cos - x2 * sin, x2 * cos + x1 * sin], axis=-1)


def _ret_prompt_kernel(q_ref, k_ref, v_ref, g_ref, cos_ref, sin_ref, mask_ref, qd_ref,
                       kd_ref, cd_ref, o_ref, s_out_ref, s_ref, *, n_chunks, scale):
    c = pl.program_id(1)
    heads, dk, _ = s_ref.shape

    @pl.when(c == 0)
    def _():
        s_ref[...] = jnp.zeros_like(s_ref)

    chunk = mask_ref.shape[1]
    for h in range(heads):
        cols = slice(h * dk, (h + 1) * dk)
        s = s_ref[h]
        for cc in range(q_ref.shape[0] // chunk):
            rows = slice(cc * chunk, (cc + 1) * chunk)
            cos, sin = cos_ref[rows, :], sin_ref[rows, :]
            q = _rotate(q_ref[rows, cols], cos, sin)
            k = _rotate(k_ref[rows, cols], cos, sin) * scale
            qb, kb, vb = q.astype(BF16), k.astype(BF16), v_ref[rows, cols].astype(BF16)
            scores = lax.dot_general(qb, kb, (((1,), (1,)), ((), ())),
                                     preferred_element_type=F32) * mask_ref[h]
            o = _bdot(scores.astype(BF16), vb) + _bdot(qb, s.astype(BF16)) * qd_ref[h]
            kdb = (k * kd_ref[h]).astype(BF16)
            s = s * cd_ref[h] + lax.dot_general(kdb, vb, (((0,), (0,)), ((), ())),
                                                preferred_element_type=F32)
            o = o * lax.rsqrt(jnp.mean(o * o, axis=-1, keepdims=True) + EPS)
            gr = g_ref[rows, cols]
            o_ref[rows, cols] = (o * (gr * jax.nn.sigmoid(gr))).astype(o_ref.dtype)
        s_ref[h] = s

    @pl.when(c == n_chunks - 1)
    def _():
        s_out_ref[0] = s_ref[...]


def _ret_prompt(proj, b, t, half):
    h = RET_HEADS
    dk = half // h
    assert t % RET_CHUNK == 0
    c = _pick_tile(t, RET_CHUNKS_PER_STEP * RET_CHUNK, RET_CHUNK)
    nc = t // c
    cos, sin = _rope_tables(np.arange(t), dk // 2)
    tables = _decay_tables(RET_CHUNK, dk)

    def col(off):
        return pl.BlockSpec((c, half), lambda bi, ci: (bi * nc + ci, off))

    const = lambda a: pl.BlockSpec(a.shape, lambda bi, ci: (0, 0, 0))
    vmem = 10 * c * half * 4 + 3 * h * dk * dk * 4 + sum(2 * a.size * 4 for a in tables)
    return pl.pallas_call(
        functools.partial(_ret_prompt_kernel, n_chunks=nc, scale=dk ** -0.5),
        grid=(b, nc),
        in_specs=[
            col(0), col(1), col(2), col(3),
            pl.BlockSpec((c, dk // 2), lambda bi, ci: (ci, 0)),
            pl.BlockSpec((c, dk // 2), lambda bi, ci: (ci, 0)),
            *[const(a) for a in tables],
        ],
        out_specs=[
            pl.BlockSpec((c, half), lambda bi, ci: (bi * nc + ci, 0)),
            pl.BlockSpec((1, h, dk, dk), lambda bi, ci: (bi, 0, 0, 0)),
        ],
        out_shape=[jax.ShapeDtypeStruct((b * t, half), BF16),
                   jax.ShapeDtypeStruct((b, h, dk, dk), F32)],
        scratch_shapes=[pltpu.VMEM((h, dk, dk), F32)],
        compiler_params=_params(("parallel", "arbitrary"), vmem),
        name="ret_prompt",
    )(proj, proj, proj, proj, cos, sin, *tables)


def _ret_sample_kernel(*refs, bb, scale, chained):
    (q_ref, k_ref, v_ref, g_ref, cos_ref, sin_ref, mask_ref, qd_ref, kd_ref, cd_ref,
     s_in_ref) = refs[:11]
    o_ref, s_out_ref, kdt_ref, acc_ref = refs[12 if chained else 11:]

    for later in range(1, s_out_ref.shape[0]):
        s_out_ref[later] = jnp.zeros(s_out_ref.shape[1:], s_out_ref.dtype)
    dt, _, dk = q_ref.shape
    rows = dt * bb
    cos, sin = cos_ref[...], sin_ref[...]
    q = _rotate(q_ref[...].reshape(rows, dk), cos, sin)
    k = _rotate(k_ref[...].reshape(rows, dk), cos, sin) * scale
    qb, kb = q.astype(BF16), k.astype(BF16)
    vb = v_ref[...].reshape(rows, dk).astype(BF16)
    scores = lax.dot_general(qb, kb, (((1,), (1,)), ((), ())),
                             preferred_element_type=F32) * mask_ref[0]
    intra = _bdot(scores.astype(BF16), vb)
    kdt_ref[...] = (k * kd_ref[0]).T
    cd = cd_ref[0]
    row_batch = lax.broadcasted_iota(jnp.int32, (rows, 1), 0) % bb
    col_batch = lax.broadcasted_iota(jnp.int32, (1, rows), 1) % bb

    acc_ref[...] = jnp.zeros_like(acc_ref)

    def body(j, carry):
        s = s_in_ref[0, j, 0]
        acc_ref[...] = jnp.where(row_batch == j, _bdot(qb, s.astype(BF16)), acc_ref[...])
        kdt_j = jnp.where(col_batch == j, kdt_ref[...], 0.0).astype(BF16)
        s_out_ref[0, j, 0] = s * cd + _bdot(kdt_j, vb)
        return carry

    lax.fori_loop(0, bb, body, 0, unroll=4)
    o = intra + acc_ref[...] * qd_ref[0]
    o = o * lax.rsqrt(jnp.mean(o * o, axis=-1, keepdims=True) + EPS)
    gr = g_ref[...].reshape(rows, dk)
    o_ref[...] = (o * (gr * jax.nn.sigmoid(gr))).astype(o_ref.dtype).reshape(dt, bb, dk)


def _ret_sample(proj, state_ret, e, prev_states, db, dt, half, past_len):
    dk = half // RET_HEADS
    h = RET_HEADS
    assert RET_CHUNK % dt == 0 and RET_SAMPLE_ROWS % dt == 0
    rows = RET_SAMPLE_ROWS
    bb = rows // dt
    assert db % bb == 0
    proj3 = proj.reshape(dt, db, proj.shape[1])
    cos, sin = _rope_tables(past_len + np.arange(dt), dk // 2)
    cos, sin = np.repeat(cos, bb, axis=0), np.repeat(sin, bb, axis=0)
    mask, q_dec, k_dec, chunk_dec = _decay_tables(dt, dk)
    mask = np.einsum("hnm,ab->hnamb", mask, np.eye(bb, dtype=np.float32)).reshape(h, rows, rows)
    q_dec, k_dec = np.repeat(q_dec, bb, axis=1), np.repeat(k_dec, bb, axis=1)

    def col(off):
        return pl.BlockSpec((dt, bb, dk), lambda bi, hi: (0, bi, off + hi))

    def per_head(shape):
        return pl.BlockSpec((1,) + shape, lambda bi, hi: (hi, 0, 0))

    chained = prev_states is not None
    if chained:
        out_state_block = pl.BlockSpec((1, bb, 1, dk, dk), lambda bi, hi: (e, bi, hi, 0, 0))
    else:
        assert e == 0
        n_layers = state_ret.shape[0]
        out_state_block = pl.BlockSpec((n_layers, bb, 1, dk, dk),
                                       lambda bi, hi: (0, bi, hi, 0, 0))
    operands = [proj3, proj3, proj3, proj3, cos, sin, mask, q_dec, k_dec, chunk_dec, state_ret]
    in_specs = [
        col(0), col(h), col(2 * h), col(3 * h),
        pl.BlockSpec((rows, dk // 2), lambda bi, hi: (0, 0)),
        pl.BlockSpec((rows, dk // 2), lambda bi, hi: (0, 0)),
        per_head((rows, rows)), per_head((rows, dk)), per_head((rows, dk)),
        per_head((1, dk)),
        pl.BlockSpec((1, bb, 1, dk, dk), lambda bi, hi: (e, bi, hi, 0, 0)),
    ]
    if chained:
        operands.append(prev_states)
        in_specs.append(pl.BlockSpec(memory_space=pl.ANY))
    out_layers = out_state_block.block_shape[0]
    vmem = (2 + 2 * out_layers) * bb * dk * dk * 4 + 12 * rows * dk * 4 + 4 * dk * dk * 4
    o, states = pl.pallas_call(
        functools.partial(_ret_sample_kernel, bb=bb, scale=dk ** -0.5, chained=chained),
        grid=(db // bb, h),
        in_specs=in_specs,
        out_specs=[
            pl.BlockSpec((dt, bb, dk), lambda bi, hi: (0, bi, hi)),
            out_state_block,
        ],
        out_shape=[jax.ShapeDtypeStruct((dt, db, half), BF16),
                   jax.ShapeDtypeStruct(state_ret.shape, state_ret.dtype)],
        scratch_shapes=[pltpu.VMEM((dk, rows), F32), pltpu.VMEM((rows, dk), F32)],
        input_output_aliases={len(operands) - 1: 1} if chained else {},
        compiler_params=_params(("parallel", "parallel"), vmem),
        name="ret_sample",
    )(*operands)
    return o.reshape(dt * db, half), states


def _lru_gates(xc, wa_ref, ba_ref, wx_ref, bx_ref, sp_ref):
    xcb = xc.astype(BF16)
    blk = wa_ref.shape[1]
    ra, ri = [], []
    for n in range(wa_ref.shape[0]):
        xn = xcb[:, n * blk:(n + 1) * blk]
        ra.append(_bdot(xn, wa_ref[n].astype(BF16)))
        ri.append(_bdot(xn, wx_ref[n].astype(BF16)))
    r = jax.nn.sigmoid(jnp.concatenate(ra, axis=-1) + ba_ref[...])
    i = jax.nn.sigmoid(jnp.concatenate(ri, axis=-1) + bx_ref[...])
    log_a = -LRU_C * r * sp_ref[...]
    a = jnp.exp(log_a)
    gap = jnp.maximum(-jnp.tanh(log_a) * (a * a + 1.0), 0.0)
    mult = jnp.where(gap > 0.0, gap * lax.rsqrt(gap), 0.0)
    return a, mult * i * xc


def _lru_prompt_kernel(xl_ref, yl_ref, cw_ref, cb_ref, wa_ref, ba_ref, wx_ref, bx_ref,
                       sp_ref, lo_ref, h_out_ref, conv_out_ref, xs_ref, a_ref, b_ref,
                       hc_ref, *, n_chunks):
    c = pl.program_id(1)
    tc, w = xl_ref.shape
    sub = V7X_SUBLANES

    @pl.when(c == 0)
    def _():
        xs_ref[0:sub, :] = jnp.zeros((sub, w), F32)
        hc_ref[...] = jnp.zeros_like(hc_ref)

    x = xl_ref[...]
    xs_ref[sub:sub + tc, :] = x
    xc = cb_ref[...] + x * cw_ref[CONV_W - 1:CONV_W, :]
    for i in range(CONV_W - 1):
        back = CONV_W - 1 - i
        xc = xc + xs_ref[sub - back:sub - back + tc, :] * cw_ref[i:i + 1, :]
    xs_ref[0:sub, :] = xs_ref[tc:tc + sub, :]

    a, bt = _lru_gates(xc, wa_ref, ba_ref, wx_ref, bx_ref, sp_ref)

    a3 = a.reshape(tc // sub, sub, w)
    b3 = bt.reshape(tc // sub, sub, w)
    step = lax.broadcasted_iota(jnp.int32, (1, sub, 1), 1)
    for s in (1, 2, 4):
        keep = step >= s
        a_prev = jnp.where(keep, pltpu.roll(a3, s, axis=1), 1.0)
        b_prev = jnp.where(keep, pltpu.roll(b3, s, axis=1), 0.0)
        b3 = a3 * b_prev + b3
        a3 = a3 * a_prev
    a_ref[...] = a3.reshape(tc, w)
    b_ref[...] = b3.reshape(tc, w)

    def body(g, h):
        sl = pl.ds(pl.multiple_of(g * sub, sub), sub)
        hg = b_ref[sl, :] + a_ref[sl, :] * h
        b_ref[sl, :] = hg
        return jnp.broadcast_to(hg[sub - 1:sub, :], (sub, w))

    h_last = lax.fori_loop(0, tc // sub, body, hc_ref[...])
    hc_ref[...] = h_last
    lo_ref[...] = (jax.nn.gelu(yl_ref[...]) * b_ref[...]).astype(lo_ref.dtype)

    @pl.when(c == n_chunks - 1)
    def _():
        h_out_ref[0] = h_last[0:1, :]
        conv_out_ref[0] = xs_ref[sub - (CONV_W - 1):sub, :]


def _lru_prompt(proj, b, t, w, cw, cb, wa, ba, wx, bx, sp):
    tc = _pick_tile(t, 256, 16)
    nc = t // tc
    xl_col = (proj.shape[1] - 2 * w) // w
    vec = lambda: pl.BlockSpec((1, w), lambda bi, ci: (0, 0))
    blocks = lambda: pl.BlockSpec(wa.shape, lambda bi, ci: (0, 0, 0))
    vmem = 4 * tc * w * 4 + 2 * tc * w * 2 + 3 * tc * w * 4 + 8 * tc * w * 4
    return pl.pallas_call(
        functools.partial(_lru_prompt_kernel, n_chunks=nc),
        grid=(b, nc),
        in_specs=[
            pl.BlockSpec((tc, w), lambda bi, ci: (bi * nc + ci, xl_col)),
            pl.BlockSpec((tc, w), lambda bi, ci: (bi * nc + ci, xl_col + 1)),
            pl.BlockSpec((CONV_W, w), lambda bi, ci: (0, 0)),
            vec(), blocks(), vec(), blocks(), vec(), vec(),
        ],
        out_specs=[
            pl.BlockSpec((tc, w), lambda bi, ci: (bi * nc + ci, 0)),
            pl.BlockSpec((1, 1, w), lambda bi, ci: (bi, 0, 0)),
            pl.BlockSpec((1, CONV_W - 1, w), lambda bi, ci: (bi, 0, 0)),
        ],
        out_shape=[jax.ShapeDtypeStruct((b * t, w), BF16),
                   jax.ShapeDtypeStruct((b, 1, w), F32),
                   jax.ShapeDtypeStruct((b, CONV_W - 1, w), F32)],
        scratch_shapes=[pltpu.VMEM((tc + V7X_SUBLANES, w), F32),
                        pltpu.VMEM((tc, w), F32), pltpu.VMEM((tc, w), F32),
                        pltpu.VMEM((V7X_SUBLANES, w), F32)],
        compiler_params=_params(("parallel", "arbitrary"), vmem),
        name="lru_prompt",
    )(proj, proj, cw, cb.reshape(1, w), wa, ba.reshape(1, w), wx, bx.reshape(1, w), sp)


def _lru_sample_kernel(xl_ref, yl_ref, conv0_ref, h0_ref, cw_ref, cb_ref, wa_ref, ba_ref,
                       wx_ref, bx_ref, sp_ref, lo_ref, h_out_ref, conv_out_ref, *, dt):
    db = h0_ref.shape[1]
    w = h0_ref.shape[2]
    taps = CONV_W - 1
    xp = [conv0_ref[0, :, i * w:(i + 1) * w] for i in range(taps)]
    xp += [xl_ref[t * db:(t + 1) * db, :] for t in range(dt)]
    xcs = []
    for t in range(dt):
        xc = cb_ref[...] + xp[t] * cw_ref[0:1, :]
        for i in range(1, CONV_W):
            xc = xc + xp[t + i] * cw_ref[i:i + 1, :]
        xcs.append(xc)
    a, bt = _lru_gates(jnp.concatenate(xcs, axis=0), wa_ref, ba_ref, wx_ref, bx_ref, sp_ref)
    h = h0_ref[0]
    for t in range(dt):
        rows = slice(t * db, (t + 1) * db)
        h = a[rows] * h + bt[rows]
        lo_ref[rows, :] = (jax.nn.gelu(yl_ref[rows, :]) * h).astype(lo_ref.dtype)
    h_out_ref[...] = h
    for i in range(taps):
        conv_out_ref[:, i * w:(i + 1) * w] = xp[dt + i]


def _lru_sample(proj, state_conv, state_lru, e, db, dt, w, cw, cb, wa, ba, wx, bx, sp):
    rows = db * dt
    assert proj.shape[0] == rows
    rb = 0
    xl_col = (proj.shape[1] - 2 * w) // w
    taps = CONV_W - 1
    conv0 = state_conv.reshape(state_conv.shape[0], db, taps * w)
    vec = lambda: pl.BlockSpec((1, w), lambda i: (0, 0))
    blocks = lambda: pl.BlockSpec(wa.shape, lambda i: (0, 0, 0))
    vmem = 16 * rows * w * 4
    lo, h_new, conv_new = pl.pallas_call(
        functools.partial(_lru_sample_kernel, dt=dt),
        grid=(1,),
        in_specs=[
            pl.BlockSpec((rows, w), lambda i: (rb, xl_col)),
            pl.BlockSpec((rows, w), lambda i: (rb, xl_col + 1)),
            pl.BlockSpec((1, db, taps * w), lambda i: (e, 0, 0)),
            pl.BlockSpec((1, db, w), lambda i: (e, 0, 0)),
            pl.BlockSpec((CONV_W, w), lambda i: (0, 0)),
            vec(), blocks(), vec(), blocks(), vec(), vec(),
        ],
        out_specs=[
            pl.BlockSpec((rows, w), lambda i: (0, 0)),
            pl.BlockSpec((db, w), lambda i: (0, 0)),
            pl.BlockSpec((db, taps * w), lambda i: (0, 0)),
        ],
        out_shape=[jax.ShapeDtypeStruct((rows, w), BF16),
                   jax.ShapeDtypeStruct((db, w), F32),
                   jax.ShapeDtypeStruct((db, taps * w), F32)],
        compiler_params=_params(("arbitrary",), vmem),
        name="lru_sample",
    )(proj, proj, conv0, state_lru, cw, cb.reshape(1, w), wa, ba.reshape(1, w), wx,
      bx.reshape(1, w), sp)
    return lo, h_new, conv_new.reshape(db, taps, w)


def _s5_tables(a_re, a_im, b_re, b_im, c_re, c_im, d, log_dt):
    g = a_re.shape[0]
    nb = g // SSM_BLOCK_GROUPS
    dt = jnp.exp(log_dt)[:, None]
    mag = jnp.exp(a_re * dt)
    abr = mag * jnp.cos(a_im * dt)
    abi = mag * jnp.sin(a_im * dt)
    den = a_re * a_re + a_im * a_im
    nr, ni = abr - 1.0, abi
    fr = (nr * a_re + ni * a_im) / den
    fi = (ni * a_re - nr * a_im) / den
    bbr = fr[..., None] * b_re - fi[..., None] * b_im
    bbi = fr[..., None] * b_im + fi[..., None] * b_re
    per_lane_block = V7X_LANES // SSM_P
    owner = (jnp.arange(SSM_BLOCK_GROUPS) % per_lane_block)[None, :, None, None, None]
    slot = jnp.arange(per_lane_block)[None, None, None, :, None]

    def pack(gkp):
        gkp = gkp.reshape(nb, SSM_BLOCK_GROUPS, SSM_GROUP, 1, SSM_P)
        return jnp.where(owner == slot, gkp, 0.0).reshape(nb, V7X_MXU_DIM, V7X_LANES)

    p_in = jnp.stack([pack(bbr.transpose(0, 2, 1)), pack(bbi.transpose(0, 2, 1))])
    p_out = jnp.stack([pack(c_re), pack(-c_im)])
    return (abr.reshape(nb, SSM_BLOCK_STATES), abi.reshape(nb, SSM_BLOCK_STATES),
            p_in, p_out, d.reshape(nb, 1, V7X_MXU_DIM))


def _expand_blockdiag(packed_ref, i, w_ref):
    ln = V7X_LANES
    per_lane_block = ln // SSM_P
    w_ref[...] = jnp.zeros_like(w_ref)
    for c in range(2):
        for g in range(SSM_BLOCK_GROUPS):
            rows = slice(g * SSM_GROUP, (g + 1) * SSM_GROUP)
            col = c * SSM_BLOCK_STATES + (g // per_lane_block) * ln
            w_ref[rows, col:col + ln] = packed_ref[c, i, rows, :].astype(w_ref.dtype)


def _s5_prompt_kernel(x_ref, g_ref, ar_ref, ai_ref, pin_ref, pout_ref, d_ref, y_ref,
                      hr_out_ref, hi_out_ref, win_ref, wout_ref, u_ref, sr_ref, si_ref,
                      hr_ref, hi_ref, *, n_chunks):
    c = pl.program_id(1)
    tc = x_ref.shape[0]
    nb = win_ref.shape[0]
    ns = SSM_BLOCK_STATES
    bw = V7X_MXU_DIM
    ln = V7X_LANES
    slots = V7X_SUBLANES
    nl = sr_ref.shape[0]
    splits = ns // (nl * ln)
    blocks_per_pass = slots // splits
    pitch = S5_ROW_PITCH

    @pl.when(c == 0)
    def _():
        hr_ref[...] = jnp.zeros_like(hr_ref)
        hi_ref[...] = jnp.zeros_like(hi_ref)
        for i in range(nb):
            _expand_blockdiag(pin_ref, i, win_ref.at[i])
            _expand_blockdiag(pout_ref, i, wout_ref.at[i])

    u_ref[...] = _rms(x_ref[...], g_ref[...])

    def slot_rows(slot):
        return pl.ds(slot, tc, stride=pitch)

    for p in range(nb // blocks_per_pass):
        blocks = range(p * blocks_per_pass, (p + 1) * blocks_per_pass)
        for il, i in enumerate(blocks):
            bu = _bdot(u_ref[:, i * bw:(i + 1) * bw].astype(BF16), win_ref[i])
            for sp in range(splits):
                for l in range(nl):
                    col = (sp * nl + l) * ln
                    sr_ref[l, slot_rows(il * splits + sp), :] = bu[:, col:col + ln]
                    si_ref[l, slot_rows(il * splits + sp), :] = bu[:, ns + col:ns + col + ln]

        srows = slice(p * slots, (p + 1) * slots)
        ar = [ar_ref[srows, l * ln:(l + 1) * ln] for l in range(nl)]
        ai = [ai_ref[srows, l * ln:(l + 1) * ln] for l in range(nl)]

        def body(t, carry):
            rows = pl.ds(t * pitch, slots)
            out = []
            for l in range(nl):
                hr, hi = carry[l]
                hr_n = ar[l] * hr - ai[l] * hi + sr_ref[l, rows, :]
                hi_n = ar[l] * hi + ai[l] * hr + si_ref[l, rows, :]
                sr_ref[l, rows, :] = hr_n
                si_ref[l, rows, :] = hi_n
                out.append((hr_n, hi_n))
            return tuple(out)

        init = tuple((hr_ref[srows, l * ln:(l + 1) * ln], hi_ref[srows, l * ln:(l + 1) * ln])
                     for l in range(nl))
        last = lax.fori_loop(0, tc, body, init, unroll=8)
        for l in range(nl):
            hr_ref[srows, l * ln:(l + 1) * ln] = last[l][0]
            hi_ref[srows, l * ln:(l + 1) * ln] = last[l][1]

        for il, i in enumerate(blocks):
            parts = [ref[l, slot_rows(il * splits + sp), :].astype(BF16)
                     for ref in (sr_ref, si_ref) for sp in range(splits) for l in range(nl)]
            cols = slice(i * bw, (i + 1) * bw)
            y = _bdot_nt(jnp.concatenate(parts, axis=-1), wout_ref[i]) + d_ref[i] * u_ref[:, cols]
            y_ref[:, cols] = jax.nn.gelu(y).astype(y_ref.dtype)

    @pl.when(c == n_chunks - 1)
    def _():
        hr_out_ref[0] = hr_ref[...]
        hi_out_ref[0] = hi_ref[...]


def _s5_prompt(x, g, tables, b, t):
    d = x.shape[1]
    abr, abi, p_in, p_out, dd = tables
    nb = abr.shape[0]
    ns = SSM_BLOCK_STATES
    w_shape = (nb, V7X_MXU_DIM, 2 * ns)
    tc = _pick_tile(t, 256, 16)
    nc = t // tc
    passes = 2
    assert nb % passes == 0 and V7X_SUBLANES % (nb // passes) == 0
    splits = V7X_SUBLANES // (nb // passes)
    slot_lanes = ns // splits
    nl = slot_lanes // V7X_LANES
    abr = abr.reshape(nb * splits, slot_lanes)
    abi = abi.reshape(nb * splits, slot_lanes)
    const = lambda a: _once(a.shape, lambda bi, ci: (0,) * a.ndim)
    scan_bytes = nl * tc * S5_ROW_PITCH * V7X_LANES * 4
    w_bytes = w_shape[0] * w_shape[1] * w_shape[2] * 2
    vmem = (5 * tc * d * 4 + 2 * tc * d * 2 + 2 * w_bytes + 2 * p_in.size * 4
            + 2 * scan_bytes + 6 * tc * 2 * ns * 4)
    y, hr, hi = pl.pallas_call(
        functools.partial(_s5_prompt_kernel, n_chunks=nc),
        grid=(b, nc),
        in_specs=[
            pl.BlockSpec((tc, d), lambda bi, ci: (bi * nc + ci, 0)),
            pl.BlockSpec((1, d), lambda bi, ci: (0, 0)),
            const(abr), const(abi), const(p_in), const(p_out), const(dd),
        ],
        out_specs=[
            pl.BlockSpec((tc, d), lambda bi, ci: (bi * nc + ci, 0)),
            pl.BlockSpec((1,) + abr.shape, lambda bi, ci: (bi, 0, 0)),
            pl.BlockSpec((1,) + abr.shape, lambda bi, ci: (bi, 0, 0)),
        ],
        out_shape=[jax.ShapeDtypeStruct((b * t, d), BF16),
                   jax.ShapeDtypeStruct((b,) + abr.shape, F32),
                   jax.ShapeDtypeStruct((b,) + abr.shape, F32)],
        scratch_shapes=[pltpu.VMEM(w_shape, BF16), pltpu.VMEM(w_shape, BF16),
                        pltpu.VMEM((tc, d), F32),
                        pltpu.VMEM((nl, tc * S5_ROW_PITCH, V7X_LANES), F32),
                        pltpu.VMEM((nl, tc * S5_ROW_PITCH, V7X_LANES), F32),
                        pltpu.VMEM(abr.shape, F32), pltpu.VMEM(abr.shape, F32)],
        compiler_params=_params(("parallel", "arbitrary"), vmem),
        name="s5_prompt",
    )(x, g.reshape(1, d), abr, abi, p_in, p_out, dd)
    return y, hr.reshape(b, nb * ns), hi.reshape(b, nb * ns)


def _s5_sample_kernel(x_ref, g_ref, ar_ref, ai_ref, pin_ref, pout_ref, d_ref, h0r_ref,
                      h0i_ref, y_ref, hr_out_ref, hi_out_ref, win_ref, wout_ref, uf_ref,
                      ub_ref, *, dt):
    i = pl.program_id(0)
    nb = uf_ref.shape[0]
    bw = V7X_MXU_DIM
    ns = SSM_BLOCK_STATES
    db = h0r_ref.shape[1]

    @pl.when(i == 0)
    def _():
        u = _rms(x_ref[...], g_ref[...])
        for n in range(nb):
            uf_ref[n] = u[:, n * bw:(n + 1) * bw]
            ub_ref[n] = u[:, n * bw:(n + 1) * bw].astype(BF16)

    _expand_blockdiag(pin_ref, 0, win_ref)
    _expand_blockdiag(pout_ref, 0, wout_ref)
    bu = _bdot(ub_ref[i], win_ref[...])
    ar, ai = ar_ref[pl.ds(i, 1), :], ai_ref[pl.ds(i, 1), :]
    hr, hi = h0r_ref[0], h0i_ref[0]
    states = []
    for t in range(dt):
        rows = slice(t * db, (t + 1) * db)
        hr, hi = (ar * hr - ai * hi + bu[rows, :ns], ar * hi + ai * hr + bu[rows, ns:])
        states.append(jnp.concatenate([hr.astype(BF16), hi.astype(BF16)], axis=-1))
    y = _bdot_nt(jnp.concatenate(states, axis=0), wout_ref[...]) + d_ref[0] * uf_ref[i]
    y_ref[...] = jax.nn.gelu(y).astype(y_ref.dtype)
    hr_out_ref[...] = hr
    hi_out_ref[...] = hi


def _s5_sample(x, g, tables, state_re, state_im, o, db, dt):
    d = x.shape[1]
    abr, abi, p_in, p_out, dd = tables
    nb = abr.shape[0]
    ns = SSM_BLOCK_STATES
    bw = V7X_MXU_DIM
    rows = db * dt
    assert x.shape[0] == rows
    rb = 0
    no = state_re.shape[0]
    h0r = state_re.reshape(no, db, nb * ns)
    h0i = state_im.reshape(no, db, nb * ns)
    blk = lambda a: pl.BlockSpec((1,) + a.shape[1:], lambda i: (i, 0, 0))
    packed = lambda a: pl.BlockSpec((2, 1) + a.shape[2:], lambda i: (0, i, 0, 0))
    vmem = (2 * rows * d * 4 + rows * d * 6 + 4 * bw * 2 * ns * 2 + 8 * db * ns * 4
            + 8 * rows * 2 * ns * 4)
    return pl.pallas_call(
        functools.partial(_s5_sample_kernel, dt=dt),
        grid=(nb,),
        in_specs=[
            _once((rows, d), lambda i: (rb, 0)),
            pl.BlockSpec((1, d), lambda i: (0, 0)),
            pl.BlockSpec(abr.shape, lambda i: (0, 0)),
            pl.BlockSpec(abi.shape, lambda i: (0, 0)),
            packed(p_in), packed(p_out), blk(dd),
            pl.BlockSpec((1, db, ns), lambda i: (o, 0, i)),
            pl.BlockSpec((1, db, ns), lambda i: (o, 0, i)),
        ],
        out_specs=[
            pl.BlockSpec((rows, bw), lambda i: (0, i)),
            pl.BlockSpec((db, ns), lambda i: (0, i)),
            pl.BlockSpec((db, ns), lambda i: (0, i)),
        ],
        out_shape=[jax.ShapeDtypeStruct((rows, d), BF16),
                   jax.ShapeDtypeStruct((db, nb * ns), F32),
                   jax.ShapeDtypeStruct((db, nb * ns), F32)],
        scratch_shapes=[pltpu.VMEM((bw, 2 * ns), BF16), pltpu.VMEM((bw, 2 * ns), BF16),
                        pltpu.VMEM((nb, rows, bw), F32), pltpu.VMEM((nb, rows, bw), BF16)],
        compiler_params=_params(("arbitrary",), vmem),
        name="s5_sample",
    )(x, g.reshape(1, d), abr, abi, p_in, p_out, dd, h0r, h0i)


def kernel(x_prompt, x_sample, state_ret, state_lru, state_conv, state_ssm_re, state_ssm_im, norm_mix_even, w_in_even, lru_conv_w, lru_conv_b, lru_wa, lru_ba, lru_wx, lru_bx, lru_lambda, w_out_even, norm_mix_odd, ssm_a_re, ssm_a_im, ssm_b_re, ssm_b_im, ssm_c_re, ssm_c_im, ssm_d, ssm_log_dt, w_glu, b_glu, norm_ffn, w_ffn_gu, w_ffn_down, norm_final):
    b, t, d = x_prompt.shape
    db, dt, _ = x_sample.shape
    depth = norm_ffn.shape[0]
    half = d // 2
    past_len = PAST_LEN
    groups, ssm_p = ssm_a_re.shape[1:]
    assert ssm_p == SSM_P and groups * SSM_GROUP == d and groups % SSM_BLOCK_GROUPS == 0

    xp = x_prompt.reshape(b * t, d)
    xs = x_sample.transpose(1, 0, 2).reshape(dt * db, d)

    rets_p, ret_s, lrus_p, lrus_s, convs_p, convs_s = [], None, [], [], [], []
    sres_p, sres_s, sims_p, sims_s = [], [], [], []
    for layer in range(depth):
        if layer % 2 == 0:
            e = layer // 2
            sp = jax.nn.softplus(-lru_lambda[e]).reshape(1, half)
            lru_w = (lru_conv_w[e], lru_conv_b[e], lru_wa[e], lru_ba[e], lru_wx[e], lru_bx[e], sp)

            proj_p = _inproj(xp, norm_mix_even[e], w_in_even, e)
            o_p, ret_p = _ret_prompt(proj_p, b, t, half)
            lo_p, lru_p, conv_p = _lru_prompt(proj_p, b, t, half, *lru_w)
            xp = _outproj(o_p, lo_p, w_out_even, e, xp)

            proj_s = _inproj(xs, norm_mix_even[e], w_in_even, e)
            o_s, ret_s = _ret_sample(proj_s, state_ret, e, ret_s, db, dt, half, past_len)
            lo_s, lru_s, conv_s = _lru_sample(proj_s, state_conv, state_lru, e, db, dt, half,
                                              *lru_w)
            xs = _outproj(o_s, lo_s, w_out_even, e, xs)

            rets_p.append(ret_p)
            lrus_p.append(lru_p.reshape(b, half))
            lrus_s.append(lru_s)
            convs_p.append(conv_p)
            convs_s.append(conv_s)
        else:
            o = layer // 2
            tables = _s5_tables(ssm_a_re[o], ssm_a_im[o], ssm_b_re[o], ssm_b_im[o],
                                ssm_c_re[o], ssm_c_im[o], ssm_d[o], ssm_log_dt[o])
            y_p, sre_p, sim_p = _s5_prompt(xp, norm_mix_odd[o], tables, b, t)
            xp = _glu(y_p, w_glu, o, b_glu[o], xp)
            y_s, sre_s, sim_s = _s5_sample(xs, norm_mix_odd[o], tables, state_ssm_re,
                                           state_ssm_im, o, db, dt)
            xs = _glu(y_s, w_glu, o, b_glu[o], xs)
            sres_p.append(sre_p.reshape(b, groups, ssm_p))
            sims_p.append(sim_p.reshape(b, groups, ssm_p))
            sres_s.append(sre_s.reshape(db, groups, ssm_p))
            sims_s.append(sim_s.reshape(db, groups, ssm_p))
        g_final = norm_final if layer == depth - 1 else None
        xp = _ffn(xp, norm_ffn[layer], w_ffn_gu, w_ffn_down, layer, g_final)
        xs = _ffn(xs, norm_ffn[layer], w_ffn_gu, w_ffn_down, layer, g_final)

    y_prompt = xp.reshape(b, t, d)
    y_sample = xs.reshape(dt, db, d).transpose(1, 0, 2)
    return (y_prompt, y_sample, jnp.stack(rets_p), ret_s, jnp.stack(lrus_p),
            jnp.stack(lrus_s), jnp.stack(convs_p), jnp.stack(convs_s), jnp.stack(sres_p),
            jnp.stack(sres_s), jnp.stack(sims_p), jnp.stack(sims_s))
```

```python
import functools

import jax
import jax.numpy as jnp
import numpy as np
from jax import lax
from jax.experimental import pallas as pl
from jax.experimental.pallas import tpu as pltpu

F32 = jnp.float32
BF16 = jnp.bfloat16

EPS = 1e-6
PAST_LEN = 16384
ROPE_BASE = 10000.0
RET_HEADS = 4
RET_CHUNK = 128
LRU_HEADS = 8
LRU_C = 8.0
CONV_W = 4
SSM_GROUP = 16
SSM_P = 64
SSM_CHUNK = 128

V7X_SUBLANES = 8
V7X_LANES = 128
V7X_MXU_DIM = 256
V7X_VMEM_BYTES = 64 * 1024 * 1024
VMEM_LIMIT_CAP = V7X_VMEM_BYTES - 6 * 1024 * 1024

DENSE_ROW_TILE = 2048

TILE_COPY_CHUNKS = 8

SSM_BLOCK_GROUPS = V7X_MXU_DIM // SSM_GROUP
SSM_BLOCK_STATES = SSM_BLOCK_GROUPS * SSM_P

RET_CHUNKS_PER_STEP = 4

RET_SAMPLE_ROWS = 128

S5_ROW_PITCH = 12


def _pick_tile(n, target, mult):
    best = None
    for t in range(mult, min(n, target) + 1, mult):
        if n % t == 0:
            best = t
    assert best is not None, (n, target, mult)
    return best


def _params(semantics, vmem_bytes):
    limit = min(int(vmem_bytes * 1.2) + (6 << 20), VMEM_LIMIT_CAP)
    return pltpu.CompilerParams(dimension_semantics=semantics, vmem_limit_bytes=limit)


def _once(block_shape, index_map):
    return pl.BlockSpec(block_shape, index_map, pipeline_mode=pl.Buffered(1))


def _rms(x, g):
    return x * lax.rsqrt(jnp.mean(x * x, axis=-1, keepdims=True) + EPS) * g


def _norm_rows_into(x_ref, g_ref, xn_ref):
    tm = x_ref.shape[0]
    rows = _pick_tile(tm, 64, 16)

    def body(r, carry):
        sl = pl.ds(pl.multiple_of(r * rows, rows), rows)
        xn_ref[sl, :] = _rms(x_ref[sl, :], g_ref[...]).astype(xn_ref.dtype)
        return carry

    lax.fori_loop(0, tm // rows, body, 0)


def _row_chunk_copies(src, dst, sems):
    n = sems.shape[0]
    rc = src.shape[0] // n
    return [pltpu.make_async_copy(src.at[pl.ds(c * rc, rc), :], dst.at[pl.ds(c * rc, rc), :],
                                  sems.at[c]) for c in range(n)]


def _load_and_norm_rows(x_hbm_rows, x_ref, g_ref, xn_ref, sems):
    copies = _row_chunk_copies(x_hbm_rows, x_ref, sems)
    rc = x_ref.shape[0] // len(copies)
    for cp in copies:
        cp.start()
    for c, cp in enumerate(copies):
        cp.wait()
        rows = pl.ds(c * rc, rc)
        _norm_rows_into(x_ref.at[rows, :], g_ref, xn_ref.at[rows, :])


def _bdot(a, b):
    return jnp.dot(a, b, preferred_element_type=F32)


def _bdot_nt(a, b):
    return lax.dot_general(a, b, (((1,), (1,)), ((), ())), preferred_element_type=F32)


def _inproj_kernel(x_hbm, g_ref, w_ref, o_ref, x_ref, xn_ref, x_sems):
    @pl.when(pl.program_id(1) == 0)
    def _():
        tm = x_ref.shape[0]
        tile = pl.ds(pl.multiple_of(pl.program_id(0) * tm, tm), tm)
        _load_and_norm_rows(x_hbm.at[tile, :], x_ref, g_ref, xn_ref, x_sems)

    o_ref[...] = _bdot(xn_ref[...], w_ref[...].astype(BF16))


def _inproj(x, g, w, e):
    m, d = x.shape
    n = w.shape[2]
    tm = _pick_tile(m, DENSE_ROW_TILE, 16)
    tn = _pick_tile(n, 512, V7X_LANES)
    vmem = tm * d * 4 + tm * d * 2 + 2 * d * tn * 4 + 2 * tm * tn * 4 + d * tn * 2
    return pl.pallas_call(
        _inproj_kernel,
        grid=(m // tm, n // tn),
        in_specs=[
            pl.BlockSpec(memory_space=pl.ANY),
            _once((1, d), lambda i, j: (0, 0)),
            pl.BlockSpec((None, d, tn), lambda i, j: (e, 0, j)),
        ],
        out_specs=pl.BlockSpec((tm, tn), lambda i, j: (i, j)),
        out_shape=jax.ShapeDtypeStruct((m, n), F32),
        scratch_shapes=[pltpu.VMEM((tm, d), F32), pltpu.VMEM((tm, d), BF16),
                        pltpu.SemaphoreType.DMA((TILE_COPY_CHUNKS,))],
        compiler_params=_params(("parallel", "arbitrary"), vmem),
        name="inproj",
    )(x, g.reshape(1, d), w)


def _outproj_kernel(o_ref, lo_ref, wo_ref, wl_ref, x_ref, out_ref):
    acc = _bdot(o_ref[...], wo_ref[...].astype(BF16))
    acc = acc + _bdot(lo_ref[...], wl_ref[...].astype(BF16))
    out_ref[...] = x_ref[...] + acc


def _outproj(o, lo, w, e, x):
    m, d = x.shape
    half = o.shape[1]
    tm = _pick_tile(m, DENSE_ROW_TILE, 16)
    tn = _pick_tile(d, 512, V7X_LANES)
    vmem = 4 * tm * half * 2 + 4 * half * tn * 4 + 4 * tm * tn * 4 + 2 * half * tn * 2
    return pl.pallas_call(
        _outproj_kernel,
        grid=(m // tm, d // tn),
        in_specs=[
            pl.BlockSpec((tm, half), lambda i, j: (i, 0)),
            pl.BlockSpec((tm, half), lambda i, j: (i, 0)),
            pl.BlockSpec((None, half, tn), lambda i, j: (e, 0, j)),
            pl.BlockSpec((None, half, tn), lambda i, j: (e, 1, j)),
            pl.BlockSpec((tm, tn), lambda i, j: (i, j)),
        ],
        out_specs=pl.BlockSpec((tm, tn), lambda i, j: (i, j)),
        out_shape=jax.ShapeDtypeStruct((m, d), F32),
        compiler_params=_params(("parallel", "arbitrary"), vmem),
        name="outproj",
    )(o, lo, w, w, x)


def _glu_kernel(y_ref, w1_ref, w2_ref, b1_ref, b2_ref, x_ref, out_ref):
    y = y_ref[...]
    z1 = _bdot(y, w1_ref[...].astype(BF16)) + b1_ref[...]
    z2 = _bdot(y, w2_ref[...].astype(BF16)) + b2_ref[...]
    out_ref[...] = x_ref[...] + z1 * jax.nn.sigmoid(z2)


def _glu(y, w, o, b, x):
    m, d = x.shape
    tm = _pick_tile(m, DENSE_ROW_TILE, 16)

    tn = _pick_tile(d, 256, V7X_LANES)
    nj = d // tn
    vmem = 2 * tm * d * 2 + 4 * d * tn * 4 + 4 * tm * tn * 4 + 2 * d * tn * 2 + 2 * tm * tn * 4
    return pl.pallas_call(
        _glu_kernel,
        grid=(m // tm, nj),
        in_specs=[
            pl.BlockSpec((tm, d), lambda i, j: (i, 0)),
            pl.BlockSpec((None, d, tn), lambda i, j: (o, 0, j)),
            pl.BlockSpec((None, d, tn), lambda i, j: (o, 0, nj + j)),
            pl.BlockSpec((1, tn), lambda i, j: (0, j)),
            pl.BlockSpec((1, tn), lambda i, j: (0, nj + j)),
            pl.BlockSpec((tm, tn), lambda i, j: (i, j)),
        ],
        out_specs=pl.BlockSpec((tm, tn), lambda i, j: (i, j)),
        out_shape=jax.ShapeDtypeStruct((m, d), F32),
        compiler_params=_params(("parallel", "arbitrary"), vmem),
        name="glu",
    )(y, w, w, b.reshape(1, 2 * d), b.reshape(1, 2 * d), x)


def _ffn_kernel(x_hbm, g_ref, gf_ref, wg_ref, wu_ref, wd_ref, out_ref, xn_ref, x_sems, *,
                final_norm):
    i, j = pl.program_id(0), pl.program_id(1)

    @pl.when(j == 0)
    def _():
        tm = out_ref.shape[0]
        tile = pl.ds(pl.multiple_of(i * tm, tm), tm)
        _load_and_norm_rows(x_hbm.at[tile, :], out_ref, g_ref, xn_ref, x_sems)

    xn = xn_ref[...]
    gate = _bdot(xn, wg_ref[...].astype(BF16))
    up = _bdot(xn, wu_ref[...].astype(BF16))
    act = (gate * jax.nn.sigmoid(gate) * up).astype(BF16)
    out_ref[...] += _bdot(act, wd_ref[...].astype(BF16))

    if final_norm:
        @pl.when(j == pl.num_programs(1) - 1)
        def _():
            _norm_rows_into(out_ref, gf_ref, out_ref)


def _ffn(x, g, w_gu, w_down, layer, g_final=None):
    m, d = x.shape
    hidden = w_down.shape[1]
    tm = _pick_tile(m, DENSE_ROW_TILE, 16)

    th = _pick_tile(hidden, 256, V7X_LANES)
    nh = hidden // th
    vmem = (tm * d * 4 + tm * d * 2 + 6 * d * th * 4 + 3 * d * th * 2
            + 3 * tm * th * 4)
    gf = g if g_final is None else g_final
    return pl.pallas_call(
        functools.partial(_ffn_kernel, final_norm=g_final is not None),
        grid=(m // tm, nh),
        in_specs=[
            pl.BlockSpec(memory_space=pl.ANY),
            _once((1, d), lambda i, j: (0, 0)),
            _once((1, d), lambda i, j: (0, 0)),
            pl.BlockSpec((None, d, th), lambda i, j: (layer, 0, j)),
            pl.BlockSpec((None, d, th), lambda i, j: (layer, 0, nh + j)),
            pl.BlockSpec((None, th, d), lambda i, j: (layer, j, 0)),
        ],
        out_specs=_once((tm, d), lambda i, j: (i, 0)),
        out_shape=jax.ShapeDtypeStruct((m, d), F32),
        scratch_shapes=[pltpu.VMEM((tm, d), BF16),
                        pltpu.SemaphoreType.DMA((TILE_COPY_CHUNKS,))],
        compiler_params=_params(("parallel", "arbitrary"), vmem),
        name="ffn",
    )(x, g.reshape(1, d), gf.reshape(1, d), w_gu, w_gu, w_down)


def _rope_tables(pos, half):
    inv = 1.0 / np.power(ROPE_BASE, np.linspace(0.0, 1.0, half))
    ang = np.asarray(pos, np.float64)[:, None] * inv[None, :]
    return np.cos(ang).astype(np.float32), np.sin(ang).astype(np.float32)


def _decay_tables(c, dk):
    log_g = np.log1p(-np.exp2(-5.0 - np.arange(RET_HEADS, dtype=np.float64)))
    idx = np.arange(c, dtype=np.float64)
    diff = idx[:, None] - idx[None, :]
    mask = np.where(diff[None] >= 0,
                    np.exp(np.maximum(diff, 0.0)[None] * log_g[:, None, None]), 0.0)
    q_dec = np.exp((idx[None, :] + 1.0) * log_g[:, None])
    k_dec = np.exp((c - 1.0 - idx)[None, :] * log_g[:, None])
    chunk_dec = np.exp(c * log_g)
    q_dec = np.broadcast_to(q_dec[:, :, None], (RET_HEADS, c, dk))
    k_dec = np.broadcast_to(k_dec[:, :, None], (RET_HEADS, c, dk))
    chunk_dec = np.broadcast_to(chunk_dec[:, None, None], (RET_HEADS, 1, dk))
    f32 = lambda a: np.ascontiguousarray(a, dtype=np.float32)
    return f32(mask), f32(q_dec), f32(k_dec), f32(chunk_dec)


def _rotate(x, cos, sin):
    half = x.shape[-1] // 2
    x1, x2 = x[:, :half], x[:, half:]
    return jnp.concatenate([x1 * cos - x2 * sin, x2 * cos + x1 * sin], axis=-1)


def _ret_prompt_kernel(q_ref, k_ref, v_ref, g_ref, cos_ref, sin_ref, mask_ref, qd_ref,
                       kd_ref, cd_ref, o_ref, s_out_ref, s_ref, *, n_chunks, scale):
    c = pl.program_id(1)
    heads, dk, _ = s_ref.shape

    @pl.when(c == 0)
    def _():
        s_ref[...] = jnp.zeros_like(s_ref)

    chunk = mask_ref.shape[1]
    for h in range(heads):
        cols = slice(h * dk, (h + 1) * dk)
        s = s_ref[h]
        for cc in range(q_ref.shape[0] // chunk):
            rows = slice(cc * chunk, (cc + 1) * chunk)
            cos, sin = cos_ref[rows, :], sin_ref[rows, :]
            q = _rotate(q_ref[rows, cols], cos, sin)
            k = _rotate(k_ref[rows, cols], cos, sin) * scale
            qb, kb, vb = q.astype(BF16), k.astype(BF16), v_ref[rows, cols].astype(BF16)
            scores = lax.dot_general(qb, kb, (((1,), (1,)), ((), ())),
                                     preferred_element_type=F32) * mask_ref[h]
            o = _bdot(scores.astype(BF16), vb) + _bdot(qb, s.astype(BF16)) * qd_ref[h]
            kdb = (k * kd_ref[h]).astype(BF16)
            s = s * cd_ref[h] + lax.dot_general(kdb, vb, (((0,), (0,)), ((), ())),
                                                preferred_element_type=F32)
            o = o * lax.rsqrt(jnp.mean(o * o, axis=-1, keepdims=True) + EPS)
            gr = g_ref[rows, cols]
            o_ref[rows, cols] = (o * (gr * jax.nn.sigmoid(gr))).astype(o_ref.dtype)
        s_ref[h] = s

    @pl.when(c == n_chunks - 1)
    def _():
        s_out_ref[0] = s_ref[...]


def _ret_prompt(proj, b, t, half):
    h = RET_HEADS
    dk = half // h
    assert t % RET_CHUNK == 0
    c = _pick_tile(t, RET_CHUNKS_PER_STEP * RET_CHUNK, RET_CHUNK)
    nc = t // c
    cos, sin = _rope_tables(np.arange(t), dk // 2)
    tables = _decay_tables(RET_CHUNK, dk)

    def col(off):
        return pl.BlockSpec((c, half), lambda bi, ci: (bi * nc + ci, off))

    const = lambda a: pl.BlockSpec(a.shape, lambda bi, ci: (0, 0, 0))
    vmem = 10 * c * half * 4 + 3 * h * dk * dk * 4 + sum(2 * a.size * 4 for a in tables)
    return pl.pallas_call(
        functools.partial(_ret_prompt_kernel, n_chunks=nc, scale=dk ** -0.5),
        grid=(b, nc),
        in_specs=[
            col(0), col(1), col(2), col(3),
            pl.BlockSpec((c, dk // 2), lambda bi, ci: (ci, 0)),
            pl.BlockSpec((c, dk // 2), lambda bi, ci: (ci, 0)),
            *[const(a) for a in tables],
        ],
        out_specs=[
            pl.BlockSpec((c, half), lambda bi, ci: (bi * nc + ci, 0)),
            pl.BlockSpec((1, h, dk, dk), lambda bi, ci: (bi, 0, 0, 0)),
        ],
        out_shape=[jax.ShapeDtypeStruct((b * t, half), BF16),
                   jax.ShapeDtypeStruct((b, h, dk, dk), F32)],
        scratch_shapes=[pltpu.VMEM((h, dk, dk), F32)],
        compiler_params=_params(("parallel", "arbitrary"), vmem),
        name="ret_prompt",
    )(proj, proj, proj, proj, cos, sin, *tables)


def _ret_sample_kernel(*refs, bb, scale, chained):
    (q_ref, k_ref, v_ref, g_ref, cos_ref, sin_ref, mask_ref, qd_ref, kd_ref, cd_ref,
     s_in_ref) = refs[:11]
    o_ref, s_out_ref, kdt_ref, acc_ref = refs[12 if chained else 11:]

    for later in range(1, s_out_ref.shape[0]):
        s_out_ref[later] = jnp.zeros(s_out_ref.shape[1:], s_out_ref.dtype)
    dt, _, dk = q_ref.shape
    rows = dt * bb
    cos, sin = cos_ref[...], sin_ref[...]
    q = _rotate(q_ref[...].reshape(rows, dk), cos, sin)
    k = _rotate(k_ref[...].reshape(rows, dk), cos, sin) * scale
    qb, kb = q.astype(BF16), k.astype(BF16)
    vb = v_ref[...].reshape(rows, dk).astype(BF16)
    scores = lax.dot_general(qb, kb, (((1,), (1,)), ((), ())),
                             preferred_element_type=F32) * mask_ref[0]
    intra = _bdot(scores.astype(BF16), vb)
    kdt_ref[...] = (k * kd_ref[0]).T
    cd = cd_ref[0]
    row_batch = lax.broadcasted_iota(jnp.int32, (rows, 1), 0) % bb
    col_batch = lax.broadcasted_iota(jnp.int32, (1, rows), 1) % bb

    acc_ref[...] = jnp.zeros_like(acc_ref)

    def body(j, carry):
        s = s_in_ref[0, j, 0]
        acc_ref[...] = jnp.where(row_batch == j, _bdot(qb, s.astype(BF16)), acc_ref[...])
        kdt_j = jnp.where(col_batch == j, kdt_ref[...], 0.0).astype(BF16)
        s_out_ref[0, j, 0] = s * cd + _bdot(kdt_j, vb)
        return carry

    lax.fori_loop(0, bb, body, 0, unroll=4)
    o = intra + acc_ref[...] * qd_ref[0]
    o = o * lax.rsqrt(jnp.mean(o * o, axis=-1, keepdims=True) + EPS)
    gr = g_ref[...].reshape(rows, dk)
    o_ref[...] = (o * (gr * jax.nn.sigmoid(gr))).astype(o_ref.dtype).reshape(dt, bb, dk)


def _ret_sample(proj, state_ret, e, prev_states, db, dt, half, past_len):
    dk = half // RET_HEADS
    h = RET_HEADS
    assert RET_CHUNK % dt == 0 and RET_SAMPLE_ROWS % dt == 0
    rows = RET_SAMPLE_ROWS
    bb = rows // dt
    assert db % bb == 0
    proj3 = proj.reshape(dt, db, proj.shape[1])
    cos, sin = _rope_tables(past_len + np.arange(dt), dk // 2)
    cos, sin = np.repeat(cos, bb, axis=0), np.repeat(sin, bb, axis=0)
    mask, q_dec, k_dec, chunk_dec = _decay_tables(dt, dk)
    mask = np.einsum("hnm,ab->hnamb", mask, np.eye(bb, dtype=np.float32)).reshape(h, rows, rows)
    q_dec, k_dec = np.repeat(q_dec, bb, axis=1), np.repeat(k_dec, bb, axis=1)

    def col(off):
        return pl.BlockSpec((dt, bb, dk), lambda bi, hi: (0, bi, off + hi))

    def per_head(shape):
        return pl.BlockSpec((1,) + shape, lambda bi, hi: (hi, 0, 0))

    chained = prev_states is not None
    if chained:
        out_state_block = pl.BlockSpec((1, bb, 1, dk, dk), lambda bi, hi: (e, bi, hi, 0, 0))
    else:
        assert e == 0
        n_layers = state_ret.shape[0]
        out_state_block = pl.BlockSpec((n_layers, bb, 1, dk, dk),
                                       lambda bi, hi: (0, bi, hi, 0, 0))
    operands = [proj3, proj3, proj3, proj3, cos, sin, mask, q_dec, k_dec, chunk_dec, state_ret]
    in_specs = [
        col(0), col(h), col(2 * h), col(3 * h),
        pl.BlockSpec((rows, dk // 2), lambda bi, hi: (0, 0)),
        pl.BlockSpec((rows, dk // 2), lambda bi, hi: (0, 0)),
        per_head((rows, rows)), per_head((rows, dk)), per_head((rows, dk)),
        per_head((1, dk)),
        pl.BlockSpec((1, bb, 1, dk, dk), lambda bi, hi: (e, bi, hi, 0, 0)),
    ]
    if chained:
        operands.append(prev_states)
        in_specs.append(pl.BlockSpec(memory_space=pl.ANY))
    out_layers = out_state_block.block_shape[0]
    vmem = (2 + 2 * out_layers) * bb * dk * dk * 4 + 12 * rows * dk * 4 + 4 * dk * dk * 4
    o, states = pl.pallas_call(
        functools.partial(_ret_sample_kernel, bb=bb, scale=dk ** -0.5, chained=chained),
        grid=(db // bb, h),
        in_specs=in_specs,
        out_specs=[
            pl.BlockSpec((dt, bb, dk), lambda bi, hi: (0, bi, hi)),
            out_state_block,
        ],
        out_shape=[jax.ShapeDtypeStruct((dt, db, half), BF16),
                   jax.ShapeDtypeStruct(state_ret.shape, state_ret.dtype)],
        scratch_shapes=[pltpu.VMEM((dk, rows), F32), pltpu.VMEM((rows, dk), F32)],
        input_output_aliases={len(operands) - 1: 1} if chained else {},
        compiler_params=_params(("parallel", "parallel"), vmem),
        name="ret_sample",
    )(*operands)
    return o.reshape(dt * db, half), states


def _lru_gates(xc, wa_ref, ba_ref, wx_ref, bx_ref, sp_ref):
    xcb = xc.astype(BF16)
    blk = wa_ref.shape[1]
    ra, ri = [], []
    for n in range(wa_ref.shape[0]):
        xn = xcb[:, n * blk:(n + 1) * blk]
        ra.append(_bdot(xn, wa_ref[n].astype(BF16)))
        ri.append(_bdot(xn, wx_ref[n].astype(BF16)))
    r = jax.nn.sigmoid(jnp.concatenate(ra, axis=-1) + ba_ref[...])
    i = jax.nn.sigmoid(jnp.concatenate(ri, axis=-1) + bx_ref[...])
    log_a = -LRU_C * r * sp_ref[...]
    a = jnp.exp(log_a)
    gap = jnp.maximum(-jnp.tanh(log_a) * (a * a + 1.0), 0.0)
    mult = jnp.where(gap > 0.0, gap * lax.rsqrt(gap), 0.0)
    return a, mult * i * xc


def _lru_prompt_kernel(xl_ref, yl_ref, cw_ref, cb_ref, wa_ref, ba_ref, wx_ref, bx_ref,
                       sp_ref, lo_ref, h_out_ref, conv_out_ref, xs_ref, a_ref, b_ref,
                       hc_ref, *, n_chunks):
    c = pl.program_id(1)
    tc, w = xl_ref.shape
    sub = V7X_SUBLANES

    @pl.when(c == 0)
    def _():
        xs_ref[0:sub, :] = jnp.zeros((sub, w), F32)
        hc_ref[...] = jnp.zeros_like(hc_ref)

    x = xl_ref[...]
    xs_ref[sub:sub + tc, :] = x
    xc = cb_ref[...] + x * cw_ref[CONV_W - 1:CONV_W, :]
    for i in range(CONV_W - 1):
        back = CONV_W - 1 - i
        xc = xc + xs_ref[sub - back:sub - back + tc, :] * cw_ref[i:i + 1, :]
    xs_ref[0:sub, :] = xs_ref[tc:tc + sub, :]

    a, bt = _lru_gates(xc, wa_ref, ba_ref, wx_ref, bx_ref, sp_ref)

    a3 = a.reshape(tc // sub, sub, w)
    b3 = bt.reshape(tc // sub, sub, w)
    step = lax.broadcasted_iota(jnp.int32, (1, sub, 1), 1)
    for s in (1, 2, 4):
        keep = step >= s
        a_prev = jnp.where(keep, pltpu.roll(a3, s, axis=1), 1.0)
        b_prev = jnp.where(keep, pltpu.roll(b3, s, axis=1), 0.0)
        b3 = a3 * b_prev + b3
        a3 = a3 * a_prev
    a_ref[...] = a3.reshape(tc, w)
    b_ref[...] = b3.reshape(tc, w)

    def body(g, h):
        sl = pl.ds(pl.multiple_of(g * sub, sub), sub)
        hg = b_ref[sl, :] + a_ref[sl, :] * h
        b_ref[sl, :] = hg
        return jnp.broadcast_to(hg[sub - 1:sub, :], (sub, w))

    h_last = lax.fori_loop(0, tc // sub, body, hc_ref[...])
    hc_ref[...] = h_last
    lo_ref[...] = (jax.nn.gelu(yl_ref[...]) * b_ref[...]).astype(lo_ref.dtype)

    @pl.when(c == n_chunks - 1)
    def _():
        h_out_ref[0] = h_last[0:1, :]
        conv_out_ref[0] = xs_ref[sub - (CONV_W - 1):sub, :]


def _lru_prompt(proj, b, t, w, cw, cb, wa, ba, wx, bx, sp):
    tc = _pick_tile(t, 512, 16)
    nc = t // tc
    xl_col = (proj.shape[1] - 2 * w) // w
    vec = lambda: pl.BlockSpec((1, w), lambda bi, ci: (0, 0))
    blocks = lambda: pl.BlockSpec(wa.shape, lambda bi, ci: (0, 0, 0))
    vmem = 4 * tc * w * 4 + 2 * tc * w * 2 + 3 * tc * w * 4 + 8 * tc * w * 4
    return pl.pallas_call(
        functools.partial(_lru_prompt_kernel, n_chunks=nc),
        grid=(b, nc),
        in_specs=[
            pl.BlockSpec((tc, w), lambda bi, ci: (bi * nc + ci, xl_col)),
            pl.BlockSpec((tc, w), lambda bi, ci: (bi * nc + ci, xl_col + 1)),
            pl.BlockSpec((CONV_W, w), lambda bi, ci: (0, 0)),
            vec(), blocks(), vec(), blocks(), vec(), vec(),
        ],
        out_specs=[
            pl.BlockSpec((tc, w), lambda bi, ci: (bi * nc + ci, 0)),
            pl.BlockSpec((1, 1, w), lambda bi, ci: (bi, 0, 0)),
            pl.BlockSpec((1, CONV_W - 1, w), lambda bi, ci: (bi, 0, 0)),
        ],
        out_shape=[jax.ShapeDtypeStruct((b * t, w), BF16),
                   jax.ShapeDtypeStruct((b, 1, w), F32),
                   jax.ShapeDtypeStruct((b, CONV_W - 1, w), F32)],
        scratch_shapes=[pltpu.VMEM((tc + V7X_SUBLANES, w), F32),
                        pltpu.VMEM((tc, w), F32), pltpu.VMEM((tc, w), F32),
                        pltpu.VMEM((V7X_SUBLANES, w), F32)],
        compiler_params=_params(("parallel", "arbitrary"), vmem),
        name="lru_prompt",
    )(proj, proj, cw, cb.reshape(1, w), wa, ba.reshape(1, w), wx, bx.reshape(1, w), sp)


def _lru_sample_kernel(xl_ref, yl_ref, conv0_ref, h0_ref, cw_ref, cb_ref, wa_ref, ba_ref,
                       wx_ref, bx_ref, sp_ref, lo_ref, h_out_ref, conv_out_ref, *, dt):
    db = h0_ref.shape[1]
    w = h0_ref.shape[2]
    taps = CONV_W - 1
    xp = [conv0_ref[0, :, i * w:(i + 1) * w] for i in range(taps)]
    xp += [xl_ref[t * db:(t + 1) * db, :] for t in range(dt)]
    xcs = []
    for t in range(dt):
        xc = cb_ref[...] + xp[t] * cw_ref[0:1, :]
        for i in range(1, CONV_W):
            xc = xc + xp[t + i] * cw_ref[i:i + 1, :]
        xcs.append(xc)
    a, bt = _lru_gates(jnp.concatenate(xcs, axis=0), wa_ref, ba_ref, wx_ref, bx_ref, sp_ref)
    h = h0_ref[0]
    for t in range(dt):
        rows = slice(t * db, (t + 1) * db)
        h = a[rows] * h + bt[rows]
        lo_ref[rows, :] = (jax.nn.gelu(yl_ref[rows, :]) * h).astype(lo_ref.dtype)
    h_out_ref[...] = h
    for i in range(taps):
        conv_out_ref[:, i * w:(i + 1) * w] = xp[dt + i]


def _lru_sample(proj, state_conv, state_lru, e, db, dt, w, cw, cb, wa, ba, wx, bx, sp):
    rows = db * dt
    assert proj.shape[0] == rows
    rb = 0
    xl_col = (proj.shape[1] - 2 * w) // w
    taps = CONV_W - 1
    conv0 = state_conv.reshape(state_conv.shape[0], db, taps * w)
    vec = lambda: pl.BlockSpec((1, w), lambda i: (0, 0))
    blocks = lambda: pl.BlockSpec(wa.shape, lambda i: (0, 0, 0))
    vmem = 16 * rows * w * 4
    lo, h_new, conv_new = pl.pallas_call(
        functools.partial(_lru_sample_kernel, dt=dt),
        grid=(1,),
        in_specs=[
            pl.BlockSpec((rows, w), lambda i: (rb, xl_col)),
            pl.BlockSpec((rows, w), lambda i: (rb, xl_col + 1)),
            pl.BlockSpec((1, db, taps * w), lambda i: (e, 0, 0)),
            pl.BlockSpec((1, db, w), lambda i: (e, 0, 0)),
            pl.BlockSpec((CONV_W, w), lambda i: (0, 0)),
            vec(), blocks(), vec(), blocks(), vec(), vec(),
        ],
        out_specs=[
            pl.BlockSpec((rows, w), lambda i: (0, 0)),
            pl.BlockSpec((db, w), lambda i: (0, 0)),
            pl.BlockSpec((db, taps * w), lambda i: (0, 0)),
        ],
        out_shape=[jax.ShapeDtypeStruct((rows, w), BF16),
                   jax.ShapeDtypeStruct((db, w), F32),
                   jax.ShapeDtypeStruct((db, taps * w), F32)],
        compiler_params=_params(("arbitrary",), vmem),
        name="lru_sample",
    )(proj, proj, conv0, state_lru, cw, cb.reshape(1, w), wa, ba.reshape(1, w), wx,
      bx.reshape(1, w), sp)
    return lo, h_new, conv_new.reshape(db, taps, w)


def _s5_tables(a_re, a_im, b_re, b_im, c_re, c_im, d, log_dt):
    g = a_re.shape[0]
    nb = g // SSM_BLOCK_GROUPS
    dt = jnp.exp(log_dt)[:, None]
    mag = jnp.exp(a_re * dt)
    abr = mag * jnp.cos(a_im * dt)
    abi = mag * jnp.sin(a_im * dt)
    den = a_re * a_re + a_im * a_im
    nr, ni = abr - 1.0, abi
    fr = (nr * a_re + ni * a_im) / den
    fi = (ni * a_re - nr * a_im) / den
    bbr = fr[..., None] * b_re - fi[..., None] * b_im
    bbi = fr[..., None] * b_im + fi[..., None] * b_re
    per_lane_block = V7X_LANES // SSM_P
    owner = (jnp.arange(SSM_BLOCK_GROUPS) % per_lane_block)[None, :, None, None, None]
    slot = jnp.arange(per_lane_block)[None, None, None, :, None]

    def pack(gkp):
        gkp = gkp.reshape(nb, SSM_BLOCK_GROUPS, SSM_GROUP, 1, SSM_P)
        return jnp.where(owner == slot, gkp, 0.0).reshape(nb, V7X_MXU_DIM, V7X_LANES)

    p_in = jnp.stack([pack(bbr.transpose(0, 2, 1)), pack(bbi.transpose(0, 2, 1))])
    p_out = jnp.stack([pack(c_re), pack(-c_im)])
    return (abr.reshape(nb, SSM_BLOCK_STATES), abi.reshape(nb, SSM_BLOCK_STATES),
            p_in, p_out, d.reshape(nb, 1, V7X_MXU_DIM))


def _expand_blockdiag(packed_ref, i, w_ref):
    ln = V7X_LANES
    per_lane_block = ln // SSM_P
    w_ref[...] = jnp.zeros_like(w_ref)
    for c in range(2):
        for g in range(SSM_BLOCK_GROUPS):
            rows = slice(g * SSM_GROUP, (g + 1) * SSM_GROUP)
            col = c * SSM_BLOCK_STATES + (g // per_lane_block) * ln
            w_ref[rows, col:col + ln] = packed_ref[c, i, rows, :].astype(w_ref.dtype)


def _s5_prompt_kernel(x_ref, g_ref, ar_ref, ai_ref, pin_ref, pout_ref, d_ref, y_ref,
                      hr_out_ref, hi_out_ref, win_ref, wout_ref, u_ref, sr_ref, si_ref,
                      hr_ref, hi_ref, *, n_chunks):
    c = pl.program_id(1)
    tc = x_ref.shape[0]
    nb = win_ref.shape[0]
    ns = SSM_BLOCK_STATES
    bw = V7X_MXU_DIM
    ln = V7X_LANES
    slots = V7X_SUBLANES
    nl = sr_ref.shape[0]
    splits = ns // (nl * ln)
    blocks_per_pass = slots // splits
    pitch = S5_ROW_PITCH

    @pl.when(c == 0)
    def _():
        hr_ref[...] = jnp.zeros_like(hr_ref)
        hi_ref[...] = jnp.zeros_like(hi_ref)
        for i in range(nb):
            _expand_blockdiag(pin_ref, i, win_ref.at[i])
            _expand_blockdiag(pout_ref, i, wout_ref.at[i])

    u_ref[...] = _rms(x_ref[...], g_ref[...])

    def slot_rows(slot):
        return pl.ds(slot, tc, stride=pitch)

    for p in range(nb // blocks_per_pass):
        blocks = range(p * blocks_per_pass, (p + 1) * blocks_per_pass)
        for il, i in enumerate(blocks):
            bu = _bdot(u_ref[:, i * bw:(i + 1) * bw].astype(BF16), win_ref[i])
            for sp in range(splits):
                for l in range(nl):
                    col = (sp * nl + l) * ln
                    sr_ref[l, slot_rows(il * splits + sp), :] = bu[:, col:col + ln]
                    si_ref[l, slot_rows(il * splits + sp), :] = bu[:, ns + col:ns + col + ln]

        srows = slice(p * slots, (p + 1) * slots)
        ar = [ar_ref[srows, l * ln:(l + 1) * ln] for l in range(nl)]
        ai = [ai_ref[srows, l * ln:(l + 1) * ln] for l in range(nl)]

        def body(t, carry):
            rows = pl.ds(t * pitch, slots)
            out = []
            for l in range(nl):
                hr, hi = carry[l]
                hr_n = ar[l] * hr - ai[l] * hi + sr_ref[l, rows, :]
                hi_n = ar[l] * hi + ai[l] * hr + si_ref[l, rows, :]
                sr_ref[l, rows, :] = hr_n
                si_ref[l, rows, :] = hi_n
                out.append((hr_n, hi_n))
            return tuple(out)

        init = tuple((hr_ref[srows, l * ln:(l + 1) * ln], hi_ref[srows, l * ln:(l + 1) * ln])
                     for l in range(nl))
        last = lax.fori_loop(0, tc, body, init, unroll=16)
        for l in range(nl):
            hr_ref[srows, l * ln:(l + 1) * ln] = last[l][0]
            hi_ref[srows, l * ln:(l + 1) * ln] = last[l][1]

        for il, i in enumerate(blocks):
            parts = [ref[l, slot_rows(il * splits + sp), :].astype(BF16)
                     for ref in (sr_ref, si_ref) for sp in range(splits) for l in range(nl)]
            cols = slice(i * bw, (i + 1) * bw)
            y = _bdot_nt(jnp.concatenate(parts, axis=-1), wout_ref[i]) + d_ref[i] * u_ref[:, cols]
            y_ref[:, cols] = jax.nn.gelu(y).astype(y_ref.dtype)

    @pl.when(c == n_chunks - 1)
    def _():
        hr_out_ref[0] = hr_ref[...]
        hi_out_ref[0] = hi_ref[...]


def _s5_prompt(x, g, tables, b, t):
    d = x.shape[1]
    abr, abi, p_in, p_out, dd = tables
    nb = abr.shape[0]
    ns = SSM_BLOCK_STATES
    w_shape = (nb, V7X_MXU_DIM, 2 * ns)
    tc = _pick_tile(t, 256, 16)
    nc = t // tc
    passes = 2
    assert nb % passes == 0 and V7X_SUBLANES % (nb // passes) == 0
    splits = V7X_SUBLANES // (nb // passes)
    slot_lanes = ns // splits
    nl = slot_lanes // V7X_LANES
    abr = abr.reshape(nb * splits, slot_lanes)
    abi = abi.reshape(nb * splits, slot_lanes)
    const = lambda a: _once(a.shape, lambda bi, ci: (0,) * a.ndim)
    scan_bytes = nl * tc * S5_ROW_PITCH * V7X_LANES * 4
    w_bytes = w_shape[0] * w_shape[1] * w_shape[2] * 2
    vmem = (5 * tc * d * 4 + 2 * tc * d * 2 + 2 * w_bytes + 2 * p_in.size * 4
            + 2 * scan_bytes + 6 * tc * 2 * ns * 4)
    y, hr, hi = pl.pallas_call(
        functools.partial(_s5_prompt_kernel, n_chunks=nc),
        grid=(b, nc),
        in_specs=[
            pl.BlockSpec((tc, d), lambda bi, ci: (bi * nc + ci, 0)),
            pl.BlockSpec((1, d), lambda bi, ci: (0, 0)),
            const(abr), const(abi), const(p_in), const(p_out), const(dd),
        ],
        out_specs=[
            pl.BlockSpec((tc, d), lambda bi, ci: (bi * nc + ci, 0)),
            pl.BlockSpec((1,) + abr.shape, lambda bi, ci: (bi, 0, 0)),
            pl.BlockSpec((1,) + abr.shape, lambda bi, ci: (bi, 0, 0)),
        ],
        out_shape=[jax.ShapeDtypeStruct((b * t, d), BF16),
                   jax.ShapeDtypeStruct((b,) + abr.shape, F32),
                   jax.ShapeDtypeStruct((b,) + abr.shape, F32)],
        scratch_shapes=[pltpu.VMEM(w_shape, BF16), pltpu.VMEM(w_shape, BF16),
                        pltpu.VMEM((tc, d), F32),
                        pltpu.VMEM((nl, tc * S5_ROW_PITCH, V7X_LANES), F32),
                        pltpu.VMEM((nl, tc * S5_ROW_PITCH, V7X_LANES), F32),
                        pltpu.VMEM(abr.shape, F32), pltpu.VMEM(abr.shape, F32)],
        compiler_params=_params(("parallel", "arbitrary"), vmem),
        name="s5_prompt",
    )(x, g.reshape(1, d), abr, abi, p_in, p_out, dd)
    return y, hr.reshape(b, nb * ns), hi.reshape(b, nb * ns)


def _s5_sample_kernel(x_ref, g_ref, ar_ref, ai_ref, pin_ref, pout_ref, d_ref, h0r_ref,
                      h0i_ref, y_ref, hr_out_ref, hi_out_ref, win_ref, wout_ref, uf_ref,
                      ub_ref, *, dt):
    i = pl.program_id(0)
    nb = uf_ref.shape[0]
    bw = V7X_MXU_DIM
    ns = SSM_BLOCK_STATES
    db = h0r_ref.shape[1]

    @pl.when(i == 0)
    def _():
        u = _rms(x_ref[...], g_ref[...])
        for n in range(nb):
            uf_ref[n] = u[:, n * bw:(n + 1) * bw]
            ub_ref[n] = u[:, n * bw:(n + 1) * bw].astype(BF16)

    _expand_blockdiag(pin_ref, 0, win_ref)
    _expand_blockdiag(pout_ref, 0, wout_ref)
    bu = _bdot(ub_ref[i], win_ref[...])
    ar, ai = ar_ref[pl.ds(i, 1), :], ai_ref[pl.ds(i, 1), :]
    hr, hi = h0r_ref[0], h0i_ref[0]
    states = []
    for t in range(dt):
        rows = slice(t * db, (t + 1) * db)
        hr, hi = (ar * hr - ai * hi + bu[rows, :ns], ar * hi + ai * hr + bu[rows, ns:])
        states.append(jnp.concatenate([hr.astype(BF16), hi.astype(BF16)], axis=-1))
    y = _bdot_nt(jnp.concatenate(states, axis=0), wout_ref[...]) + d_ref[0] * uf_ref[i]
    y_ref[...] = jax.nn.gelu(y).astype(y_ref.dtype)
    hr_out_ref[...] = hr
    hi_out_ref[...] = hi


def _s5_sample(x, g, tables, state_re, state_im, o, db, dt):
    d = x.shape[1]
    abr, abi, p_in, p_out, dd = tables
    nb = abr.shape[0]
    ns = SSM_BLOCK_STATES
    bw = V7X_MXU_DIM
    rows = db * dt
    assert x.shape[0] == rows
    rb = 0
    no = state_re.shape[0]
    h0r = state_re.reshape(no, db, nb * ns)
    h0i = state_im.reshape(no, db, nb * ns)
    blk = lambda a: pl.BlockSpec((1,) + a.shape[1:], lambda i: (i, 0, 0))
    packed = lambda a: pl.BlockSpec((2, 1) + a.shape[2:], lambda i: (0, i, 0, 0))
    vmem = (2 * rows * d * 4 + rows * d * 6 + 4 * bw * 2 * ns * 2 + 8 * db * ns * 4
            + 8 * rows * 2 * ns * 4)
    return pl.pallas_call(
        functools.partial(_s5_sample_kernel, dt=dt),
        grid=(nb,),
        in_specs=[
            _once((rows, d), lambda i: (rb, 0)),
            pl.BlockSpec((1, d), lambda i: (0, 0)),
            pl.BlockSpec(abr.shape, lambda i: (0, 0)),
            pl.BlockSpec(abi.shape, lambda i: (0, 0)),
            packed(p_in), packed(p_out), blk(dd),
            pl.BlockSpec((1, db, ns), lambda i: (o, 0, i)),
            pl.BlockSpec((1, db, ns), lambda i: (o, 0, i)),
        ],
        out_specs=[
            pl.BlockSpec((rows, bw), lambda i: (0, i)),
            pl.BlockSpec((db, ns), lambda i: (0, i)),
            pl.BlockSpec((db, ns), lambda i: (0, i)),
        ],
        out_shape=[jax.ShapeDtypeStruct((rows, d), BF16),
                   jax.ShapeDtypeStruct((db, nb * ns), F32),
                   jax.ShapeDtypeStruct((db, nb * ns), F32)],
        scratch_shapes=[pltpu.VMEM((bw, 2 * ns), BF16), pltpu.VMEM((bw, 2 * ns), BF16),
                        pltpu.VMEM((nb, rows, bw), F32), pltpu.VMEM((nb, rows, bw), BF16)],
        compiler_params=_params(("arbitrary",), vmem),
        name="s5_sample",
    )(x, g.reshape(1, d), abr, abi, p_in, p_out, dd, h0r, h0i)


def kernel(x_prompt, x_sample, state_ret, state_lru, state_conv, state_ssm_re, state_ssm_im, norm_mix_even, w_in_even, lru_conv_w, lru_conv_b, lru_wa, lru_ba, lru_wx, lru_bx, lru_lambda, w_out_even, norm_mix_odd, ssm_a_re, ssm_a_im, ssm_b_re, ssm_b_im, ssm_c_re, ssm_c_im, ssm_d, ssm_log_dt, w_glu, b_glu, norm_ffn, w_ffn_gu, w_ffn_down, norm_final):
    b, t, d = x_prompt.shape
    db, dt, _ = x_sample.shape
    depth = norm_ffn.shape[0]
    half = d // 2
    past_len = PAST_LEN
    groups, ssm_p = ssm_a_re.shape[1:]
    assert ssm_p == SSM_P and groups * SSM_GROUP == d and groups % SSM_BLOCK_GROUPS == 0

    xp = x_prompt.reshape(b * t, d)
    xs = x_sample.transpose(1, 0, 2).reshape(dt * db, d)

    rets_p, ret_s, lrus_p, lrus_s, convs_p, convs_s = [], None, [], [], [], []
    sres_p, sres_s, sims_p, sims_s = [], [], [], []
    for layer in range(depth):
        if layer % 2 == 0:
            e = layer // 2
            sp = jax.nn.softplus(-lru_lambda[e]).reshape(1, half)
            lru_w = (lru_conv_w[e], lru_conv_b[e], lru_wa[e], lru_ba[e], lru_wx[e], lru_bx[e], sp)

            proj_p = _inproj(xp, norm_mix_even[e], w_in_even, e)
            o_p, ret_p = _ret_prompt(proj_p, b, t, half)
            lo_p, lru_p, conv_p = _lru_prompt(proj_p, b, t, half, *lru_w)
            xp = _outproj(o_p, lo_p, w_out_even, e, xp)

            proj_s = _inproj(xs, norm_mix_even[e], w_in_even, e)
            o_s, ret_s = _ret_sample(proj_s, state_ret, e, ret_s, db, dt, half, past_len)
            lo_s, lru_s, conv_s = _lru_sample(proj_s, state_conv, state_lru, e, db, dt, half,
                                              *lru_w)
            xs = _outproj(o_s, lo_s, w_out_even, e, xs)

            rets_p.append(ret_p)
            lrus_p.append(lru_p.reshape(b, half))
            lrus_s.append(lru_s)
            convs_p.append(conv_p)
            convs_s.append(conv_s)
        else:
            o = layer // 2
            tables = _s5_tables(ssm_a_re[o], ssm_a_im[o], ssm_b_re[o], ssm_b_im[o],
                                ssm_c_re[o], ssm_c_im[o], ssm_d[o], ssm_log_dt[o])
            y_p, sre_p, sim_p = _s5_prompt(xp, norm_mix_odd[o], tables, b, t)
            xp = _glu(y_p, w_glu, o, b_glu[o], xp)
            y_s, sre_s, sim_s = _s5_sample(xs, norm_mix_odd[o], tables, state_ssm_re,
                                           state_ssm_im, o, db, dt)
            xs = _glu(y_s, w_glu, o, b_glu[o], xs)
            sres_p.append(sre_p.reshape(b, groups, ssm_p))
            sims_p.append(sim_p.reshape(b, groups, ssm_p))
            sres_s.append(sre_s.reshape(db, groups, ssm_p))
            sims_s.append(sim_s.reshape(db, groups, ssm_p))
        g_final = norm_final if layer == depth - 1 else None
        xp = _ffn(xp, norm_ffn[layer], w_ffn_gu, w_ffn_down, layer, g_final)
        xs = _ffn(xs, norm_ffn[layer], w_ffn_gu, w_ffn_down, layer, g_final)

    y_prompt = xp.reshape(b, t, d)
    y_sample = xs.reshape(dt, db, d).transpose(1, 0, 2)
    return (y_prompt, y_sample, jnp.stack(rets_p), ret_s, jnp.stack(lrus_p),
            jnp.stack(lrus_s), jnp.stack(convs_p), jnp.stack(convs_s), jnp.stack(sres_p),
            jnp.stack(sres_s), jnp.stack(sims_p), jnp.stack(sims_s))
```

```python
import functools

import jax
import jax.numpy as jnp
import numpy as np
from jax import lax
from jax.experimental import pallas as pl
from jax.experimental.pallas import tpu as pltpu

F32 = jnp.float32
BF16 = jnp.bfloat16

EPS = 1e-6
PAST_LEN = 16384
ROPE_BASE = 10000.0
RET_HEADS = 4
RET_CHUNK = 128
LRU_HEADS = 8
LRU_C = 8.0
CONV_W = 4
SSM_GROUP = 16
SSM_P = 64
SSM_CHUNK = 128

V7X_SUBLANES = 8
V7X_LANES = 128
V7X_MXU_DIM = 256
V7X_VMEM_BYTES = 64 * 1024 * 1024
VMEM_LIMIT_CAP = V7X_VMEM_BYTES - 6 * 1024 * 1024

DENSE_ROW_TILE = 2048

TILE_COPY_CHUNKS = 8

SSM_BLOCK_GROUPS = V7X_MXU_DIM // SSM_GROUP
SSM_BLOCK_STATES = SSM_BLOCK_GROUPS * SSM_P

RET_CHUNKS_PER_STEP = 4

RET_SAMPLE_ROWS = 128

S5_ROW_PITCH = 12


def _pick_tile(n, target, mult):
    best = None
    for t in range(mult, min(n, target) + 1, mult):
        if n % t == 0:
            best = t
    assert best is not None, (n, target, mult)
    return best


def _params(semantics, vmem_bytes):
    limit = min(int(vmem_bytes * 1.2) + (6 << 20), VMEM_LIMIT_CAP)
    return pltpu.CompilerParams(dimension_semantics=semantics, vmem_limit_bytes=limit)


def _once(block_shape, index_map):
    return pl.BlockSpec(block_shape, index_map, pipeline_mode=pl.Buffered(1))


def _rms(x, g):
    return x * lax.rsqrt(jnp.mean(x * x, axis=-1, keepdims=True) + EPS) * g


def _norm_rows_into(x_ref, g_ref, xn_ref):
    tm = x_ref.shape[0]
    rows = _pick_tile(tm, 64, 16)

    def body(r, carry):
        sl = pl.ds(pl.multiple_of(r * rows, rows), rows)
        xn_ref[sl, :] = _rms(x_ref[sl, :], g_ref[...]).astype(xn_ref.dtype)
        return carry

    lax.fori_loop(0, tm // rows, body, 0)


def _row_chunk_copies(src, dst, sems):
    n = sems.shape[0]
    rc = src.shape[0] // n
    return [pltpu.make_async_copy(src.at[pl.ds(c * rc, rc), :], dst.at[pl.ds(c * rc, rc), :],
                                  sems.at[c]) for c in range(n)]


def _load_and_norm_rows(x_hbm_rows, x_ref, g_ref, xn_ref, sems):
    copies = _row_chunk_copies(x_hbm_rows, x_ref, sems)
    rc = x_ref.shape[0] // len(copies)
    for cp in copies:
        cp.start()
    for c, cp in enumerate(copies):
        cp.wait()
        rows = pl.ds(c * rc, rc)
        _norm_rows_into(x_ref.at[rows, :], g_ref, xn_ref.at[rows, :])


def _bdot(a, b):
    return jnp.dot(a, b, preferred_element_type=F32)


def _bdot_nt(a, b):
    return lax.dot_general(a, b, (((1,), (1,)), ((), ())), preferred_element_type=F32)


def _inproj_kernel(x_hbm, g_ref, w_ref, o_ref, x_ref, xn_ref, x_sems):
    @pl.when(pl.program_id(1) == 0)
    def _():
        tm = x_ref.shape[0]
        tile = pl.ds(pl.multiple_of(pl.program_id(0) * tm, tm), tm)
        _load_and_norm_rows(x_hbm.at[tile, :], x_ref, g_ref, xn_ref, x_sems)

    o_ref[...] = _bdot(xn_ref[...], w_ref[...].astype(BF16))


def _inproj(x, g, w, e):
    m, d = x.shape
    n = w.shape[2]
    tm = _pick_tile(m, DENSE_ROW_TILE, 16)
    tn = _pick_tile(n, 512, V7X_LANES)
    vmem = tm * d * 4 + tm * d * 2 + 2 * d * tn * 4 + 2 * tm * tn * 4 + d * tn * 2
    return pl.pallas_call(
        _inproj_kernel,
        grid=(m // tm, n // tn),
        in_specs=[
            pl.BlockSpec(memory_space=pl.ANY),
            _once((1, d), lambda i, j: (0, 0)),
            pl.BlockSpec((None, d, tn), lambda i, j: (e, 0, j)),
        ],
        out_specs=pl.BlockSpec((tm, tn), lambda i, j: (i, j)),
        out_shape=jax.ShapeDtypeStruct((m, n), F32),
        scratch_shapes=[pltpu.VMEM((tm, d), F32), pltpu.VMEM((tm, d), BF16),
                        pltpu.SemaphoreType.DMA((TILE_COPY_CHUNKS,))],
        compiler_params=_params(("parallel", "arbitrary"), vmem),
        name="inproj",
    )(x, g.reshape(1, d), w)


def _outproj_kernel(o_ref, lo_ref, wo_ref, wl_ref, x_ref, out_ref):
    acc = _bdot(o_ref[...], wo_ref[...].astype(BF16))
    acc = acc + _bdot(lo_ref[...], wl_ref[...].astype(BF16))
    out_ref[...] = x_ref[...] + acc


def _outproj(o, lo, w, e, x):
    m, d = x.shape
    half = o.shape[1]
    tm = _pick_tile(m, DENSE_ROW_TILE, 16)
    tn = _pick_tile(d, 512, V7X_LANES)
    vmem = 4 * tm * half * 2 + 4 * half * tn * 4 + 4 * tm * tn * 4 + 2 * half * tn * 2
    return pl.pallas_call(
        _outproj_kernel,
        grid=(m // tm, d // tn),
        in_specs=[
            pl.BlockSpec((tm, half), lambda i, j: (i, 0)),
            pl.BlockSpec((tm, half), lambda i, j: (i, 0)),
            pl.BlockSpec((None, half, tn), lambda i, j: (e, 0, j)),
            pl.BlockSpec((None, half, tn), lambda i, j: (e, 1, j)),
            pl.BlockSpec((tm, tn), lambda i, j: (i, j)),
        ],
        out_specs=pl.BlockSpec((tm, tn), lambda i, j: (i, j)),
        out_shape=jax.ShapeDtypeStruct((m, d), F32),
        compiler_params=_params(("parallel", "arbitrary"), vmem),
        name="outproj",
    )(o, lo, w, w, x)


def _glu_kernel(y_ref, w1_ref, w2_ref, b1_ref, b2_ref, x_ref, out_ref):
    y = y_ref[...]
    z1 = _bdot(y, w1_ref[...].astype(BF16)) + b1_ref[...]
    z2 = _bdot(y, w2_ref[...].astype(BF16)) + b2_ref[...]
    out_ref[...] = x_ref[...] + z1 * jax.nn.sigmoid(z2)


def _glu(y, w, o, b, x):
    m, d = x.shape
    tm = _pick_tile(m, DENSE_ROW_TILE, 16)

    tn = _pick_tile(d, 256, V7X_LANES)
    nj = d // tn
    vmem = 2 * tm * d * 2 + 4 * d * tn * 4 + 4 * tm * tn * 4 + 2 * d * tn * 2 + 2 * tm * tn * 4
    return pl.pallas_call(
        _glu_kernel,
        grid=(m // tm, nj),
        in_specs=[
            pl.BlockSpec((tm, d), lambda i, j: (i, 0)),
            pl.BlockSpec((None, d, tn), lambda i, j: (o, 0, j)),
            pl.BlockSpec((None, d, tn), lambda i, j: (o, 0, nj + j)),
            pl.BlockSpec((1, tn), lambda i, j: (0, j)),
            pl.BlockSpec((1, tn), lambda i, j: (0, nj + j)),
            pl.BlockSpec((tm, tn), lambda i, j: (i, j)),
        ],
        out_specs=pl.BlockSpec((tm, tn), lambda i, j: (i, j)),
        out_shape=jax.ShapeDtypeStruct((m, d), F32),
        compiler_params=_params(("parallel", "arbitrary"), vmem),
        name="glu",
    )(y, w, w, b.reshape(1, 2 * d), b.reshape(1, 2 * d), x)


def _ffn_kernel(x_hbm, g_ref, gf_ref, wg_ref, wu_ref, wd_ref, out_ref, xn_ref, x_sems, *,
                final_norm):
    i, j = pl.program_id(0), pl.program_id(1)

    @pl.when(j == 0)
    def _():
        tm = out_ref.shape[0]
        tile = pl.ds(pl.multiple_of(i * tm, tm), tm)
        _load_and_norm_rows(x_hbm.at[tile, :], out_ref, g_ref, xn_ref, x_sems)

    xn = xn_ref[...]
    gate = _bdot(xn, wg_ref[...].astype(BF16))
    up = _bdot(xn, wu_ref[...].astype(BF16))
    act = (gate * jax.nn.sigmoid(gate) * up).astype(BF16)
    out_ref[...] += _bdot(act, wd_ref[...].astype(BF16))

    if final_norm:
        @pl.when(j == pl.num_programs(1) - 1)
        def _():
            _norm_rows_into(out_ref, gf_ref, out_ref)


def _ffn(x, g, w_gu, w_down, layer, g_final=None):
    m, d = x.shape
    hidden = w_down.shape[1]
    tm = _pick_tile(m, DENSE_ROW_TILE, 16)

    th = _pick_tile(hidden, 256, V7X_LANES)
    nh = hidden // th
    vmem = (tm * d * 4 + tm * d * 2 + 6 * d * th * 4 + 3 * d * th * 2
            + 3 * tm * th * 4)
    gf = g if g_final is None else g_final
    return pl.pallas_call(
        functools.partial(_ffn_kernel, final_norm=g_final is not None),
        grid=(m // tm, nh),
        in_specs=[
            pl.BlockSpec(memory_space=pl.ANY),
            _once((1, d), lambda i, j: (0, 0)),
            _once((1, d), lambda i, j: (0, 0)),
            pl.BlockSpec((None, d, th), lambda i, j: (layer, 0, j)),
            pl.BlockSpec((None, d, th), lambda i, j: (layer, 0, nh + j)),
            pl.BlockSpec((None, th, d), lambda i, j: (layer, j, 0)),
        ],
        out_specs=_once((tm, d), lambda i, j: (i, 0)),
        out_shape=jax.ShapeDtypeStruct((m, d), F32),
        scratch_shapes=[pltpu.VMEM((tm, d), BF16),
                        pltpu.SemaphoreType.DMA((TILE_COPY_CHUNKS,))],
        compiler_params=_params(("parallel", "arbitrary"), vmem),
        name="ffn",
    )(x, g.reshape(1, d), gf.reshape(1, d), w_gu, w_gu, w_down)


def _rope_tables(pos, half):
    inv = 1.0 / np.power(ROPE_BASE, np.linspace(0.0, 1.0, half))
    ang = np.asarray(pos, np.float64)[:, None] * inv[None, :]
    return np.cos(ang).astype(np.float32), np.sin(ang).astype(np.float32)


def _decay_tables(c, dk):
    log_g = np.log1p(-np.exp2(-5.0 - np.arange(RET_HEADS, dtype=np.float64)))
    idx = np.arange(c, dtype=np.float64)
    diff = idx[:, None] - idx[None, :]
    mask = np.where(diff[None] >= 0,
                    np.exp(np.maximum(diff, 0.0)[None] * log_g[:, None, None]), 0.0)
    q_dec = np.exp((idx[None, :] + 1.0) * log_g[:, None])
    k_dec = np.exp((c - 1.0 - idx)[None, :] * log_g[:, None])
    chunk_dec = np.exp(c * log_g)
    q_dec = np.broadcast_to(q_dec[:, :, None], (RET_HEADS, c, dk))
    k_dec = np.broadcast_to(k_dec[:, :, None], (RET_HEADS, c, dk))
    chunk_dec = np.broadcast_to(chunk_dec[:, None, None], (RET_HEADS, 1, dk))
    f32 = lambda a: np.ascontiguousarray(a, dtype=np.float32)
    return f32(mask), f32(q_dec), f32(k_dec), f32(chunk_dec)


def _rotate(x, cos, sin):
    half = x.shape[-1] // 2
    x1, x2 = x[:, :half], x[:, half:]
    return jnp.concatenate([x1 * cos - x2 * sin, x2 * cos + x1 * sin], axis=-1)


def _ret_prompt_body(q_ref, k_ref, v_ref, g_ref, cos_ref, sin_ref, mask_ref, qd_ref,
                     kd_ref, cd_ref, o_ref, s_ref, *, scale):
    heads, dk, _ = s_ref.shape

    chunk = mask_ref.shape[1]
    for h in range(heads):
        cols = slice(h * dk, (h + 1) * dk)
        s = s_ref[h]
        for cc in range(q_ref.shape[0] // chunk):
            rows = slice(cc * chunk, (cc + 1) * chunk)
            cos, sin = cos_ref[rows, :], sin_ref[rows, :]
            q = _rotate(q_ref[rows, cols], cos, sin)
            k = _rotate(k_ref[rows, cols], cos, sin) * scale
            qb, kb, vb = q.astype(BF16), k.astype(BF16), v_ref[rows, cols].astype(BF16)
            scores = lax.dot_general(qb, kb, (((1,), (1,)), ((), ())),
                                     preferred_element_type=F32) * mask_ref[h]
            o = _bdot(scores.astype(BF16), vb) + _bdot(qb, s.astype(BF16)) * qd_ref[h]
            kdb = (k * kd_ref[h]).astype(BF16)
            s = s * cd_ref[h] + lax.dot_general(kdb, vb, (((0,), (0,)), ((), ())),
                                                preferred_element_type=F32)
            o = o * lax.rsqrt(jnp.mean(o * o, axis=-1, keepdims=True) + EPS)
            gr = g_ref[rows, cols]
            o_ref[rows, cols] = (o * (gr * jax.nn.sigmoid(gr))).astype(o_ref.dtype)
        s_ref[h] = s


def _mix_prompt(proj, b, t, half, cw, cb, wa, ba, wx, bx, sp):
    h = RET_HEADS
    w = half
    dk = half // h
    assert t % RET_CHUNK == 0
    c = _pick_tile(t, RET_CHUNKS_PER_STEP * RET_CHUNK, RET_CHUNK)
    nc = t // c
    cos, sin = _rope_tables(np.arange(t), dk // 2)
    tables = _decay_tables(RET_CHUNK, dk)

    def col(off):
        return pl.BlockSpec((c, half), lambda bi, ci: (bi * nc + ci, off))

    const = lambda a: pl.BlockSpec(a.shape, lambda bi, ci: (0,) * a.ndim)
    vec = lambda: pl.BlockSpec((1, w), lambda bi, ci: (0, 0))
    rows_out = lambda: pl.BlockSpec((c, half), lambda bi, ci: (bi * nc + ci, 0))
    vmem = (30 * c * half * 4 + 3 * h * dk * dk * 4 + sum(2 * a.size * 4 for a in tables)
            + 4 * wa.size * 4)
    return pl.pallas_call(
        functools.partial(_mix_prompt_kernel, n_chunks=nc, scale=dk ** -0.5),
        grid=(b, nc),
        in_specs=[
            col(0), col(1), col(2), col(3),
            pl.BlockSpec((c, dk // 2), lambda bi, ci: (ci, 0)),
            pl.BlockSpec((c, dk // 2), lambda bi, ci: (ci, 0)),
            *[const(a) for a in tables],
            col(4), col(5), const(cw), vec(), const(wa), vec(), const(wx), vec(), vec(),
        ],
        out_specs=[
            rows_out(),
            pl.BlockSpec((1, h, dk, dk), lambda bi, ci: (bi, 0, 0, 0)),
            rows_out(),
            pl.BlockSpec((1, 1, w), lambda bi, ci: (bi, 0, 0)),
            pl.BlockSpec((1, CONV_W - 1, w), lambda bi, ci: (bi, 0, 0)),
        ],
        out_shape=[jax.ShapeDtypeStruct((b * t, half), BF16),
                   jax.ShapeDtypeStruct((b, h, dk, dk), F32),
                   jax.ShapeDtypeStruct((b * t, w), BF16),
                   jax.ShapeDtypeStruct((b, 1, w), F32),
                   jax.ShapeDtypeStruct((b, CONV_W - 1, w), F32)],
        scratch_shapes=[pltpu.VMEM((h, dk, dk), F32),
                        pltpu.VMEM((c + V7X_SUBLANES, w), F32),
                        pltpu.VMEM((c, w), F32), pltpu.VMEM((c, w), F32),
                        pltpu.VMEM((V7X_SUBLANES, w), F32)],
        compiler_params=_params(("parallel", "arbitrary"), vmem),
        name="mix_prompt",
    )(proj, proj, proj, proj, cos, sin, *tables,
      proj, proj, cw, cb.reshape(1, w), wa, ba.reshape(1, w), wx, bx.reshape(1, w), sp)


def _ret_sample_kernel(*refs, bb, scale, chained):
    (q_ref, k_ref, v_ref, g_ref, cos_ref, sin_ref, mask_ref, qd_ref, kd_ref, cd_ref,
     s_in_ref) = refs[:11]
    o_ref, s_out_ref, kdt_ref, acc_ref = refs[12 if chained else 11:]

    for later in range(1, s_out_ref.shape[0]):
        s_out_ref[later] = jnp.zeros(s_out_ref.shape[1:], s_out_ref.dtype)
    dt, _, dk = q_ref.shape
    rows = dt * bb
    cos, sin = cos_ref[...], sin_ref[...]
    q = _rotate(q_ref[...].reshape(rows, dk), cos, sin)
    k = _rotate(k_ref[...].reshape(rows, dk), cos, sin) * scale
    qb, kb = q.astype(BF16), k.astype(BF16)
    vb = v_ref[...].reshape(rows, dk).astype(BF16)
    scores = lax.dot_general(qb, kb, (((1,), (1,)), ((), ())),
                             preferred_element_type=F32) * mask_ref[0]
    intra = _bdot(scores.astype(BF16), vb)
    kdt_ref[...] = (k * kd_ref[0]).T
    cd = cd_ref[0]
    row_batch = lax.broadcasted_iota(jnp.int32, (rows, 1), 0) % bb
    col_batch = lax.broadcasted_iota(jnp.int32, (1, rows), 1) % bb

    acc_ref[...] = jnp.zeros_like(acc_ref)

    def body(j, carry):
        s = s_in_ref[0, j, 0]
        acc_ref[...] = jnp.where(row_batch == j, _bdot(qb, s.astype(BF16)), acc_ref[...])
        kdt_j = jnp.where(col_batch == j, kdt_ref[...], 0.0).astype(BF16)
        s_out_ref[0, j, 0] = s * cd + _bdot(kdt_j, vb)
        return carry

    lax.fori_loop(0, bb, body, 0, unroll=4)
    o = intra + acc_ref[...] * qd_ref[0]
    o = o * lax.rsqrt(jnp.mean(o * o, axis=-1, keepdims=True) + EPS)
    gr = g_ref[...].reshape(rows, dk)
    o_ref[...] = (o * (gr * jax.nn.sigmoid(gr))).astype(o_ref.dtype).reshape(dt, bb, dk)


def _ret_sample(proj, state_ret, e, prev_states, db, dt, half, past_len):
    dk = half // RET_HEADS
    h = RET_HEADS
    assert RET_CHUNK % dt == 0 and RET_SAMPLE_ROWS % dt == 0
    rows = RET_SAMPLE_ROWS
    bb = rows // dt
    assert db % bb == 0
    proj3 = proj.reshape(dt, db, proj.shape[1])
    cos, sin = _rope_tables(past_len + np.arange(dt), dk // 2)
    cos, sin = np.repeat(cos, bb, axis=0), np.repeat(sin, bb, axis=0)
    mask, q_dec, k_dec, chunk_dec = _decay_tables(dt, dk)
    mask = np.einsum("hnm,ab->hnamb", mask, np.eye(bb, dtype=np.float32)).reshape(h, rows, rows)
    q_dec, k_dec = np.repeat(q_dec, bb, axis=1), np.repeat(k_dec, bb, axis=1)

    def col(off):
        return pl.BlockSpec((dt, bb, dk), lambda bi, hi: (0, bi, off + hi))

    def per_head(shape):
        return pl.BlockSpec((1,) + shape, lambda bi, hi: (hi, 0, 0))

    chained = prev_states is not None
    if chained:
        out_state_block = pl.BlockSpec((1, bb, 1, dk, dk), lambda bi, hi: (e, bi, hi, 0, 0))
    else:
        assert e == 0
        n_layers = state_ret.shape[0]
        out_state_block = pl.BlockSpec((n_layers, bb, 1, dk, dk),
                                       lambda bi, hi: (0, bi, hi, 0, 0))
    operands = [proj3, proj3, proj3, proj3, cos, sin, mask, q_dec, k_dec, chunk_dec, state_ret]
    in_specs = [
        col(0), col(h), col(2 * h), col(3 * h),
        pl.BlockSpec((rows, dk // 2), lambda bi, hi: (0, 0)),
        pl.BlockSpec((rows, dk // 2), lambda bi, hi: (0, 0)),
        per_head((rows, rows)), per_head((rows, dk)), per_head((rows, dk)),
        per_head((1, dk)),
        pl.BlockSpec((1, bb, 1, dk, dk), lambda bi, hi: (e, bi, hi, 0, 0)),
    ]
    if chained:
        operands.append(prev_states)
        in_specs.append(pl.BlockSpec(memory_space=pl.ANY))
    out_layers = out_state_block.block_shape[0]
    vmem = (2 + 2 * out_layers) * bb * dk * dk * 4 + 12 * rows * dk * 4 + 4 * dk * dk * 4
    o, states = pl.pallas_call(
        functools.partial(_ret_sample_kernel, bb=bb, scale=dk ** -0.5, chained=chained),
        grid=(db // bb, h),
        in_specs=in_specs,
        out_specs=[
            pl.BlockSpec((dt, bb, dk), lambda bi, hi: (0, bi, hi)),
            out_state_block,
        ],
        out_shape=[jax.ShapeDtypeStruct((dt, db, half), BF16),
                   jax.ShapeDtypeStruct(state_ret.shape, state_ret.dtype)],
        scratch_shapes=[pltpu.VMEM((dk, rows), F32), pltpu.VMEM((rows, dk), F32)],
        input_output_aliases={len(operands) - 1: 1} if chained else {},
        compiler_params=_params(("parallel", "parallel"), vmem),
        name="ret_sample",
    )(*operands)
    return o.reshape(dt * db, half), states


def _lru_gates(xc, wa_ref, ba_ref, wx_ref, bx_ref, sp_ref):
    xcb = xc.astype(BF16)
    blk = wa_ref.shape[1]
    ra, ri = [], []
    for n in range(wa_ref.shape[0]):
        xn = xcb[:, n * blk:(n + 1) * blk]
        ra.append(_bdot(xn, wa_ref[n].astype(BF16)))
        ri.append(_bdot(xn, wx_ref[n].astype(BF16)))
    r = jax.nn.sigmoid(jnp.concatenate(ra, axis=-1) + ba_ref[...])
    i = jax.nn.sigmoid(jnp.concatenate(ri, axis=-1) + bx_ref[...])
    log_a = -LRU_C * r * sp_ref[...]
    a = jnp.exp(log_a)
    gap = jnp.maximum(-jnp.tanh(log_a) * (a * a + 1.0), 0.0)
    mult = jnp.where(gap > 0.0, gap * lax.rsqrt(gap), 0.0)
    return a, mult * i * xc


def _lru_prompt_body(xl_ref, yl_ref, cw_ref, cb_ref, wa_ref, ba_ref, wx_ref, bx_ref,
                     sp_ref, lo_ref, xs_ref, a_ref, b_ref, hc_ref):
    tc, w = xl_ref.shape
    sub = V7X_SUBLANES

    x = xl_ref[...]
    xs_ref[sub:sub + tc, :] = x
    xc = cb_ref[...] + x * cw_ref[CONV_W - 1:CONV_W, :]
    for i in range(CONV_W - 1):
        back = CONV_W - 1 - i
        xc = xc + xs_ref[sub - back:sub - back + tc, :] * cw_ref[i:i + 1, :]
    xs_ref[0:sub, :] = xs_ref[tc:tc + sub, :]

    a, bt = _lru_gates(xc, wa_ref, ba_ref, wx_ref, bx_ref, sp_ref)

    a3 = a.reshape(tc // sub, sub, w)
    b3 = bt.reshape(tc // sub, sub, w)
    step = lax.broadcasted_iota(jnp.int32, (1, sub, 1), 1)
    for s in (1, 2, 4):
        keep = step >= s
        a_prev = jnp.where(keep, pltpu.roll(a3, s, axis=1), 1.0)
        b_prev = jnp.where(keep, pltpu.roll(b3, s, axis=1), 0.0)
        b3 = a3 * b_prev + b3
        a3 = a3 * a_prev
    a_ref[...] = a3.reshape(tc, w)
    b_ref[...] = b3.reshape(tc, w)

    def body(g, h):
        sl = pl.ds(pl.multiple_of(g * sub, sub), sub)
        hg = b_ref[sl, :] + a_ref[sl, :] * h
        b_ref[sl, :] = hg
        return jnp.broadcast_to(hg[sub - 1:sub, :], (sub, w))

    h_last = lax.fori_loop(0, tc // sub, body, hc_ref[...])
    hc_ref[...] = h_last
    lo_ref[...] = (jax.nn.gelu(yl_ref[...]) * b_ref[...]).astype(lo_ref.dtype)


def _mix_prompt_kernel(*refs, n_chunks, scale):
    ret_in, lru_in = refs[:10], refs[10:19]
    o_ref, s_out_ref, lo_ref, h_out_ref, conv_out_ref = refs[19:24]
    s_ref, xs_ref, a_ref, b_ref, hc_ref = refs[24:]
    c = pl.program_id(1)
    sub = V7X_SUBLANES

    @pl.when(c == 0)
    def _():
        s_ref[...] = jnp.zeros_like(s_ref)
        xs_ref[0:sub, :] = jnp.zeros((sub, xs_ref.shape[1]), F32)
        hc_ref[...] = jnp.zeros_like(hc_ref)

    _ret_prompt_body(*ret_in, o_ref, s_ref, scale=scale)
    _lru_prompt_body(*lru_in, lo_ref, xs_ref, a_ref, b_ref, hc_ref)

    @pl.when(c == n_chunks - 1)
    def _():
        s_out_ref[0] = s_ref[...]
        h_out_ref[0] = hc_ref[0:1, :]
        conv_out_ref[0] = xs_ref[sub - (CONV_W - 1):sub, :]


def _lru_sample_kernel(xl_ref, yl_ref, conv0_ref, h0_ref, cw_ref, cb_ref, wa_ref, ba_ref,
                       wx_ref, bx_ref, sp_ref, lo_ref, h_out_ref, conv_out_ref, *, dt):
    db = h0_ref.shape[1]
    w = h0_ref.shape[2]
    taps = CONV_W - 1
    xp = [conv0_ref[0, :, i * w:(i + 1) * w] for i in range(taps)]
    xp += [xl_ref[t * db:(t + 1) * db, :] for t in range(dt)]
    xcs = []
    for t in range(dt):
        xc = cb_ref[...] + xp[t] * cw_ref[0:1, :]
        for i in range(1, CONV_W):
            xc = xc + xp[t + i] * cw_ref[i:i + 1, :]
        xcs.append(xc)
    a, bt = _lru_gates(jnp.concatenate(xcs, axis=0), wa_ref, ba_ref, wx_ref, bx_ref, sp_ref)
    h = h0_ref[0]
    for t in range(dt):
        rows = slice(t * db, (t + 1) * db)
        h = a[rows] * h + bt[rows]
        lo_ref[rows, :] = (jax.nn.gelu(yl_ref[rows, :]) * h).astype(lo_ref.dtype)
    h_out_ref[...] = h
    for i in range(taps):
        conv_out_ref[:, i * w:(i + 1) * w] = xp[dt + i]


def _lru_sample(proj, state_conv, state_lru, e, db, dt, w, cw, cb, wa, ba, wx, bx, sp):
    rows = db * dt
    assert proj.shape[0] == rows
    rb = 0
    xl_col = (proj.shape[1] - 2 * w) // w
    taps = CONV_W - 1
    conv0 = state_conv.reshape(state_conv.shape[0], db, taps * w)
    vec = lambda: pl.BlockSpec((1, w), lambda i: (0, 0))
    blocks = lambda: pl.BlockSpec(wa.shape, lambda i: (0, 0, 0))
    vmem = 16 * rows * w * 4
    lo, h_new, conv_new = pl.pallas_call(
        functools.partial(_lru_sample_kernel, dt=dt),
        grid=(1,),
        in_specs=[
            pl.BlockSpec((rows, w), lambda i: (rb, xl_col)),
            pl.BlockSpec((rows, w), lambda i: (rb, xl_col + 1)),
            pl.BlockSpec((1, db, taps * w), lambda i: (e, 0, 0)),
            pl.BlockSpec((1, db, w), lambda i: (e, 0, 0)),
            pl.BlockSpec((CONV_W, w), lambda i: (0, 0)),
            vec(), blocks(), vec(), blocks(), vec(), vec(),
        ],
        out_specs=[
            pl.BlockSpec((rows, w), lambda i: (0, 0)),
            pl.BlockSpec((db, w), lambda i: (0, 0)),
            pl.BlockSpec((db, taps * w), lambda i: (0, 0)),
        ],
        out_shape=[jax.ShapeDtypeStruct((rows, w), BF16),
                   jax.ShapeDtypeStruct((db, w), F32),
                   jax.ShapeDtypeStruct((db, taps * w), F32)],
        compiler_params=_params(("arbitrary",), vmem),
        name="lru_sample",
    )(proj, proj, conv0, state_lru, cw, cb.reshape(1, w), wa, ba.reshape(1, w), wx,
      bx.reshape(1, w), sp)
    return lo, h_new, conv_new.reshape(db, taps, w)


def _s5_tables(a_re, a_im, b_re, b_im, c_re, c_im, d, log_dt):
    g = a_re.shape[0]
    nb = g // SSM_BLOCK_GROUPS
    dt = jnp.exp(log_dt)[:, None]
    mag = jnp.exp(a_re * dt)
    abr = mag * jnp.cos(a_im * dt)
    abi = mag * jnp.sin(a_im * dt)
    den = a_re * a_re + a_im * a_im
    nr, ni = abr - 1.0, abi
    fr = (nr * a_re + ni * a_im) / den
    fi = (ni * a_re - nr * a_im) / den
    bbr = fr[..., None] * b_re - fi[..., None] * b_im
    bbi = fr[..., None] * b_im + fi[..., None] * b_re
    per_lane_block = V7X_LANES // SSM_P
    owner = (jnp.arange(SSM_BLOCK_GROUPS) % per_lane_block)[None, :, None, None, None]
    slot = jnp.arange(per_lane_block)[None, None, None, :, None]

    def pack(gkp):
        gkp = gkp.reshape(nb, SSM_BLOCK_GROUPS, SSM_GROUP, 1, SSM_P)
        return jnp.where(owner == slot, gkp, 0.0).reshape(nb, V7X_MXU_DIM, V7X_LANES)

    p_in = jnp.stack([pack(bbr.transpose(0, 2, 1)), pack(bbi.transpose(0, 2, 1))])
    p_out = jnp.stack([pack(c_re), pack(-c_im)])
    return (abr.reshape(nb, SSM_BLOCK_STATES), abi.reshape(nb, SSM_BLOCK_STATES),
            p_in, p_out, d.reshape(nb, 1, V7X_MXU_DIM))


def _expand_blockdiag(packed_ref, i, w_ref):
    ln = V7X_LANES
    per_lane_block = ln // SSM_P
    w_ref[...] = jnp.zeros_like(w_ref)
    for c in range(2):
        for g in range(SSM_BLOCK_GROUPS):
            rows = slice(g * SSM_GROUP, (g + 1) * SSM_GROUP)
            col = c * SSM_BLOCK_STATES + (g // per_lane_block) * ln
            w_ref[rows, col:col + ln] = packed_ref[c, i, rows, :].astype(w_ref.dtype)


def _s5_prompt_kernel(x_ref, g_ref, ar_ref, ai_ref, pin_ref, pout_ref, d_ref, y_ref,
                      hr_out_ref, hi_out_ref, win_ref, wout_ref, u_ref, sr_ref, si_ref,
                      hr_ref, hi_ref, *, n_chunks):
    c = pl.program_id(1)
    tc = x_ref.shape[0]
    nb = win_ref.shape[0]
    ns = SSM_BLOCK_STATES
    bw = V7X_MXU_DIM
    ln = V7X_LANES
    slots = V7X_SUBLANES
    nl = sr_ref.shape[0]
    splits = ns // (nl * ln)
    blocks_per_pass = slots // splits
    pitch = S5_ROW_PITCH

    @pl.when(c == 0)
    def _():
        hr_ref[...] = jnp.zeros_like(hr_ref)
        hi_ref[...] = jnp.zeros_like(hi_ref)
        for i in range(nb):
            _expand_blockdiag(pin_ref, i, win_ref.at[i])
            _expand_blockdiag(pout_ref, i, wout_ref.at[i])

    u_ref[...] = _rms(x_ref[...], g_ref[...])

    def slot_rows(slot):
        return pl.ds(slot, tc, stride=pitch)

    for p in range(nb // blocks_per_pass):
        blocks = range(p * blocks_per_pass, (p + 1) * blocks_per_pass)
        for il, i in enumerate(blocks):
            bu = _bdot(u_ref[:, i * bw:(i + 1) * bw].astype(BF16), win_ref[i])
            for sp in range(splits):
                for l in range(nl):
                    col = (sp * nl + l) * ln
                    sr_ref[l, slot_rows(il * splits + sp), :] = bu[:, col:col + ln]
                    si_ref[l, slot_rows(il * splits + sp), :] = bu[:, ns + col:ns + col + ln]

        srows = slice(p * slots, (p + 1) * slots)
        ar = [ar_ref[srows, l * ln:(l + 1) * ln] for l in range(nl)]
        ai = [ai_ref[srows, l * ln:(l + 1) * ln] for l in range(nl)]

        def body(t, carry):
            rows = pl.ds(t * pitch, slots)
            out = []
            for l in range(nl):
                hr, hi = carry[l]
                hr_n = ar[l] * hr - ai[l] * hi + sr_ref[l, rows, :]
                hi_n = ar[l] * hi + ai[l] * hr + si_ref[l, rows, :]
                sr_ref[l, rows, :] = hr_n
                si_ref[l, rows, :] = hi_n
                out.append((hr_n, hi_n))
            return tuple(out)

        init = tuple((hr_ref[srows, l * ln:(l + 1) * ln], hi_ref[srows, l * ln:(l + 1) * ln])
                     for l in range(nl))
        last = lax.fori_loop(0, tc, body, init, unroll=16)
        for l in range(nl):
            hr_ref[srows, l * ln:(l + 1) * ln] = last[l][0]
            hi_ref[srows, l * ln:(l + 1) * ln] = last[l][1]

        for il, i in enumerate(blocks):
            parts = [ref[l, slot_rows(il * splits + sp), :].astype(BF16)
                     for ref in (sr_ref, si_ref) for sp in range(splits) for l in range(nl)]
            cols = slice(i * bw, (i + 1) * bw)
            y = _bdot_nt(jnp.concatenate(parts, axis=-1), wout_ref[i]) + d_ref[i] * u_ref[:, cols]
            y_ref[:, cols] = jax.nn.gelu(y).astype(y_ref.dtype)

    @pl.when(c == n_chunks - 1)
    def _():
        hr_out_ref[0] = hr_ref[...]
        hi_out_ref[0] = hi_ref[...]


def _s5_prompt(x, g, tables, b, t):
    d = x.shape[1]
    abr, abi, p_in, p_out, dd = tables
    nb = abr.shape[0]
    ns = SSM_BLOCK_STATES
    w_shape = (nb, V7X_MXU_DIM, 2 * ns)
    tc = _pick_tile(t, 256, 16)
    nc = t // tc
    passes = 2
    assert nb % passes == 0 and V7X_SUBLANES % (nb // passes) == 0
    splits = V7X_SUBLANES // (nb // passes)
    slot_lanes = ns // splits
    nl = slot_lanes // V7X_LANES
    abr = abr.reshape(nb * splits, slot_lanes)
    abi = abi.reshape(nb * splits, slot_lanes)
    const = lambda a: _once(a.shape, lambda bi, ci: (0,) * a.ndim)
    scan_bytes = nl * tc * S5_ROW_PITCH * V7X_LANES * 4
    w_bytes = w_shape[0] * w_shape[1] * w_shape[2] * 2
    vmem = (5 * tc * d * 4 + 2 * tc * d * 2 + 2 * w_bytes + 2 * p_in.size * 4
            + 2 * scan_bytes + 6 * tc * 2 * ns * 4)
    y, hr, hi = pl.pallas_call(
        functools.partial(_s5_prompt_kernel, n_chunks=nc),
        grid=(b, nc),
        in_specs=[
            pl.BlockSpec((tc, d), lambda bi, ci: (bi * nc + ci, 0)),
            pl.BlockSpec((1, d), lambda bi, ci: (0, 0)),
            const(abr), const(abi), const(p_in), const(p_out), const(dd),
        ],
        out_specs=[
            pl.BlockSpec((tc, d), lambda bi, ci: (bi * nc + ci, 0)),
            pl.BlockSpec((1,) + abr.shape, lambda bi, ci: (bi, 0, 0)),
            pl.BlockSpec((1,) + abr.shape, lambda bi, ci: (bi, 0, 0)),
        ],
        out_shape=[jax.ShapeDtypeStruct((b * t, d), BF16),
                   jax.ShapeDtypeStruct((b,) + abr.shape, F32),
                   jax.ShapeDtypeStruct((b,) + abr.shape, F32)],
        scratch_shapes=[pltpu.VMEM(w_shape, BF16), pltpu.VMEM(w_shape, BF16),
                        pltpu.VMEM((tc, d), F32),
                        pltpu.VMEM((nl, tc * S5_ROW_PITCH, V7X_LANES), F32),
                        pltpu.VMEM((nl, tc * S5_ROW_PITCH, V7X_LANES), F32),
                        pltpu.VMEM(abr.shape, F32), pltpu.VMEM(abr.shape, F32)],
        compiler_params=_params(("parallel", "arbitrary"), vmem),
        name="s5_prompt",
    )(x, g.reshape(1, d), abr, abi, p_in, p_out, dd)
    return y, hr.reshape(b, nb * ns), hi.reshape(b, nb * ns)


def _s5_sample_kernel(x_ref, g_ref, ar_ref, ai_ref, pin_ref, pout_ref, d_ref, h0r_ref,
                      h0i_ref, y_ref, hr_out_ref, hi_out_ref, win_ref, wout_ref, uf_ref,
                      ub_ref, *, dt):
    i = pl.program_id(0)
    nb = uf_ref.shape[0]
    bw = V7X_MXU_DIM
    ns = SSM_BLOCK_STATES
    db = h0r_ref.shape[1]

    @pl.when(i == 0)
    def _():
        u = _rms(x_ref[...], g_ref[...])
        for n in range(nb):
            uf_ref[n] = u[:, n * bw:(n + 1) * bw]
            ub_ref[n] = u[:, n * bw:(n + 1) * bw].astype(BF16)

    _expand_blockdiag(pin_ref, 0, win_ref)
    _expand_blockdiag(pout_ref, 0, wout_ref)
    bu = _bdot(ub_ref[i], win_ref[...])
    ar, ai = ar_ref[pl.ds(i, 1), :], ai_ref[pl.ds(i, 1), :]
    hr, hi = h0r_ref[0], h0i_ref[0]
    states = []
    for t in range(dt):
        rows = slice(t * db, (t + 1) * db)
        hr, hi = (ar * hr - ai * hi + bu[rows, :ns], ar * hi + ai * hr + bu[rows, ns:])
        states.append(jnp.concatenate([hr.astype(BF16), hi.astype(BF16)], axis=-1))
    y = _bdot_nt(jnp.concatenate(states, axis=0), wout_ref[...]) + d_ref[0] * uf_ref[i]
    y_ref[...] = jax.nn.gelu(y).astype(y_ref.dtype)
    hr_out_ref[...] = hr
    hi_out_ref[...] = hi


def _s5_sample(x, g, tables, state_re, state_im, o, db, dt):
    d = x.shape[1]
    abr, abi, p_in, p_out, dd = tables
    nb = abr.shape[0]
    ns = SSM_BLOCK_STATES
    bw = V7X_MXU_DIM
    rows = db * dt
    assert x.shape[0] == rows
    rb = 0
    no = state_re.shape[0]
    h0r = state_re.reshape(no, db, nb * ns)
    h0i = state_im.reshape(no, db, nb * ns)
    blk = lambda a: pl.BlockSpec((1,) + a.shape[1:], lambda i: (i, 0, 0))
    packed = lambda a: pl.BlockSpec((2, 1) + a.shape[2:], lambda i: (0, i, 0, 0))
    vmem = (2 * rows * d * 4 + rows * d * 6 + 4 * bw * 2 * ns * 2 + 8 * db * ns * 4
            + 8 * rows * 2 * ns * 4)
    return pl.pallas_call(
        functools.partial(_s5_sample_kernel, dt=dt),
        grid=(nb,),
        in_specs=[
            _once((rows, d), lambda i: (rb, 0)),
            pl.BlockSpec((1, d), lambda i: (0, 0)),
            pl.BlockSpec(abr.shape, lambda i: (0, 0)),
            pl.BlockSpec(abi.shape, lambda i: (0, 0)),
            packed(p_in), packed(p_out), blk(dd),
            pl.BlockSpec((1, db, ns), lambda i: (o, 0, i)),
            pl.BlockSpec((1, db, ns), lambda i: (o, 0, i)),
        ],
        out_specs=[
            pl.BlockSpec((rows, bw), lambda i: (0, i)),
            pl.BlockSpec((db, ns), lambda i: (0, i)),
            pl.BlockSpec((db, ns), lambda i: (0, i)),
        ],
        out_shape=[jax.ShapeDtypeStruct((rows, d), BF16),
                   jax.ShapeDtypeStruct((db, nb * ns), F32),
                   jax.ShapeDtypeStruct((db, nb * ns), F32)],
        scratch_shapes=[pltpu.VMEM((bw, 2 * ns), BF16), pltpu.VMEM((bw, 2 * ns), BF16),
                        pltpu.VMEM((nb, rows, bw), F32), pltpu.VMEM((nb, rows, bw), BF16)],
        compiler_params=_params(("arbitrary",), vmem),
        name="s5_sample",
    )(x, g.reshape(1, d), abr, abi, p_in, p_out, dd, h0r, h0i)


def kernel(x_prompt, x_sample, state_ret, state_lru, state_conv, state_ssm_re, state_ssm_im, norm_mix_even, w_in_even, lru_conv_w, lru_conv_b, lru_wa, lru_ba, lru_wx, lru_bx, lru_lambda, w_out_even, norm_mix_odd, ssm_a_re, ssm_a_im, ssm_b_re, ssm_b_im, ssm_c_re, ssm_c_im, ssm_d, ssm_log_dt, w_glu, b_glu, norm_ffn, w_ffn_gu, w_ffn_down, norm_final):
    b, t, d = x_prompt.shape
    db, dt, _ = x_sample.shape
    depth = norm_ffn.shape[0]
    half = d // 2
    past_len = PAST_LEN
    groups, ssm_p = ssm_a_re.shape[1:]
    assert ssm_p == SSM_P and groups * SSM_GROUP == d and groups % SSM_BLOCK_GROUPS == 0

    xp = x_prompt.reshape(b * t, d)
    xs = x_sample.transpose(1, 0, 2).reshape(dt * db, d)

    rets_p, ret_s, lrus_p, lrus_s, convs_p, convs_s = [], None, [], [], [], []
    sres_p, sres_s, sims_p, sims_s = [], [], [], []
    for layer in range(depth):
        if layer % 2 == 0:
            e = layer // 2
            sp = jax.nn.softplus(-lru_lambda[e]).reshape(1, half)
            lru_w = (lru_conv_w[e], lru_conv_b[e], lru_wa[e], lru_ba[e], lru_wx[e], lru_bx[e], sp)

            proj_p = _inproj(xp, norm_mix_even[e], w_in_even, e)
            o_p, ret_p, lo_p, lru_p, conv_p = _mix_prompt(proj_p, b, t, half, *lru_w)
            xp = _outproj(o_p, lo_p, w_out_even, e, xp)

            proj_s = _inproj(xs, norm_mix_even[e], w_in_even, e)
            o_s, ret_s = _ret_sample(proj_s, state_ret, e, ret_s, db, dt, half, past_len)
            lo_s, lru_s, conv_s = _lru_sample(proj_s, state_conv, state_lru, e, db, dt, half,
                                              *lru_w)
            xs = _outproj(o_s, lo_s, w_out_even, e, xs)

            rets_p.append(ret_p)
            lrus_p.append(lru_p.reshape(b, half))
            lrus_s.append(lru_s)
            convs_p.append(conv_p)
            convs_s.append(conv_s)
        else:
            o = layer // 2
            tables = _s5_tables(ssm_a_re[o], ssm_a_im[o], ssm_b_re[o], ssm_b_im[o],
                                ssm_c_re[o], ssm_c_im[o], ssm_d[o], ssm_log_dt[o])
            y_p, sre_p, sim_p = _s5_prompt(xp, norm_mix_odd[o], tables, b, t)
            xp = _glu(y_p, w_glu, o, b_glu[o], xp)
            y_s, sre_s, sim_s = _s5_sample(xs, norm_mix_odd[o], tables, state_ssm_re,
                                           state_ssm_im, o, db, dt)
            xs = _glu(y_s, w_glu, o, b_glu[o], xs)
            sres_p.append(sre_p.reshape(b, groups, ssm_p))
            sims_p.append(sim_p.reshape(b, groups, ssm_p))
            sres_s.append(sre_s.reshape(db, groups, ssm_p))
            sims_s.append(sim_s.reshape(db, groups, ssm_p))
        g_final = norm_final if layer == depth - 1 else None
        xp = _ffn(xp, norm_ffn[layer], w_ffn_gu, w_ffn_down, layer, g_final)
        xs = _ffn(xs, norm_ffn[layer], w_ffn_gu, w_ffn_down, layer, g_final)

    y_prompt = xp.reshape(b, t, d)
    y_sample = xs.reshape(dt, db, d).transpose(1, 0, 2)
    return (y_prompt, y_sample, jnp.stack(rets_p), ret_s, jnp.stack(lrus_p),
            jnp.stack(lrus_s), jnp.stack(convs_p), jnp.stack(convs_s), jnp.stack(sres_p),
            jnp.stack(sres_s), jnp.stack(sims_p), jnp.stack(sims_s))
```
